```python
import jax, jax.numpy as jnp
from jax import lax
import numpy as np

D_MODEL = 1024
BATCH = 4
SEQ = 8192
DEPTH = 4
DEC_BATCH = 1
DEC_SEQ = 16384
PAST_LEN = 128

GRID_W = 64
HEAD_DIM = 64
NA_HEADS = D_MODEL // HEAD_DIM
NA_KH_MAX = 8
NA_KW = 16
NA_QCB = 16
NA_KCB = NA_QCB + NA_KW
SW_Q_HEADS = D_MODEL // HEAD_DIM
SW_KV_HEADS = 4
SW_WINDOW = 128
SW_BLOCK = 128
N_EXPERTS = 16
EXPERT_FF = 2 * D_MODEL
EC_FACTOR = 2
RMS_EPS = 1e-6
NEG = -1e30
N_LAYERS_A = (DEPTH + 1) // 2
N_LAYERS_B = DEPTH // 2

kernel_name = "hybrid_natten_swa_ec_encoder"


def rmsnorm(x, g):
    xf = x.astype(jnp.float32)
    y = xf * lax.rsqrt(jnp.mean(xf * xf, axis=-1, keepdims=True) + RMS_EPS)
    return (y * g.astype(jnp.float32)).astype(x.dtype)


def alibi_slopes(n):
    return jnp.asarray(2.0 ** (-8.0 * np.arange(1, n + 1) / n), dtype=jnp.float32)


def neighborhood_attention(x, w_qkv, w_o, rpb):
    B, S, D = x.shape
    rows = S // GRID_W
    kh = min(NA_KH_MAX, rows)
    nj = GRID_W // NA_QCB
    H, dh = NA_HEADS, HEAD_DIM
    q, k, v = jnp.split(x @ w_qkv, 3, axis=-1)
    shp = (B, rows, GRID_W, H, dh)
    q = q.reshape(shp) * (dh ** -0.5)
    k = k.reshape(shp)
    v = v.reshape(shp)
    qcol = np.arange(GRID_W).reshape(nj, NA_QCB)
    cs = np.clip(np.arange(nj) * NA_QCB - NA_KW // 2, 0, GRID_W - NA_KCB)
    kcol = cs[:, None] + np.arange(NA_KCB)
    c0 = np.clip(qcol - NA_KW // 2, 0, GRID_W - NA_KW)
    col_ok = (kcol[:, None, :] >= c0[..., None]) & (kcol[:, None, :] < c0[..., None] + NA_KW)
    coff = np.clip(kcol[:, None, :] - qcol[..., None], -(NA_KW - 1), NA_KW - 1) + NA_KW - 1
    rpb_c = rpb.astype(jnp.float32)[:, :, coff]
    col_ok = jnp.asarray(col_ok)[None, None, :, :, None, :]

    def row_step(r):
        rs = jnp.clip(r - kh // 2, 0, rows - kh)
        kr = lax.dynamic_slice_in_dim(k, rs, kh, axis=1)[:, :, kcol]
        vr = lax.dynamic_slice_in_dim(v, rs, kh, axis=1)[:, :, kcol]
        qr = lax.dynamic_index_in_dim(q, r, axis=1, keepdims=False).reshape(B, nj, NA_QCB, H, dh)
        s = jnp.einsum('bjqhd,bkjchd->bhjqkc', qr, kr).astype(jnp.float32)
        roff = rs + jnp.arange(kh) - r + NA_KH_MAX - 1
        bias = rpb_c[:, roff].transpose(0, 2, 3, 1, 4)
        s = jnp.where(col_ok, s + bias[None], NEG)
        p = jax.nn.softmax(s.reshape(B, H, nj, NA_QCB, kh * NA_KCB), axis=-1)
        p = p.reshape(B, H, nj, NA_QCB, kh, NA_KCB).astype(vr.dtype)
        o = jnp.einsum('bhjqkc,bkjchd->bjqhd', p, vr)
        return o.reshape(B, GRID_W, H * dh)

    out = lax.map(row_step, jnp.arange(rows))
    out = out.transpose(1, 0, 2, 3).reshape(B, S, H * dh)
    return out @ w_o


def sliding_window_gqa(x, w_qkv, w_o, sink):
    B, S, D = x.shape
    hq, hkv, dh = SW_Q_HEADS, SW_KV_HEADS, HEAD_DIM
    g = hq // hkv
    nb = S // SW_BLOCK
    qkv = x @ w_qkv
    q = qkv[..., :hq * dh].reshape(B, nb, SW_BLOCK, hkv, g, dh) * (dh ** -0.5)
    k = qkv[..., hq * dh:(hq + hkv) * dh].reshape(B, S, hkv, dh)
    v = qkv[..., (hq + hkv) * dh:].reshape(B, S, hkv, dh)
    pad = SW_WINDOW
    kp = jnp.pad(k, ((0, 0), (pad, pad), (0, 0), (0, 0)))
    vp = jnp.pad(v, ((0, 0), (pad, pad), (0, 0), (0, 0)))
    span = SW_BLOCK + 2 * pad
    ks = jnp.arange(span) - pad
    dist = ks[None, :] - jnp.arange(SW_BLOCK)[:, None]
    win_ok = jnp.abs(dist) <= SW_WINDOW
    slopes = alibi_slopes(hq).reshape(hkv, g)
    alibi = -slopes[:, :, None, None] * jnp.abs(dist).astype(jnp.float32)[None, None]
    sink_l = jnp.broadcast_to(sink.astype(jnp.float32).reshape(1, hkv, g, 1, 1), (B, hkv, g, SW_BLOCK, 1))

    def block_step(n):
        qb = lax.dynamic_index_in_dim(q, n, axis=1, keepdims=False)
        kb = lax.dynamic_slice_in_dim(kp, n * SW_BLOCK, span, axis=1)
        vb = lax.dynamic_slice_in_dim(vp, n * SW_BLOCK, span, axis=1)
        s = jnp.einsum('bqhgd,bshd->bhgqs', qb, kb).astype(jnp.float32) + alibi[None]
        pos = n * SW_BLOCK + ks
        valid = win_ok & ((pos >= 0) & (pos < S))[None, :]
        s = jnp.where(valid, s, NEG)
        p = jax.nn.softmax(jnp.concatenate([s, sink_l], axis=-1), axis=-1)[..., :span]
        o = jnp.einsum('bhgqs,bshd->bqhgd', p.astype(vb.dtype), vb)
        return o.reshape(B, SW_BLOCK, hq * dh)

    out = lax.map(block_step, jnp.arange(nb))
    out = out.transpose(1, 0, 2, 3).reshape(B, S, hq * dh)
    return out @ w_o


def expert_choice_ffn(x, w_router, w_gate, w_up, w_down):
    B, S, D = x.shape
    T = B * S
    cap = EC_FACTOR * T // N_EXPERTS
    xf = x.reshape(T, D)
    aff = jax.nn.softmax((xf @ w_router).astype(jnp.float32), axis=-1)
    gates, idx = lax.top_k(aff.T, cap)

    def expert_step(args):
        ix, gt, wg, wu, wd = args
        xe = xf[ix]
        h = jax.nn.silu(xe @ wg) * (xe @ wu)
        return (h @ wd) * gt[:, None].astype(xe.dtype)

    ye = lax.map(expert_step, (idx, gates, w_gate, w_up, w_down))
    y = jnp.zeros_like(xf).at[idx.reshape(-1)].add(ye.reshape(-1, D))
    return y.reshape(B, S, D)


def trunk(x, norm_mix, norm_ffn, norm_final, a_wqkv, a_wo, a_rpb, b_wqkv, b_wo, b_sink,
          w_router, w_gate, w_up, w_down):
    for i in range(DEPTH):
        h = rmsnorm(x, norm_mix[i])
        j = i // 2
        if i % 2 == 0:
            x = x + neighborhood_attention(h, a_wqkv[j], a_wo[j], a_rpb[j])
        else:
            x = x + sliding_window_gqa(h, b_wqkv[j], b_wo[j], b_sink[j])
        h = rmsnorm(x, norm_ffn[i])
        x = x + expert_choice_ffn(h, w_router[i], w_gate[i], w_up[i], w_down[i])
    return rmsnorm(x, norm_final)


def setup_inputs(seed: int = 0) -> dict:
    key = jax.random.key(seed)
    ks = jax.random.split(key, 16)
    D, F, E = D_MODEL, EXPERT_FF, N_EXPERTS
    qkv_b = (SW_Q_HEADS + 2 * SW_KV_HEADS) * HEAD_DIM
    nrm = lambda k, shp, s: jax.random.normal(k, shp, jnp.float32) * s
    return {
        "x_prompt": nrm(ks[0], (BATCH, SEQ, D), 1.0),
        "x_sample": nrm(ks[1], (DEC_BATCH, DEC_SEQ, D), 1.0),
        "norm_mix": 1.0 + nrm(ks[2], (DEPTH, D), 0.01),
        "norm_ffn": 1.0 + nrm(ks[3], (DEPTH, D), 0.01),
        "norm_final": 1.0 + nrm(ks[4], (D,), 0.01),
        "a_wqkv": nrm(ks[5], (N_LAYERS_A, D, 3 * NA_HEADS * HEAD_DIM), D ** -0.5),
        "a_wo": nrm(ks[6], (N_LAYERS_A, NA_HEADS * HEAD_DIM, D), D ** -0.5),
        "a_rpb": nrm(ks[7], (N_LAYERS_A, NA_HEADS, 2 * NA_KH_MAX - 1, 2 * NA_KW - 1), 0.5),
        "b_wqkv": nrm(ks[8], (N_LAYERS_B, D, qkv_b), D ** -0.5),
        "b_wo": nrm(ks[9], (N_LAYERS_B, SW_Q_HEADS * HEAD_DIM, D), D ** -0.5),
        "b_sink": nrm(ks[10], (N_LAYERS_B, SW_Q_HEADS), 0.5),
        "w_router": nrm(ks[11], (DEPTH, D, E), D ** -0.5),
        "w_gate": nrm(ks[12], (DEPTH, E, D, F), D ** -0.5),
        "w_up": nrm(ks[13], (DEPTH, E, D, F), D ** -0.5),
        "w_down": nrm(ks[14], (DEPTH, E, F, D), F ** -0.5),
    }


def reference(x_prompt, x_sample, norm_mix, norm_ffn, norm_final, a_wqkv, a_wo, a_rpb,
              b_wqkv, b_wo, b_sink, w_router, w_gate, w_up, w_down):
    y_prompt = trunk(x_prompt, norm_mix, norm_ffn, norm_final, a_wqkv, a_wo, a_rpb,
                     b_wqkv, b_wo, b_sink, w_router, w_gate, w_up, w_down)
    y_sample = trunk(x_sample, norm_mix, norm_ffn, norm_final, a_wqkv, a_wo, a_rpb,
                     b_wqkv, b_wo, b_sink, w_router, w_gate, w_up, w_down)
    return (y_prompt, y_sample)
```

```python
import functools

import numpy as np
import jax
import jax.numpy as jnp
from jax import lax
from jax.experimental import pallas as pl
from jax.experimental.pallas import tpu as pltpu

F32 = jnp.float32
BF16 = jnp.bfloat16
I32 = jnp.int32

D_MODEL = 1024
HEAD_DIM = 64
N_HEADS = 16
LANES = 128
SUBLANES = 8
N_PLANES = D_MODEL // LANES
COL_BLOCKS = D_MODEL // LANES
GRID_W = 64
NA_KH = 8
NA_KW = 16
NA_UNIT = 4 * GRID_W
NA_UNITS_PER_STEP = 4
SW_KV_HEADS = 4
SW_WINDOW = 128
SW_UNIT = 128
SW_UNITS_PER_STEP = 4
N_EXPERTS = 16
EXPERT_FF = 2048
FF_CHUNK = 512
ROUTE_TILE = 256
SLAB = 64
RMS_EPS = 1e-6
NEG = -1e30
ROW_TILE = 512
VMEM_LIMIT = 56 * 1024 * 1024


def _cparams(*sem):
    return pltpu.CompilerParams(dimension_semantics=sem, vmem_limit_bytes=VMEM_LIMIT)


def _rmsnorm(x, g):
    return x * lax.rsqrt(jnp.mean(x * x, axis=-1, keepdims=True) + RMS_EPS) * g


def _qkv_kernel(x_ref, g_ref, w_ref, o_ref):
    y = _rmsnorm(x_ref[...], g_ref[...]).astype(BF16)
    r = jnp.dot(y, w_ref[...], preferred_element_type=F32).astype(BF16)
    for p in range(o_ref.shape[0]):
        o_ref[p] = r[:, p * LANES:(p + 1) * LANES]


def _qkv_call(x, g, w):
    t, d = x.shape
    n = w.shape[1]
    tm = min(ROW_TILE, t)
    return pl.pallas_call(
        _qkv_kernel,
        grid=(t // tm,),
        in_specs=[pl.BlockSpec((tm, d), lambda i: (i, 0)),
                  pl.BlockSpec((1, d), lambda i: (0, 0)),
                  pl.BlockSpec((d, n), lambda i: (0, 0))],
        out_specs=pl.BlockSpec((n // LANES, tm, LANES), lambda i: (0, i, 0)),
        out_shape=jax.ShapeDtypeStruct((n // LANES, t, LANES), BF16),
        compiler_params=_cparams("parallel"),
    )(x, g, w)


def _edge_case(u, n_units):
    i = pl.program_id(2)
    last = pl.num_programs(2) - 1
    if n_units == 1:
        return jnp.where(i == 0, 1, jnp.where(i == last, 2, 0))
    if u == 0:
        return jnp.where(i == 0, 1, 0)
    if u == n_units - 1:
        return jnp.where(i == last, 2, 0)
    return 0


def _two_head_attention(q, kc, vc, bias_of, sink_of):
    lo_half = lax.broadcasted_iota(I32, q.shape, 1) < HEAD_DIM
    outs = []
    for hh in range(2):
        qm = jnp.where(lo_half if hh == 0 else jnp.logical_not(lo_half), q, jnp.zeros_like(q))
        s = lax.dot_general(qm, kc, (((1,), (1,)), ((), ())), preferred_element_type=F32)
        s = s + bias_of(hh)
        m = jnp.max(s, axis=1, keepdims=True)
        sink = sink_of(hh)
        if sink is not None:
            m = jnp.maximum(m, sink)
        e = jnp.exp(s - m)
        l = jnp.sum(e, axis=1, keepdims=True)
        if sink is not None:
            l = l + jnp.exp(sink - m)
        outs.append(jnp.dot(e.astype(BF16), vc, preferred_element_type=F32) / l)
    return jnp.where(lo_half, outs[0], outs[1]).astype(BF16)


def _na_kernel(*refs, n_units):
    q_ref = refs[0]
    k_refs = refs[1:n_units + 3]
    v_refs = refs[n_units + 3:2 * n_units + 5]
    bias_ref = refs[2 * n_units + 5]
    o_ref = refs[2 * n_units + 6]
    for u in range(n_units):
        case = _edge_case(u, n_units)
        kc = jnp.concatenate([k_refs[u + j][0] for j in range(3)], axis=0)
        vc = jnp.concatenate([v_refs[u + j][0] for j in range(3)], axis=0)
        q = q_ref[0, u * NA_UNIT:(u + 1) * NA_UNIT, :]
        o_ref[0, u * NA_UNIT:(u + 1) * NA_UNIT, :] = _two_head_attention(
            q, kc, vc, lambda hh: bias_ref[0, case, hh], lambda hh: None)


def _na_call(qkv, bias, batch, seq):
    t = batch * seq
    units = seq // NA_UNIT
    n_units = min(NA_UNITS_PER_STEP, units)
    steps = units // n_units
    assert seq % (NA_UNIT * n_units) == 0 and units >= 4
    qb = NA_UNIT * n_units

    def unit_map(plane0, j):
        return lambda hp, b, i: (plane0 + hp, b * units + jnp.clip(n_units * i - 1 + j, 0, units - 1), 0)

    in_specs = [pl.BlockSpec((1, qb, LANES), lambda hp, b, i: (hp, b * steps + i, 0))]
    in_specs += [pl.BlockSpec((1, NA_UNIT, LANES), unit_map(N_PLANES, j)) for j in range(n_units + 2)]
    in_specs += [pl.BlockSpec((1, NA_UNIT, LANES), unit_map(2 * N_PLANES, j)) for j in range(n_units + 2)]
    in_specs += [pl.BlockSpec((1, 3, 2, NA_UNIT, 3 * NA_UNIT), lambda hp, b, i: (hp, 0, 0, 0, 0))]
    return pl.pallas_call(
        functools.partial(_na_kernel, n_units=n_units),
        grid=(N_PLANES, batch, steps),
        in_specs=in_specs,
        out_specs=pl.BlockSpec((1, qb, LANES), lambda hp, b, i: (hp, b * steps + i, 0)),
        out_shape=jax.ShapeDtypeStruct((N_PLANES, t, LANES), BF16),
        compiler_params=_cparams("arbitrary", "arbitrary", "arbitrary"),
    )(*([qkv] * (2 * n_units + 5)), bias)


def _na_bias_table(rpb):
    a = np.arange(NA_UNIT) // GRID_W
    qc = np.arange(NA_UNIT) % GRID_W
    j = np.arange(3 * NA_UNIT) // GRID_W
    kc = np.arange(3 * NA_UNIT) % GRID_W
    dr = np.clip(j[None, :] - a[:, None] + 3, 0, 2 * NA_KH - 2)
    dc = np.clip(kc[None, :] - qc[:, None], -(NA_KW - 1), NA_KW - 1) + NA_KW - 1
    c0 = np.clip(qc - NA_KW // 2, 0, GRID_W - NA_KW)
    col_ok = (kc[None, :] >= c0[:, None]) & (kc[None, :] < c0[:, None] + NA_KW)
    row_ok = np.stack([
        (j[None, :] >= a[:, None]) & (j[None, :] < a[:, None] + NA_KH),
        np.broadcast_to((j[None, :] >= 4) & (j[None, :] < 4 + NA_KH), (NA_UNIT, 3 * NA_UNIT)),
        np.broadcast_to(j[None, :] < NA_KH, (NA_UNIT, 3 * NA_UNIT)),
    ])
    ok = jnp.asarray(row_ok & col_ok[None])
    tbl = rpb.astype(F32)[:, dr, dc]
    full = jnp.where(ok[:, None], tbl[None], NEG)
    full = full.reshape(3, N_PLANES, 2, NA_UNIT, 3 * NA_UNIT)
    return full.transpose(1, 0, 2, 3, 4)


SW_HEAD_ORDER = np.array([8 * m + 4 * hh + i for m in range(2) for i in range(4) for hh in range(2)])


def _sw_kernel(*refs, n_units):
    q_ref = refs[0]
    k_refs = refs[1:n_units + 3]
    v_refs = refs[n_units + 3:2 * n_units + 5]
    bias_ref = refs[2 * n_units + 5]
    sink_ref = refs[2 * n_units + 6]
    o_ref = refs[2 * n_units + 7]
    for u in range(n_units):
        case = _edge_case(u, n_units)
        kc = jnp.concatenate([k_refs[u + j][0] for j in range(3)], axis=0)
        vc = jnp.concatenate([v_refs[u + j][0] for j in range(3)], axis=0)
        for pi in range(4):
            q = q_ref[pi, u * SW_UNIT:(u + 1) * SW_UNIT, :]
            o_ref[pi, u * SW_UNIT:(u + 1) * SW_UNIT, :] = _two_head_attention(
                q, kc, vc,
                lambda hh: bias_ref[0, case, 2 * pi + hh],
                lambda hh: sink_ref[0, 2 * pi + hh:2 * pi + hh + 1, 0:1])


def _sw_call(qkv, bias, sink, batch, seq):
    t = batch * seq
    units = seq // SW_UNIT
    n_units = min(SW_UNITS_PER_STEP, units)
    steps = units // n_units
    assert seq % (SW_UNIT * n_units) == 0 and units >= 2
    qb = SW_UNIT * n_units

    def unit_map(plane0):
        def make(j):
            return lambda m, b, i: (plane0 + m, b * units + jnp.clip(n_units * i - 1 + j, 0, units - 1), 0)
        return make

    in_specs = [pl.BlockSpec((4, qb, LANES), lambda m, b, i: (m, b * steps + i, 0))]
    in_specs += [pl.BlockSpec((1, SW_UNIT, LANES), unit_map(N_PLANES)(j)) for j in range(n_units + 2)]
    in_specs += [pl.BlockSpec((1, SW_UNIT, LANES), unit_map(N_PLANES + 2)(j)) for j in range(n_units + 2)]
    in_specs += [pl.BlockSpec((1, 3, 8, SW_UNIT, 3 * SW_UNIT), lambda m, b, i: (m, 0, 0, 0, 0)),
                 pl.BlockSpec((1, 8, LANES), lambda m, b, i: (m, 0, 0))]
    return pl.pallas_call(
        functools.partial(_sw_kernel, n_units=n_units),
        grid=(2, batch, steps),
        in_specs=in_specs,
        out_specs=pl.BlockSpec((4, qb, LANES), lambda m, b, i: (m, b * steps + i, 0)),
        out_shape=jax.ShapeDtypeStruct((N_PLANES, t, LANES), BF16),
        compiler_params=_cparams("arbitrary", "arbitrary", "arbitrary"),
    )(*([qkv] * (2 * n_units + 5)), bias, sink)


def _sw_bias_table():
    slopes = np.asarray(2.0 ** (-8.0 * np.arange(1, N_HEADS + 1) / N_HEADS), dtype=np.float32)
    ks = np.arange(3 * SW_UNIT) - SW_WINDOW
    dist = ks[None, :] - np.arange(SW_UNIT)[:, None]
    win_ok = np.abs(dist) <= SW_WINDOW
    pos_ok = np.stack([np.ones(3 * SW_UNIT, bool), ks >= 0, ks < SW_UNIT])
    ok = win_ok[None] & pos_ok[:, None, :]
    alibi = -slopes[SW_HEAD_ORDER][:, None, None] * np.abs(dist).astype(np.float32)[None]
    full = np.where(ok[:, None], alibi[None], np.float32(NEG)).astype(np.float32)
    return jnp.asarray(full.reshape(3, 2, 8, SW_UNIT, 3 * SW_UNIT).transpose(1, 0, 2, 3, 4))


def _post_kernel(o_ref, x_ref, wo_ref, g_ref, wrh_ref, wrl_ref, x1_ref, h_ref, aff_ref, afft_ref):
    oc = jnp.concatenate([o_ref[p] for p in range(N_PLANES)], axis=1)
    x1 = x_ref[...] + jnp.dot(oc, wo_ref[...], preferred_element_type=F32)
    x1_ref[...] = x1
    h = _rmsnorm(x1, g_ref[...])
    hb = h.astype(BF16)
    h_ref[...] = hb
    hl = (h - hb.astype(F32)).astype(BF16)
    logits = jnp.dot(hb, wrh_ref[...], preferred_element_type=F32) + (
        jnp.dot(hb, wrl_ref[...], preferred_element_type=F32)
        + jnp.dot(hl, wrh_ref[...], preferred_element_type=F32))
    lane = lax.broadcasted_iota(I32, logits.shape, 1)
    logits = jnp.where(lane < N_EXPERTS, logits, NEG)
    ex = jnp.exp(logits - jnp.max(logits, axis=1, keepdims=True))
    aff = ex / jnp.sum(ex, axis=1, keepdims=True)
    aff_ref[...] = aff
    afft_ref[...] = aff.T[:N_EXPERTS, :]


def _post_call(o, x, wo, g, wrh, wrl):
    t, d = x.shape
    tm = min(ROW_TILE, t)
    return pl.pallas_call(
        _post_kernel,
        grid=(t // tm,),
        in_specs=[pl.BlockSpec((N_PLANES, tm, LANES), lambda i: (0, i, 0)),
                  pl.BlockSpec((tm, d), lambda i: (i, 0)),
                  pl.BlockSpec((d, d), lambda i: (0, 0)),
                  pl.BlockSpec((1, d), lambda i: (0, 0)),
                  pl.BlockSpec((d, LANES), lambda i: (0, 0)),
                  pl.BlockSpec((d, LANES), lambda i: (0, 0))],
        out_specs=[pl.BlockSpec((tm, d), lambda i: (i, 0)),
                   pl.BlockSpec((tm, d), lambda i: (i, 0)),
                   pl.BlockSpec((tm, LANES), lambda i: (i, 0)),
                   pl.BlockSpec((N_EXPERTS, tm), lambda i: (0, i))],
        out_shape=[jax.ShapeDtypeStruct((t, d), F32),
                   jax.ShapeDtypeStruct((t, d), BF16),
                   jax.ShapeDtypeStruct((t, LANES), F32),
                   jax.ShapeDtypeStruct((N_EXPERTS, t), F32)],
        compiler_params=_cparams("parallel"),
    )(o, x, wo, g, wrh, wrl)


def _threshold_kernel(aff_ref, thr_ref, need_ref, *, cap):
    bits = lax.bitcast_convert_type(aff_ref[...], I32)

    def body(k, cur):
        cand = cur | jnp.left_shift(jnp.int32(1), 30 - k)
        cnt = jnp.sum(jnp.where(bits >= cand, 1, 0), axis=1, keepdims=True)
        return jnp.where(cnt >= cap, cand, cur)

    cur = lax.fori_loop(0, 31, body, jnp.zeros((N_EXPERTS, 1), I32))
    above = jnp.sum(jnp.where(bits > cur, 1, 0), axis=1, keepdims=True)
    thr_ref[...] = jnp.broadcast_to(cur, thr_ref.shape)
    need_ref[...] = jnp.broadcast_to(cap - above, need_ref.shape)


def _threshold_call(afft, cap):
    e, t = afft.shape
    return pl.pallas_call(
        functools.partial(_threshold_kernel, cap=cap),
        grid=(1,),
        in_specs=[pl.BlockSpec((e, t), lambda i: (0, 0))],
        out_specs=[pl.BlockSpec((e, LANES), lambda i: (0, 0)),
                   pl.BlockSpec((e, LANES), lambda i: (0, 0))],
        out_shape=[jax.ShapeDtypeStruct((e, LANES), I32),
                   jax.ShapeDtypeStruct((e, LANES), I32)],
        compiler_params=_cparams("arbitrary"),
    )(afft)


def _positions_kernel(aff_ref, thr_ref, need_ref, lpos_ref, lpost_ref, starts_ref, npass_ref, base_sc, eqb_sc):
    @pl.when(pl.program_id(0) == 0)
    def _():
        base_sc[...] = jnp.zeros_like(base_sc)
        eqb_sc[...] = jnp.zeros_like(eqb_sc)

    bits = lax.bitcast_convert_type(aff_ref[...], I32)
    thr = thr_ref[:, 0:1]
    need = need_ref[:, 0:1].astype(F32)
    r = lax.broadcasted_iota(I32, (ROUTE_TILE, ROUTE_TILE), 0)
    c = lax.broadcasted_iota(I32, (ROUTE_TILE, ROUTE_TILE), 1)
    before = jnp.where(r < c, 1.0, 0.0).astype(BF16)
    eq = bits == thr
    eqf = jnp.where(eq, 1.0, 0.0)
    eq_rank = eqb_sc[:, 0:1] + jnp.dot(eqf.astype(BF16), before, preferred_element_type=F32)
    sel = (bits > thr) | (eq & (eq_rank < need))
    self_ = jnp.where(sel, 1.0, 0.0)
    rank = jnp.dot(self_.astype(BF16), before, preferred_element_type=F32)
    cnt = jnp.sum(self_, axis=1, keepdims=True)
    lpos = jnp.where(sel, rank, -1.0)
    lpos_ref[...] = lpos.astype(I32)
    pad = jnp.full((LANES - N_EXPERTS, ROUTE_TILE), -1.0, F32)
    lpost_ref[...] = jnp.concatenate([lpos, pad], axis=0).T
    starts_ref[0] = base_sc[...].astype(I32)
    most = jnp.max(cnt, axis=0, keepdims=True)
    npass_ref[0] = jnp.broadcast_to(jnp.floor((most + (SLAB - 1)) * (1.0 / SLAB)), npass_ref.shape[1:]).astype(I32)
    base_sc[...] = base_sc[...] + cnt
    eqb_sc[...] = eqb_sc[...] + jnp.sum(eqf, axis=1, keepdims=True)


def _positions_call(afft, thr, need):
    e, t = afft.shape
    n_tiles = t // ROUTE_TILE
    return pl.pallas_call(
        _positions_kernel,
        grid=(n_tiles,),
        in_specs=[pl.BlockSpec((e, ROUTE_TILE), lambda i: (0, i)),
                  pl.BlockSpec((e, LANES), lambda i: (0, 0)),
                  pl.BlockSpec((e, LANES), lambda i: (0, 0))],
        out_specs=[pl.BlockSpec((e, ROUTE_TILE), lambda i: (0, i)),
                   pl.BlockSpec((ROUTE_TILE, LANES), lambda i: (i, 0)),
                   pl.BlockSpec((1, e, LANES), lambda i: (i, 0, 0)),
                   pl.BlockSpec((1, SUBLANES, LANES), lambda i: (i, 0, 0))],
        out_shape=[jax.ShapeDtypeStruct((e, t), I32),
                   jax.ShapeDtypeStruct((t, LANES), F32),
                   jax.ShapeDtypeStruct((n_tiles, e, LANES), I32),
                   jax.ShapeDtypeStruct((n_tiles, SUBLANES, LANES), I32)],
        scratch_shapes=[pltpu.VMEM((e, LANES), F32), pltpu.VMEM((e, LANES), F32)],
        compiler_params=_cparams("arbitrary"),
    )(afft, thr, need)


SLAB_ROWS = SLAB * COL_BLOCKS


def _dispatch_kernel(starts_sm, npass_sm, h_ref, lpos_ref, xe_ref, slab_sc, sem, *, cap):
    i = pl.program_id(0)
    sub = lax.broadcasted_iota(I32, (SLAB, ROUTE_TILE), 0)

    @pl.when(i == 0)
    def _():
        slab_sc[pl.ds(0, SLAB_ROWS), :] = jnp.zeros((SLAB_ROWS, LANES), F32)
        pads = [pltpu.make_async_copy(slab_sc.at[pl.ds(0, SLAB_ROWS)],
                                      xe_ref.at[e, pl.ds(cap * COL_BLOCKS, SLAB_ROWS)], sem.at[e])
                for e in range(N_EXPERTS)]
        for cp in pads:
            cp.start()
        for cp in pads:
            cp.wait()

    def one_pass(s, carry):
        rows = []
        for e in range(N_EXPERTS):
            lp = lpos_ref[e:e + 1, :] - SLAB * s
            rows.append(jnp.where(lp == sub, 1.0, 0.0).astype(BF16))
        onehot = jnp.concatenate(rows, axis=0)
        res = jnp.dot(onehot, h_ref[...], preferred_element_type=F32)
        for cb in range(COL_BLOCKS):
            slab_sc[pl.ds(cb, N_EXPERTS * SLAB, stride=COL_BLOCKS), :] = res[:, cb * LANES:(cb + 1) * LANES]
        copies = []
        for e in range(N_EXPERTS):
            dst = jnp.minimum(starts_sm[i * N_EXPERTS + e] + SLAB * s, cap)
            cp = pltpu.make_async_copy(
                slab_sc.at[pl.ds(e * SLAB_ROWS, SLAB_ROWS)],
                xe_ref.at[e, pl.ds(pl.multiple_of(dst * COL_BLOCKS, COL_BLOCKS), SLAB_ROWS)],
                sem.at[e])
            cp.start()
            copies.append(cp)
        for cp in copies:
            cp.wait()
        return carry

    lax.fori_loop(0, npass_sm[i], one_pass, 0)


def _dispatch_call(starts, npass, h, lpos, cap):
    t, d = h.shape
    n_tiles = t // ROUTE_TILE
    grid_spec = pltpu.PrefetchScalarGridSpec(
        num_scalar_prefetch=2,
        grid=(n_tiles,),
        in_specs=[pl.BlockSpec((ROUTE_TILE, d), lambda i, *_: (i, 0)),
                  pl.BlockSpec((N_EXPERTS, ROUTE_TILE), lambda i, *_: (0, i))],
        out_specs=pl.BlockSpec(memory_space=pl.ANY),
        scratch_shapes=[pltpu.VMEM((N_EXPERTS * SLAB_ROWS, LANES), F32),
                        pltpu.SemaphoreType.DMA((N_EXPERTS,))],
    )
    return pl.pallas_call(
        functools.partial(_dispatch_kernel, cap=cap),
        grid_spec=grid_spec,
        out_shape=jax.ShapeDtypeStruct((N_EXPERTS, (cap + SLAB) * COL_BLOCKS, LANES), F32),
        compiler_params=_cparams("arbitrary"),
    )(starts, npass, h, lpos)


def _expert_kernel(x_ref, wg_ref, wu_ref, wd_ref, o_ref):
    tm = x_ref.shape[1] // COL_BLOCKS
    x = jnp.concatenate([x_ref[0, pl.ds(cb, tm, stride=COL_BLOCKS), :] for cb in range(COL_BLOCKS)],
                        axis=1).astype(BF16)
    acc = jnp.zeros((tm, D_MODEL), F32)
    for fc in range(EXPERT_FF // FF_CHUNK):
        fs = slice(fc * FF_CHUNK, (fc + 1) * FF_CHUNK)
        g = jnp.dot(x, wg_ref[0, :, fs], preferred_element_type=F32)
        u = jnp.dot(x, wu_ref[0, :, fs], preferred_element_type=F32)
        hmid = (g * (1.0 / (1.0 + jnp.exp(-g))) * u).astype(BF16)
        acc = acc + jnp.dot(hmid, wd_ref[0, fs, :], preferred_element_type=F32)
    for cb in range(COL_BLOCKS):
        o_ref[0, pl.ds(cb, tm, stride=COL_BLOCKS), :] = acc[:, cb * LANES:(cb + 1) * LANES]


def _expert_call(xe, wg, wu, wd, cap):
    tm = min(ROW_TILE, cap)
    d, f = wg.shape[1], wg.shape[2]
    return pl.pallas_call(
        _expert_kernel,
        grid=(N_EXPERTS, cap // tm),
        in_specs=[pl.BlockSpec((1, tm * COL_BLOCKS, LANES), lambda e, j: (e, j, 0)),
                  pl.BlockSpec((1, d, f), lambda e, j: (e, 0, 0)),
                  pl.BlockSpec((1, d, f), lambda e, j: (e, 0, 0)),
                  pl.BlockSpec((1, f, d), lambda e, j: (e, 0, 0))],
        out_specs=pl.BlockSpec((1, tm * COL_BLOCKS, LANES), lambda e, j: (e, j, 0)),
        out_shape=jax.ShapeDtypeStruct((N_EXPERTS, cap * COL_BLOCKS, LANES), F32),
        compiler_params=_cparams("parallel", "parallel"),
    )(xe, wg, wu, wd)


def _combine_kernel(starts_sm, npass_sm, x_ref, aff_ref, lpost_ref, expand_ref, g_ref, ye_ref, o_ref,
                    slab_sc, sem, *, cap, final):
    i = pl.program_id(0)
    lane = lax.broadcasted_iota(I32, (1, LANES), 1)
    slot_in_slab = (lax.broadcasted_iota(I32, (1, N_EXPERTS * SLAB), 1) % SLAB).astype(F32)
    aff = aff_ref[...]
    ghi = aff.astype(BF16)
    glo = (aff - ghi.astype(F32)).astype(BF16)
    expand = expand_ref[...]
    ghi_x = jnp.dot(ghi, expand, preferred_element_type=F32)
    glo_x = jnp.dot(glo, expand, preferred_element_type=F32)
    lpost = lpost_ref[...]

    def one_pass(s, acc):
        copies = []
        shift = jnp.zeros((1, LANES), F32)
        for e in range(N_EXPERTS):
            want = starts_sm[i * N_EXPERTS + e] + SLAB * s
            src = jnp.minimum(want, cap - SLAB)
            shift = jnp.where(lane == e, (want - src).astype(F32), shift)
            cp = pltpu.make_async_copy(
                ye_ref.at[e, pl.ds(pl.multiple_of(src * COL_BLOCKS, COL_BLOCKS), SLAB_ROWS)],
                slab_sc.at[pl.ds(e * SLAB_ROWS, SLAB_ROWS)],
                sem.at[e])
            cp.start()
            copies.append(cp)
        lo = (SLAB * s).astype(F32)
        owned = (lpost >= lo) & (lpost < lo + SLAB)
        row = jnp.where(owned, lpost - lo + shift, -1.0).astype(BF16)
        row_x = jnp.dot(row, expand, preferred_element_type=F32)
        match = row_x == slot_in_slab
        whi = jnp.where(match, ghi_x, 0.0).astype(BF16)
        wlo = jnp.where(match, glo_x, 0.0).astype(BF16)
        for cp in copies:
            cp.wait()
        rows = jnp.concatenate(
            [slab_sc[pl.ds(cb, N_EXPERTS * SLAB, stride=COL_BLOCKS), :] for cb in range(COL_BLOCKS)],
            axis=1).astype(BF16)
        return acc + (jnp.dot(whi, rows, preferred_element_type=F32)
                      + jnp.dot(wlo, rows, preferred_element_type=F32))

    y = lax.fori_loop(0, npass_sm[i], one_pass, jnp.zeros(x_ref.shape, F32))
    out = x_ref[...] + y
    if final:
        out = _rmsnorm(out, g_ref[...])
    o_ref[...] = out


def _combine_call(starts, npass, x1, aff, lpost, expand, g, ye, cap, final):
    t, d = x1.shape
    n_tiles = t // ROUTE_TILE
    grid_spec = pltpu.PrefetchScalarGridSpec(
        num_scalar_prefetch=2,
        grid=(n_tiles,),
        in_specs=[pl.BlockSpec((ROUTE_TILE, d), lambda i, *_: (i, 0)),
                  pl.BlockSpec((ROUTE_TILE, LANES), lambda i, *_: (i, 0)),
                  pl.BlockSpec((ROUTE_TILE, LANES), lambda i, *_: (i, 0)),
                  pl.BlockSpec((LANES, N_EXPERTS * SLAB), lambda i, *_: (0, 0)),
                  pl.BlockSpec((1, d), lambda i, *_: (0, 0)),
                  pl.BlockSpec(memory_space=pl.ANY)],
        out_specs=pl.BlockSpec((ROUTE_TILE, d), lambda i, *_: (i, 0)),
        scratch_shapes=[pltpu.VMEM((N_EXPERTS * SLAB_ROWS, LANES), F32),
                        pltpu.SemaphoreType.DMA((N_EXPERTS,))],
    )
    return pl.pallas_call(
        functools.partial(_combine_kernel, cap=cap, final=final),
        grid_spec=grid_spec,
        out_shape=jax.ShapeDtypeStruct((t, d), F32),
        compiler_params=_cparams("arbitrary"),
    )(starts, npass, x1, aff, lpost, expand, g, ye)


def _moe(x1, h, aff, afft, wg, wu, wd, expand, g_final, final):
    t = x1.shape[0]
    cap = 2 * t // N_EXPERTS
    assert t % ROUTE_TILE == 0 and cap >= SLAB
    thr, need = _threshold_call(afft, cap)
    lpos, lpost, starts3, npass3 = _positions_call(afft, thr, need)
    starts = starts3[:, :, 0].reshape(-1)
    npass = npass3[:, 0, 0]
    xe = _dispatch_call(starts, npass, h, lpos, cap)
    ye = _expert_call(xe, wg, wu, wd, cap)
    return _combine_call(starts, npass, x1, aff, lpost, expand, g_final, ye, cap, final)


def _trunk(x, p):
    batch, seq, d = x.shape
    x = x.reshape(batch * seq, d)
    depth = p["norm_mix"].shape[0]
    for i in range(depth):
        j = i // 2
        if i % 2 == 0:
            qkv = _qkv_call(x, p["norm_mix"][i:i + 1], p["a_wqkv"][j])
            o = _na_call(qkv, p["a_bias"][j], batch, seq)
            wo = p["a_wo"][j]
        else:
            qkv = _qkv_call(x, p["norm_mix"][i:i + 1], p["b_wqkv"][j])
            o = _sw_call(qkv, p["b_bias"], p["b_sink"][j], batch, seq)
            wo = p["b_wo"][j]
        x1, h, aff, afft = _post_call(o, x, wo, p["norm_ffn"][i:i + 1], p["wr_hi"][i], p["wr_lo"][i])
        x = _moe(x1, h, aff, afft, p["w_gate"][i], p["w_up"][i], p["w_down"][i],
                 p["expand"], p["norm_final"], i == depth - 1)
    return x.reshape(batch, seq, d)


def _prepare(norm_mix, norm_ffn, norm_final, a_wqkv, a_wo, a_rpb, b_wqkv, b_wo, b_sink,
             w_router, w_gate, w_up, w_down):
    d = D_MODEL
    scale = HEAD_DIM ** -0.5
    a_w = jnp.concatenate([a_wqkv[:, :, :d] * scale, a_wqkv[:, :, d:]], axis=2).astype(BF16)
    order = SW_HEAD_ORDER
    nb = b_wqkv.shape[0]
    bq = (b_wqkv[:, :, :d] * scale).reshape(nb, d, N_HEADS, HEAD_DIM)[:, :, order].reshape(nb, d, d)
    b_w = jnp.concatenate([bq, b_wqkv[:, :, d:]], axis=2).astype(BF16)
    b_wo_p = b_wo.reshape(nb, N_HEADS, HEAD_DIM, d)[:, order].reshape(nb, d, d).astype(BF16)
    sink = jnp.broadcast_to(b_sink.astype(F32)[:, order].reshape(nb, 2, 8, 1), (nb, 2, 8, LANES))
    wr = jnp.pad(w_router.astype(F32), ((0, 0), (0, 0), (0, LANES - N_EXPERTS)))
    wr_hi = wr.astype(BF16)
    wr_lo = (wr - wr_hi.astype(F32)).astype(BF16)
    lane_expert = np.arange(N_EXPERTS * SLAB) // SLAB
    expand = jnp.asarray(np.arange(LANES)[:, None] == lane_expert[None, :], dtype=BF16)
    return {
        "norm_mix": norm_mix.astype(F32), "norm_ffn": norm_ffn.astype(F32),
        "norm_final": norm_final.astype(F32).reshape(1, d),
        "a_wqkv": a_w, "a_wo": a_wo.astype(BF16),
        "a_bias": jnp.stack([_na_bias_table(a_rpb[j]) for j in range(a_rpb.shape[0])]),
        "b_wqkv": b_w, "b_wo": b_wo_p, "b_bias": _sw_bias_table(), "b_sink": sink,
        "wr_hi": wr_hi, "wr_lo": wr_lo,
        "w_gate": w_gate.astype(BF16), "w_up": w_up.astype(BF16), "w_down": w_down.astype(BF16),
        "expand": expand,
    }


def kernel(x_prompt, x_sample, norm_mix, norm_ffn, norm_final, a_wqkv, a_wo, a_rpb, b_wqkv, b_wo, b_sink,
           w_router, w_gate, w_up, w_down):
    p = _prepare(norm_mix, norm_ffn, norm_final, a_wqkv, a_wo, a_rpb, b_wqkv, b_wo, b_sink,
                 w_router, w_gate, w_up, w_down)
    return (_trunk(x_prompt, p), _trunk(x_sample, p))
```

```python
import functools

import numpy as np
import jax
import jax.numpy as jnp
from jax import lax
from jax.experimental import pallas as pl
from jax.experimental.pallas import tpu as pltpu

F32 = jnp.float32
BF16 = jnp.bfloat16
I32 = jnp.int32

D_MODEL = 1024
HEAD_DIM = 64
N_HEADS = 16
LANES = 128
SUBLANES = 8
N_PLANES = D_MODEL // LANES
COL_BLOCKS = D_MODEL // LANES
GRID_W = 64
NA_KH = 8
NA_KW = 16
NA_UNIT = 4 * GRID_W
NA_UNITS_PER_STEP = 4
SW_KV_HEADS = 4
SW_WINDOW = 128
SW_UNIT = 128
SW_UNITS_PER_STEP = 4
N_EXPERTS = 16
EXPERT_FF = 2048
FF_CHUNK = 512
ROUTE_TILE = 256
SLAB = 64
RMS_EPS = 1e-6
NEG = -1e30
ROW_TILE = 512
VMEM_LIMIT = 56 * 1024 * 1024


def _cparams(*sem):
    return pltpu.CompilerParams(dimension_semantics=sem, vmem_limit_bytes=VMEM_LIMIT)


def _rmsnorm(x, g):
    return x * lax.rsqrt(jnp.mean(x * x, axis=-1, keepdims=True) + RMS_EPS) * g


def _qkv_kernel(x_ref, g_ref, w_ref, o_ref):
    y = _rmsnorm(x_ref[...], g_ref[...]).astype(BF16)
    r = jnp.dot(y, w_ref[...], preferred_element_type=F32).astype(BF16)
    for p in range(o_ref.shape[0]):
        o_ref[p] = r[:, p * LANES:(p + 1) * LANES]


def _qkv_call(x, g, w):
    t, d = x.shape
    n = w.shape[1]
    tm = min(ROW_TILE, t)
    return pl.pallas_call(
        _qkv_kernel,
        name="qkv",
        grid=(t // tm,),
        in_specs=[pl.BlockSpec((tm, d), lambda i: (i, 0)),
                  pl.BlockSpec((1, d), lambda i: (0, 0)),
                  pl.BlockSpec((d, n), lambda i: (0, 0))],
        out_specs=pl.BlockSpec((n // LANES, tm, LANES), lambda i: (0, i, 0)),
        out_shape=jax.ShapeDtypeStruct((n // LANES, t, LANES), BF16),
        compiler_params=_cparams("parallel"),
    )(x, g, w)


def _edge_case(u, n_units):
    i = pl.program_id(2)
    last = pl.num_programs(2) - 1
    if n_units == 1:
        return jnp.where(i == 0, 1, jnp.where(i == last, 2, 0))
    if u == 0:
        return jnp.where(i == 0, 1, 0)
    if u == n_units - 1:
        return jnp.where(i == last, 2, 0)
    return 0


def _stacked_attention(qs, kc, vc, bias, sink):
    m_rows = qs[0].shape[0]
    lo_half = lax.broadcasted_iota(I32, qs[0].shape, 1) < HEAD_DIM
    zero = jnp.zeros_like(qs[0])
    lhs = jnp.concatenate(
        [jnp.where(lo_half if hh == 0 else jnp.logical_not(lo_half), q, zero) for q in qs for hh in range(2)],
        axis=0)
    s = lax.dot_general(lhs, kc, (((1,), (1,)), ((), ())), preferred_element_type=F32) + bias
    m = jnp.max(s, axis=1, keepdims=True)
    if sink is not None:
        m = jnp.maximum(m, sink)
    e = jnp.exp(s - m)
    l = jnp.sum(e, axis=1, keepdims=True)
    if sink is not None:
        l = l + jnp.exp(sink - m)
    o = jnp.dot(e.astype(BF16), vc, preferred_element_type=F32) / l
    return [jnp.where(lo_half, o[(2 * i) * m_rows:(2 * i + 1) * m_rows],
                      o[(2 * i + 1) * m_rows:(2 * i + 2) * m_rows]).astype(BF16) for i in range(len(qs))]


def _na_kernel(*refs, n_units):
    q_ref = refs[0]
    k_refs = refs[1:n_units + 3]
    v_refs = refs[n_units + 3:2 * n_units + 5]
    bias_ref = refs[2 * n_units + 5]
    o_ref = refs[2 * n_units + 6]
    for u in range(n_units):
        case = _edge_case(u, n_units)
        kc = jnp.concatenate([k_refs[u + j][0] for j in range(3)], axis=0)
        vc = jnp.concatenate([v_refs[u + j][0] for j in range(3)], axis=0)
        q = q_ref[0, u * NA_UNIT:(u + 1) * NA_UNIT, :]
        o_ref[0, u * NA_UNIT:(u + 1) * NA_UNIT, :] = _stacked_attention(
            [q], kc, vc, bias_ref[0, case], None)[0]


def _na_call(qkv, bias, batch, seq):
    t = batch * seq
    units = seq // NA_UNIT
    n_units = min(NA_UNITS_PER_STEP, units)
    steps = units // n_units
    assert seq % (NA_UNIT * n_units) == 0 and units >= 4
    qb = NA_UNIT * n_units

    def unit_map(plane0, j):
        return lambda hp, b, i: (plane0 + hp, b * units + jnp.clip(n_units * i - 1 + j, 0, units - 1), 0)

    in_specs = [pl.BlockSpec((1, qb, LANES), lambda hp, b, i: (hp, b * steps + i, 0))]
    in_specs += [pl.BlockSpec((1, NA_UNIT, LANES), unit_map(N_PLANES, j)) for j in range(n_units + 2)]
    in_specs += [pl.BlockSpec((1, NA_UNIT, LANES), unit_map(2 * N_PLANES, j)) for j in range(n_units + 2)]
    in_specs += [pl.BlockSpec((1, 3, 2 * NA_UNIT, 3 * NA_UNIT), lambda hp, b, i: (hp, 0, 0, 0))]
    return pl.pallas_call(
        functools.partial(_na_kernel, n_units=n_units),
        name="na_attn",
        grid=(N_PLANES, batch, steps),
        in_specs=in_specs,
        out_specs=pl.BlockSpec((1, qb, LANES), lambda hp, b, i: (hp, b * steps + i, 0)),
        out_shape=jax.ShapeDtypeStruct((N_PLANES, t, LANES), BF16),
        compiler_params=_cparams("arbitrary", "arbitrary", "arbitrary"),
    )(*([qkv] * (2 * n_units + 5)), bias)


def _na_bias_table(rpb):
    a = np.arange(NA_UNIT) // GRID_W
    j = np.arange(3 * NA_UNIT) // GRID_W
    row_ok = np.stack([
        (j[None, :] >= a[:, None]) & (j[None, :] < a[:, None] + NA_KH),
        np.broadcast_to((j[None, :] >= 4) & (j[None, :] < 4 + NA_KH), (NA_UNIT, 3 * NA_UNIT)),
        np.broadcast_to(j[None, :] < NA_KH, (NA_UNIT, 3 * NA_UNIT)),
    ])
    w = np.arange(GRID_W)
    dc = np.clip(w[None, :] - w[:, None], -(NA_KW - 1), NA_KW - 1) + NA_KW - 1
    c0 = np.clip(w - NA_KW // 2, 0, GRID_W - NA_KW)
    col_ok = (w[None, :] >= c0[:, None]) & (w[None, :] < c0[:, None] + NA_KW)
    cols = jnp.take(rpb.astype(F32), jnp.asarray(dc.reshape(-1)), axis=2)
    cols = cols.reshape(N_HEADS, 2 * NA_KH - 1, GRID_W, GRID_W)
    cols = jnp.where(jnp.asarray(col_ok), cols, NEG)
    rows = jnp.stack([cols[:, 3 - ai:15 - ai] for ai in range(4)], axis=1)
    tbl = rows.transpose(0, 1, 3, 2, 4).reshape(N_HEADS, NA_UNIT, 3 * NA_UNIT)
    full = jnp.where(jnp.asarray(row_ok)[:, None], tbl[None], NEG)
    full = full.reshape(3, N_PLANES, 2 * NA_UNIT, 3 * NA_UNIT)
    return full.transpose(1, 0, 2, 3)


SW_HEAD_ORDER = np.array([8 * m + 4 * hh + i for m in range(2) for i in range(4) for hh in range(2)])


def _sw_kernel(*refs, n_units):
    q_ref = refs[0]
    k_refs = refs[1:n_units + 3]
    v_refs = refs[n_units + 3:2 * n_units + 5]
    bias_ref = refs[2 * n_units + 5]
    sink_ref = refs[2 * n_units + 6]
    o_ref = refs[2 * n_units + 7]
    for u in range(n_units):
        case = _edge_case(u, n_units)
        kc = jnp.concatenate([k_refs[u + j][0] for j in range(3)], axis=0)
        vc = jnp.concatenate([v_refs[u + j][0] for j in range(3)], axis=0)
        qs = [q_ref[pi, u * SW_UNIT:(u + 1) * SW_UNIT, :] for pi in range(4)]
        outs = _stacked_attention(qs, kc, vc, bias_ref[0, case], sink_ref[0, :, 0:1])
        for pi in range(4):
            o_ref[pi, u * SW_UNIT:(u + 1) * SW_UNIT, :] = outs[pi]


def _sw_call(qkv, bias, sink, batch, seq):
    t = batch * seq
    units = seq // SW_UNIT
    n_units = min(SW_UNITS_PER_STEP, units)
    steps = units // n_units
    assert seq % (SW_UNIT * n_units) == 0 and units >= 2
    qb = SW_UNIT * n_units

    def unit_map(plane0):
        def make(j):
            return lambda m, b, i: (plane0 + m, b * units + jnp.clip(n_units * i - 1 + j, 0, units - 1), 0)
        return make

    in_specs = [pl.BlockSpec((4, qb, LANES), lambda m, b, i: (m, b * steps + i, 0))]
    in_specs += [pl.BlockSpec((1, SW_UNIT, LANES), unit_map(N_PLANES)(j)) for j in range(n_units + 2)]
    in_specs += [pl.BlockSpec((1, SW_UNIT, LANES), unit_map(N_PLANES + 2)(j)) for j in range(n_units + 2)]
    in_specs += [pl.BlockSpec((1, 3, 8 * SW_UNIT, 3 * SW_UNIT), lambda m, b, i: (m, 0, 0, 0)),
                 pl.BlockSpec((1, 8 * SW_UNIT, LANES), lambda m, b, i: (m, 0, 0))]
    return pl.pallas_call(
        functools.partial(_sw_kernel, n_units=n_units),
        name="sw_attn",
        grid=(2, batch, steps),
        in_specs=in_specs,
        out_specs=pl.BlockSpec((4, qb, LANES), lambda m, b, i: (m, b * steps + i, 0)),
        out_shape=jax.ShapeDtypeStruct((N_PLANES, t, LANES), BF16),
        compiler_params=_cparams("arbitrary", "arbitrary", "arbitrary"),
    )(*([qkv] * (2 * n_units + 5)), bias, sink)


def _sw_bias_table():
    slopes = np.asarray(2.0 ** (-8.0 * np.arange(1, N_HEADS + 1) / N_HEADS), dtype=np.float32)
    ks = np.arange(3 * SW_UNIT) - SW_WINDOW
    dist = ks[None, :] - np.arange(SW_UNIT)[:, None]
    win_ok = np.abs(dist) <= SW_WINDOW
    pos_ok = np.stack([np.ones(3 * SW_UNIT, bool), ks >= 0, ks < SW_UNIT])
    ok = win_ok[None] & pos_ok[:, None, :]
    alibi = -slopes[SW_HEAD_ORDER][:, None, None] * np.abs(dist).astype(np.float32)[None]
    full = np.where(ok[:, None], alibi[None], np.float32(NEG)).astype(np.float32)
    return jnp.asarray(full.reshape(3, 2, 8 * SW_UNIT, 3 * SW_UNIT).transpose(1, 0, 2, 3))


def _post_kernel(o_ref, x_ref, wo_ref, g_ref, wrh_ref, wrl_ref, x1_ref, h_ref, aff_ref, afft_ref):
    oc = jnp.concatenate([o_ref[p] for p in range(N_PLANES)], axis=1)
    x1 = x_ref[...] + jnp.dot(oc, wo_ref[...], preferred_element_type=F32)
    x1_ref[...] = x1
    h = _rmsnorm(x1, g_ref[...])
    hb = h.astype(BF16)
    h_ref[...] = hb
    hl = (h - hb.astype(F32)).astype(BF16)
    logits = jnp.dot(hb, wrh_ref[...], preferred_element_type=F32) + (
        jnp.dot(hb, wrl_ref[...], preferred_element_type=F32)
        + jnp.dot(hl, wrh_ref[...], preferred_element_type=F32))
    lane = lax.broadcasted_iota(I32, logits.shape, 1)
    logits = jnp.where(lane < N_EXPERTS, logits, NEG)
    ex = jnp.exp(logits - jnp.max(logits, axis=1, keepdims=True))
    aff = ex / jnp.sum(ex, axis=1, keepdims=True)
    aff_ref[...] = aff
    afft_ref[...] = aff.T[:N_EXPERTS, :]


def _post_call(o, x, wo, g, wrh, wrl):
    t, d = x.shape
    tm = min(ROW_TILE, t)
    return pl.pallas_call(
        _post_kernel,
        name="post",
        grid=(t // tm,),
        in_specs=[pl.BlockSpec((N_PLANES, tm, LANES), lambda i: (0, i, 0)),
                  pl.BlockSpec((tm, d), lambda i: (i, 0)),
                  pl.BlockSpec((d, d), lambda i: (0, 0)),
                  pl.BlockSpec((1, d), lambda i: (0, 0)),
                  pl.BlockSpec((d, LANES), lambda i: (0, 0)),
                  pl.BlockSpec((d, LANES), lambda i: (0, 0))],
        out_specs=[pl.BlockSpec((tm, d), lambda i: (i, 0)),
                   pl.BlockSpec((tm, d), lambda i: (i, 0)),
                   pl.BlockSpec((tm, LANES), lambda i: (i, 0)),
                   pl.BlockSpec((N_EXPERTS, tm), lambda i: (0, i))],
        out_shape=[jax.ShapeDtypeStruct((t, d), F32),
                   jax.ShapeDtypeStruct((t, d), BF16),
                   jax.ShapeDtypeStruct((t, LANES), F32),
                   jax.ShapeDtypeStruct((N_EXPERTS, t), F32)],
        compiler_params=_cparams("parallel"),
    )(o, x, wo, g, wrh, wrl)


def _threshold_kernel(aff_ref, thr_ref, need_ref, *, cap):
    bits = lax.bitcast_convert_type(aff_ref[...], I32)

    def body(k, cur):
        cand = cur | jnp.left_shift(jnp.int32(1), 30 - k)
        cnt = jnp.sum(jnp.where(bits >= cand, 1, 0), axis=1, keepdims=True)
        return jnp.where(cnt >= cap, cand, cur)

    cur = lax.fori_loop(0, 31, body, jnp.zeros((N_EXPERTS, 1), I32))
    above = jnp.sum(jnp.where(bits > cur, 1, 0), axis=1, keepdims=True)
    thr_ref[...] = jnp.broadcast_to(cur, thr_ref.shape)
    need_ref[...] = jnp.broadcast_to(cap - above, need_ref.shape)


def _threshold_call(afft, cap):
    e, t = afft.shape
    return pl.pallas_call(
        functools.partial(_threshold_kernel, cap=cap),
        name="threshold",
        grid=(1,),
        in_specs=[pl.BlockSpec((e, t), lambda i: (0, 0))],
        out_specs=[pl.BlockSpec((e, LANES), lambda i: (0, 0)),
                   pl.BlockSpec((e, LANES), lambda i: (0, 0))],
        out_shape=[jax.ShapeDtypeStruct((e, LANES), I32),
                   jax.ShapeDtypeStruct((e, LANES), I32)],
        compiler_params=_cparams("arbitrary"),
    )(afft)


def _positions_kernel(aff_ref, thr_ref, need_ref, lpos_ref, lpost_ref, starts_ref, npass_ref, base_sc, eqb_sc):
    @pl.when(pl.program_id(0) == 0)
    def _():
        base_sc[...] = jnp.zeros_like(base_sc)
        eqb_sc[...] = jnp.zeros_like(eqb_sc)

    bits = lax.bitcast_convert_type(aff_ref[...], I32)
    thr = thr_ref[:, 0:1]
    need = need_ref[:, 0:1].astype(F32)
    r = lax.broadcasted_iota(I32, (ROUTE_TILE, ROUTE_TILE), 0)
    c = lax.broadcasted_iota(I32, (ROUTE_TILE, ROUTE_TILE), 1)
    before = jnp.where(r < c, 1.0, 0.0).astype(BF16)
    eq = bits == thr
    eqf = jnp.where(eq, 1.0, 0.0)
    eq_rank = eqb_sc[:, 0:1] + jnp.dot(eqf.astype(BF16), before, preferred_element_type=F32)
    sel = (bits > thr) | (eq & (eq_rank < need))
    self_ = jnp.where(sel, 1.0, 0.0)
    rank = jnp.dot(self_.astype(BF16), before, preferred_element_type=F32)
    cnt = jnp.sum(self_, axis=1, keepdims=True)
    lpos = jnp.where(sel, rank, -1.0)
    lpos_ref[...] = lpos.astype(I32)
    pad = jnp.full((LANES - N_EXPERTS, ROUTE_TILE), -1.0, F32)
    lpost_ref[...] = jnp.concatenate([lpos, pad], axis=0).T
    starts_ref[0] = base_sc[...].astype(I32)
    most = jnp.max(cnt, axis=0, keepdims=True)
    npass_ref[0] = jnp.broadcast_to(jnp.floor((most + (SLAB - 1)) * (1.0 / SLAB)), npass_ref.shape[1:]).astype(I32)
    base_sc[...] = base_sc[...] + cnt
    eqb_sc[...] = eqb_sc[...] + jnp.sum(eqf, axis=1, keepdims=True)


def _positions_call(afft, thr, need):
    e, t = afft.shape
    n_tiles = t // ROUTE_TILE
    return pl.pallas_call(
        _positions_kernel,
        name="positions",
        grid=(n_tiles,),
        in_specs=[pl.BlockSpec((e, ROUTE_TILE), lambda i: (0, i)),
                  pl.BlockSpec((e, LANES), lambda i: (0, 0)),
                  pl.BlockSpec((e, LANES), lambda i: (0, 0))],
        out_specs=[pl.BlockSpec((e, ROUTE_TILE), lambda i: (0, i)),
                   pl.BlockSpec((ROUTE_TILE, LANES), lambda i: (i, 0)),
                   pl.BlockSpec((1, e, LANES), lambda i: (i, 0, 0)),
                   pl.BlockSpec((1, SUBLANES, LANES), lambda i: (i, 0, 0))],
        out_shape=[jax.ShapeDtypeStruct((e, t), I32),
                   jax.ShapeDtypeStruct((t, LANES), F32),
                   jax.ShapeDtypeStruct((n_tiles, e, LANES), I32),
                   jax.ShapeDtypeStruct((n_tiles, SUBLANES, LANES), I32)],
        scratch_shapes=[pltpu.VMEM((e, LANES), F32), pltpu.VMEM((e, LANES), F32)],
        compiler_params=_cparams("arbitrary"),
    )(afft, thr, need)


SLAB_ROWS = SLAB * COL_BLOCKS


def _dispatch_kernel(starts_sm, npass_sm, h_ref, lpos_ref, xe_ref, slab_sc, sem, *, cap):
    i = pl.program_id(0)
    sub = lax.broadcasted_iota(I32, (SLAB, ROUTE_TILE), 0)

    @pl.when(i == 0)
    def _():
        slab_sc[pl.ds(0, SLAB_ROWS), :] = jnp.zeros((SLAB_ROWS, LANES), F32)
        pads = [pltpu.make_async_copy(slab_sc.at[pl.ds(0, SLAB_ROWS)],
                                      xe_ref.at[e, pl.ds(cap * COL_BLOCKS, SLAB_ROWS)], sem.at[e])
                for e in range(N_EXPERTS)]
        for cp in pads:
            cp.start()
        for cp in pads:
            cp.wait()

    def one_pass(s, carry):
        rows = []
        for e in range(N_EXPERTS):
            lp = lpos_ref[e:e + 1, :] - SLAB * s
            rows.append(jnp.where(lp == sub, 1.0, 0.0).astype(BF16))
        onehot = jnp.concatenate(rows, axis=0)
        res = jnp.dot(onehot, h_ref[...], preferred_element_type=F32)
        for cb in range(COL_BLOCKS):
            slab_sc[pl.ds(cb, N_EXPERTS * SLAB, stride=COL_BLOCKS), :] = res[:, cb * LANES:(cb + 1) * LANES]
        copies = []
        for e in range(N_EXPERTS):
            dst = jnp.minimum(starts_sm[i * N_EXPERTS + e] + SLAB * s, cap)
            cp = pltpu.make_async_copy(
                slab_sc.at[pl.ds(e * SLAB_ROWS, SLAB_ROWS)],
                xe_ref.at[e, pl.ds(pl.multiple_of(dst * COL_BLOCKS, COL_BLOCKS), SLAB_ROWS)],
                sem.at[e])
            cp.start()
            copies.append(cp)
        for cp in copies:
            cp.wait()
        return carry

    lax.fori_loop(0, npass_sm[i], one_pass, 0)


def _dispatch_call(starts, npass, h, lpos, cap):
    t, d = h.shape
    n_tiles = t // ROUTE_TILE
    grid_spec = pltpu.PrefetchScalarGridSpec(
        num_scalar_prefetch=2,
        grid=(n_tiles,),
        in_specs=[pl.BlockSpec((ROUTE_TILE, d), lambda i, *_: (i, 0)),
                  pl.BlockSpec((N_EXPERTS, ROUTE_TILE), lambda i, *_: (0, i))],
        out_specs=pl.BlockSpec(memory_space=pl.ANY),
        scratch_shapes=[pltpu.VMEM((N_EXPERTS * SLAB_ROWS, LANES), F32),
                        pltpu.SemaphoreType.DMA((N_EXPERTS,))],
    )
    return pl.pallas_call(
        functools.partial(_dispatch_kernel, cap=cap),
        name="dispatch",
        grid_spec=grid_spec,
        out_shape=jax.ShapeDtypeStruct((N_EXPERTS, (cap + SLAB) * COL_BLOCKS, LANES), F32),
        compiler_params=_cparams("arbitrary"),
    )(starts, npass, h, lpos)


def _expert_kernel(x_ref, wg_ref, wu_ref, wd_ref, o_ref):
    tm = x_ref.shape[1] // COL_BLOCKS
    x = jnp.concatenate([x_ref[0, pl.ds(cb, tm, stride=COL_BLOCKS), :] for cb in range(COL_BLOCKS)],
                        axis=1).astype(BF16)
    acc = jnp.zeros((tm, D_MODEL), F32)
    for fc in range(EXPERT_FF // FF_CHUNK):
        fs = slice(fc * FF_CHUNK, (fc + 1) * FF_CHUNK)
        g = jnp.dot(x, wg_ref[0, 0, :, fs], preferred_element_type=F32)
        u = jnp.dot(x, wu_ref[0, 0, :, fs], preferred_element_type=F32)
        hmid = (g * (1.0 / (1.0 + jnp.exp(-g))) * u).astype(BF16)
        acc = acc + jnp.dot(hmid, wd_ref[0, 0, fs, :], preferred_element_type=F32)
    for cb in range(COL_BLOCKS):
        o_ref[0, pl.ds(cb, tm, stride=COL_BLOCKS), :] = acc[:, cb * LANES:(cb + 1) * LANES]


def _expert_call(xe, wg, wu, wd, layer, cap):
    tm = min(ROW_TILE, cap)
    d, f = wg.shape[2], wg.shape[3]
    return pl.pallas_call(
        _expert_kernel,
        name="expert",
        grid=(N_EXPERTS, cap // tm),
        in_specs=[pl.BlockSpec((1, tm * COL_BLOCKS, LANES), lambda e, j: (e, j, 0)),
                  pl.BlockSpec((1, 1, d, f), lambda e, j: (layer, e, 0, 0)),
                  pl.BlockSpec((1, 1, d, f), lambda e, j: (layer, e, 0, 0)),
                  pl.BlockSpec((1, 1, f, d), lambda e, j: (layer, e, 0, 0))],
        out_specs=pl.BlockSpec((1, tm * COL_BLOCKS, LANES), lambda e, j: (e, j, 0)),
        out_shape=jax.ShapeDtypeStruct((N_EXPERTS, cap * COL_BLOCKS, LANES), F32),
        compiler_params=_cparams("parallel", "parallel"),
    )(xe, wg, wu, wd)


def _combine_kernel(starts_sm, npass_sm, x_ref, aff_ref, lpost_ref, expand_ref, g_ref, ye_ref, o_ref,
                    slab_sc, sem, *, cap, final):
    i = pl.program_id(0)
    lane = lax.broadcasted_iota(I32, (1, LANES), 1)
    slot_in_slab = (lax.broadcasted_iota(I32, (1, N_EXPERTS * SLAB), 1) % SLAB).astype(F32)
    aff = aff_ref[...]
    ghi = aff.astype(BF16)
    glo = (aff - ghi.astype(F32)).astype(BF16)
    expand = expand_ref[...]
    ghi_x = jnp.dot(ghi, expand, preferred_element_type=F32)
    glo_x = jnp.dot(glo, expand, preferred_element_type=F32)
    lpost = lpost_ref[...]

    def one_pass(s, acc):
        copies = []
        shift = jnp.zeros((1, LANES), F32)
        for e in range(N_EXPERTS):
            want = starts_sm[i * N_EXPERTS + e] + SLAB * s
            src = jnp.minimum(want, cap - SLAB)
            shift = jnp.where(lane == e, (want - src).astype(F32), shift)
            cp = pltpu.make_async_copy(
                ye_ref.at[e, pl.ds(pl.multiple_of(src * COL_BLOCKS, COL_BLOCKS), SLAB_ROWS)],
                slab_sc.at[pl.ds(e * SLAB_ROWS, SLAB_ROWS)],
                sem.at[e])
            cp.start()
            copies.append(cp)
        lo = (SLAB * s).astype(F32)
        owned = (lpost >= lo) & (lpost < lo + SLAB)
        row = jnp.where(owned, lpost - lo + shift, -1.0).astype(BF16)
        row_x = jnp.dot(row, expand, preferred_element_type=F32)
        match = row_x == slot_in_slab
        whi = jnp.where(match, ghi_x, 0.0).astype(BF16)
        wlo = jnp.where(match, glo_x, 0.0).astype(BF16)
        for cp in copies:
            cp.wait()
        rows = jnp.concatenate(
            [slab_sc[pl.ds(cb, N_EXPERTS * SLAB, stride=COL_BLOCKS), :] for cb in range(COL_BLOCKS)],
            axis=1).astype(BF16)
        return acc + (jnp.dot(whi, rows, preferred_element_type=F32)
                      + jnp.dot(wlo, rows, preferred_element_type=F32))

    y = lax.fori_loop(0, npass_sm[i], one_pass, jnp.zeros(x_ref.shape, F32))
    out = x_ref[...] + y
    if final:
        out = _rmsnorm(out, g_ref[...])
    o_ref[...] = out


def _combine_call(starts, npass, x1, aff, lpost, expand, g, ye, cap, final):
    t, d = x1.shape
    n_tiles = t // ROUTE_TILE
    grid_spec = pltpu.PrefetchScalarGridSpec(
        num_scalar_prefetch=2,
        grid=(n_tiles,),
        in_specs=[pl.BlockSpec((ROUTE_TILE, d), lambda i, *_: (i, 0)),
                  pl.BlockSpec((ROUTE_TILE, LANES), lambda i, *_: (i, 0)),
                  pl.BlockSpec((ROUTE_TILE, LANES), lambda i, *_: (i, 0)),
                  pl.BlockSpec((LANES, N_EXPERTS * SLAB), lambda i, *_: (0, 0)),
                  pl.BlockSpec((1, d), lambda i, *_: (0, 0)),
                  pl.BlockSpec(memory_space=pl.ANY)],
        out_specs=pl.BlockSpec((ROUTE_TILE, d), lambda i, *_: (i, 0)),
        scratch_shapes=[pltpu.VMEM((N_EXPERTS * SLAB_ROWS, LANES), F32),
                        pltpu.SemaphoreType.DMA((N_EXPERTS,))],
    )
    return pl.pallas_call(
        functools.partial(_combine_kernel, cap=cap, final=final),
        name="combine",
        grid_spec=grid_spec,
        out_shape=jax.ShapeDtypeStruct((t, d), F32),
        compiler_params=_cparams("arbitrary"),
    )(starts, npass, x1, aff, lpost, expand, g, ye)


def _moe(x1, h, aff, afft, wg, wu, wd, layer, expand, g_final, final):
    t = x1.shape[0]
    cap = 2 * t // N_EXPERTS
    assert t % ROUTE_TILE == 0 and cap >= SLAB
    thr, need = _threshold_call(afft, cap)
    lpos, lpost, starts3, npass3 = _positions_call(afft, thr, need)
    starts = starts3[:, :, 0].reshape(-1)
    npass = npass3[:, 0, 0]
    xe = _dispatch_call(starts, npass, h, lpos, cap)
    ye = _expert_call(xe, wg, wu, wd, layer, cap)
    return _combine_call(starts, npass, x1, aff, lpost, expand, g_final, ye, cap, final)


def _trunk(x, p):
    batch, seq, d = x.shape
    x = x.reshape(batch * seq, d)
    depth = p["norm_mix"].shape[0]
    for i in range(depth):
        j = i // 2
        if i % 2 == 0:
            qkv = _qkv_call(x, p["norm_mix"][i:i + 1], p["a_wqkv"][j])
            o = _na_call(qkv, p["a_bias"][j], batch, seq)
            wo = p["a_wo"][j]
        else:
            qkv = _qkv_call(x, p["norm_mix"][i:i + 1], p["b_wqkv"][j])
            o = _sw_call(qkv, p["b_bias"], p["b_sink"][j], batch, seq)
            wo = p["b_wo"][j]
        x1, h, aff, afft = _post_call(o, x, wo, p["norm_ffn"][i:i + 1], p["wr_hi"][i], p["wr_lo"][i])
        x = _moe(x1, h, aff, afft, p["w_gate"], p["w_up"], p["w_down"], i,
                 p["expand"], p["norm_final"], i == depth - 1)
    return x.reshape(batch, seq, d)


def _prepare(norm_mix, norm_ffn, norm_final, a_wqkv, a_wo, a_rpb, b_wqkv, b_wo, b_sink,
             w_router, w_gate, w_up, w_down):
    d = D_MODEL
    scale = HEAD_DIM ** -0.5
    a_w = jnp.concatenate([a_wqkv[:, :, :d] * scale, a_wqkv[:, :, d:]], axis=2).astype(BF16)
    order = SW_HEAD_ORDER
    nb = b_wqkv.shape[0]
    bq = (b_wqkv[:, :, :d] * scale).reshape(nb, d, N_HEADS, HEAD_DIM)[:, :, order].reshape(nb, d, d)
    b_w = jnp.concatenate([bq, b_wqkv[:, :, d:]], axis=2).astype(BF16)
    b_wo_p = b_wo.reshape(nb, N_HEADS, HEAD_DIM, d)[:, order].reshape(nb, d, d).astype(BF16)
    sink = jnp.broadcast_to(b_sink.astype(F32)[:, order].reshape(nb, 2, 8, 1, 1), (nb, 2, 8, SW_UNIT, LANES))
    sink = sink.reshape(nb, 2, 8 * SW_UNIT, LANES)
    wr = jnp.pad(w_router.astype(F32), ((0, 0), (0, 0), (0, LANES - N_EXPERTS)))
    wr_hi = wr.astype(BF16)
    wr_lo = (wr - wr_hi.astype(F32)).astype(BF16)
    lane_expert = np.arange(N_EXPERTS * SLAB) // SLAB
    expand = jnp.asarray(np.arange(LANES)[:, None] == lane_expert[None, :], dtype=BF16)
    return {
        "norm_mix": norm_mix.astype(F32), "norm_ffn": norm_ffn.astype(F32),
        "norm_final": norm_final.astype(F32).reshape(1, d),
        "a_wqkv": a_w, "a_wo": a_wo.astype(BF16),
        "a_bias": jnp.stack([_na_bias_table(a_rpb[j]) for j in range(a_rpb.shape[0])]),
        "b_wqkv": b_w, "b_wo": b_wo_p, "b_bias": _sw_bias_table(), "b_sink": sink,
        "wr_hi": wr_hi, "wr_lo": wr_lo,
        "w_gate": w_gate.astype(BF16), "w_up": w_up.astype(BF16), "w_down": w_down.astype(BF16),
        "expand": expand,
    }


def kernel(x_prompt, x_sample, norm_mix, norm_ffn, norm_final, a_wqkv, a_wo, a_rpb, b_wqkv, b_wo, b_sink,
           w_router, w_gate, w_up, w_down):
    p = _prepare(norm_mix, norm_ffn, norm_final, a_wqkv, a_wo, a_rpb, b_wqkv, b_wo, b_sink,
                 w_router, w_gate, w_up, w_down)
    return (_trunk(x_prompt, p), _trunk(x_sample, p))
```

```python
import functools

import numpy as np
import jax
import jax.numpy as jnp
from jax import lax
from jax.experimental import pallas as pl
from jax.experimental.pallas import tpu as pltpu

F32 = jnp.float32
BF16 = jnp.bfloat16
I32 = jnp.int32

D_MODEL = 1024
HEAD_DIM = 64
N_HEADS = 16
LANES = 128
SUBLANES = 8
N_PLANES = D_MODEL // LANES
COL_BLOCKS = D_MODEL // LANES
GRID_W = 64
NA_KH = 8
NA_KW = 16
NA_UNIT = 4 * GRID_W
NA_UNITS_PER_STEP = 4
SW_KV_HEADS = 4
SW_WINDOW = 128
SW_UNIT = 128
SW_UNITS_PER_STEP = 4
N_EXPERTS = 16
EXPERT_FF = 2048
FF_CHUNK = 512
ROUTE_TILE = 256
SLAB = 64
RMS_EPS = 1e-6
NEG = -1e30
ROW_TILE = 512
VMEM_LIMIT = 56 * 1024 * 1024


def _cparams(*sem):
    return pltpu.CompilerParams(dimension_semantics=sem, vmem_limit_bytes=VMEM_LIMIT)


def _rmsnorm(x, g):
    return x * lax.rsqrt(jnp.mean(x * x, axis=-1, keepdims=True) + RMS_EPS) * g


def _qkv_kernel(x_ref, g_ref, w_ref, qk_ref, vt_ref):
    y = _rmsnorm(x_ref[...], g_ref[...]).astype(BF16)
    r = jnp.dot(y, w_ref[...], preferred_element_type=F32)
    n_qk = qk_ref.shape[0]
    for p in range(n_qk):
        qk_ref[p] = r[:, p * LANES:(p + 1) * LANES].astype(BF16)
    for p in range(vt_ref.shape[0]):
        vt_ref[p] = r[:, (n_qk + p) * LANES:(n_qk + p + 1) * LANES].T.astype(BF16)


def _qkv_call(x, g, w, n_v_planes):
    t, d = x.shape
    n_qk = w.shape[1] // LANES - n_v_planes
    tm = min(ROW_TILE, t)
    return pl.pallas_call(
        _qkv_kernel,
        name="qkv",
        grid=(t // tm,),
        in_specs=[pl.BlockSpec((tm, d), lambda i: (i, 0)),
                  pl.BlockSpec((1, d), lambda i: (0, 0)),
                  pl.BlockSpec((d, w.shape[1]), lambda i: (0, 0))],
        out_specs=[pl.BlockSpec((n_qk, tm, LANES), lambda i: (0, i, 0)),
                   pl.BlockSpec((n_v_planes, LANES, tm), lambda i: (0, 0, i))],
        out_shape=[jax.ShapeDtypeStruct((n_qk, t, LANES), BF16),
                   jax.ShapeDtypeStruct((n_v_planes, LANES, t), BF16)],
        compiler_params=_cparams("parallel"),
    )(x, g, w)


def _edge_case(u, n_units):
    i = pl.program_id(2)
    last = pl.num_programs(2) - 1
    if n_units == 1:
        return jnp.where(i == 0, 1, jnp.where(i == last, 2, 0))
    if u == 0:
        return jnp.where(i == 0, 1, 0)
    if u == n_units - 1:
        return jnp.where(i == last, 2, 0)
    return 0


def _attn_kernel(*refs, unit, n_units, has_sink):
    q_ref, kp_ref, kc_ref, kn_ref, vp_ref, vc_ref, vn_ref, bias_ref = refs[:8]
    sink_ref = refs[8] if has_sink else None
    o_ref = refs[-1]
    n_planes = q_ref.shape[0]
    kcat = jnp.concatenate([kp_ref[0], kc_ref[0], kn_ref[0]], axis=0)
    vcat = jnp.concatenate([vp_ref[0], vc_ref[0], vn_ref[0]], axis=1)
    lo_half = lax.broadcasted_iota(I32, (unit, LANES), 1) < HEAD_DIM
    zero = jnp.zeros((unit, LANES), BF16)
    for u in range(n_units):
        case = _edge_case(u, n_units)
        kw = kcat[u * unit:(u + 3) * unit]
        vw = vcat[:, u * unit:(u + 3) * unit]
        qs = [q_ref[pi, u * unit:(u + 1) * unit, :] for pi in range(n_planes)]
        q_stack = jnp.concatenate(
            [jnp.where(lo_half if hh == 0 else jnp.logical_not(lo_half), q, zero) for q in qs for hh in range(2)],
            axis=0)
        st = lax.dot_general(kw, q_stack, (((1,), (1,)), ((), ())), preferred_element_type=F32)
        st = st + bias_ref[0, case]
        m = jnp.max(st, axis=0, keepdims=True)
        if has_sink:
            sink = sink_ref[0, 0:1, :]
            m = jnp.maximum(m, sink)
        e = jnp.exp(st - m)
        l = jnp.sum(e, axis=0, keepdims=True)
        if has_sink:
            l = l + jnp.exp(sink - m)
        ot = jnp.dot(vw, e.astype(BF16), preferred_element_type=F32) / l
        for pi in range(n_planes):
            both = jnp.concatenate(
                [ot[0:HEAD_DIM, (2 * pi) * unit:(2 * pi + 1) * unit],
                 ot[HEAD_DIM:LANES, (2 * pi + 1) * unit:(2 * pi + 2) * unit]], axis=0)
            o_ref[pi, u * unit:(u + 1) * unit, :] = both.T.astype(BF16)


def _attn_call(name, qk, vt, bias, sink, batch, seq, unit, n_units, n_groups):
    t = batch * seq
    units = seq // unit
    n_units = min(n_units, units)
    steps = units // n_units
    assert seq % (unit * n_units) == 0 and units >= 4
    ppg = N_PLANES // n_groups
    qb = unit * n_units
    width = 2 * ppg * unit

    def cur(g, b, i):
        return b * steps + i

    def prev(g, b, i):
        return b * units + jnp.maximum(n_units * i - 1, 0)

    def nxt(g, b, i):
        return b * units + jnp.minimum(n_units * i + n_units, units - 1)

    in_specs = [pl.BlockSpec((ppg, qb, LANES), lambda g, b, i: (g, cur(g, b, i), 0)),
                pl.BlockSpec((1, unit, LANES), lambda g, b, i: (N_PLANES + g, prev(g, b, i), 0)),
                pl.BlockSpec((1, qb, LANES), lambda g, b, i: (N_PLANES + g, cur(g, b, i), 0)),
                pl.BlockSpec((1, unit, LANES), lambda g, b, i: (N_PLANES + g, nxt(g, b, i), 0)),
                pl.BlockSpec((1, LANES, unit), lambda g, b, i: (g, 0, prev(g, b, i))),
                pl.BlockSpec((1, LANES, qb), lambda g, b, i: (g, 0, cur(g, b, i))),
                pl.BlockSpec((1, LANES, unit), lambda g, b, i: (g, 0, nxt(g, b, i))),
                pl.BlockSpec((1, 3, 3 * unit, width), lambda g, b, i: (g, 0, 0, 0))]
    args = [qk, qk, qk, qk, vt, vt, vt, bias]
    if sink is not None:
        in_specs.append(pl.BlockSpec((1, SUBLANES, width), lambda g, b, i: (g, 0, 0)))
        args.append(sink)
    return pl.pallas_call(
        functools.partial(_attn_kernel, unit=unit, n_units=n_units, has_sink=sink is not None),
        name=name,
        grid=(n_groups, batch, steps),
        in_specs=in_specs,
        out_specs=pl.BlockSpec((ppg, qb, LANES), lambda g, b, i: (g, cur(g, b, i), 0)),
        out_shape=jax.ShapeDtypeStruct((N_PLANES, t, LANES), BF16),
        compiler_params=_cparams("arbitrary", "arbitrary", "arbitrary"),
    )(*args)


def _na_bias_table(rpb):
    a = np.arange(NA_UNIT) // GRID_W
    j = np.arange(3 * NA_UNIT) // GRID_W
    row_ok = np.stack([
        (j[None, :] >= a[:, None]) & (j[None, :] < a[:, None] + NA_KH),
        np.broadcast_to((j[None, :] >= 4) & (j[None, :] < 4 + NA_KH), (NA_UNIT, 3 * NA_UNIT)),
        np.broadcast_to(j[None, :] < NA_KH, (NA_UNIT, 3 * NA_UNIT)),
    ])
    w = np.arange(GRID_W)
    dc = np.clip(w[None, :] - w[:, None], -(NA_KW - 1), NA_KW - 1) + NA_KW - 1
    c0 = np.clip(w - NA_KW // 2, 0, GRID_W - NA_KW)
    col_ok = (w[None, :] >= c0[:, None]) & (w[None, :] < c0[:, None] + NA_KW)
    cols = jnp.take(rpb.astype(F32), jnp.asarray(dc.reshape(-1)), axis=2)
    cols = cols.reshape(N_HEADS, 2 * NA_KH - 1, GRID_W, GRID_W)
    cols = jnp.where(jnp.asarray(col_ok), cols, NEG)
    rows = jnp.stack([cols[:, 3 - ai:15 - ai] for ai in range(4)], axis=1)
    tbl = rows.transpose(0, 2, 4, 1, 3).reshape(N_HEADS, 3 * NA_UNIT, NA_UNIT)
    ok = jnp.asarray(row_ok.transpose(0, 2, 1))
    full = jnp.where(ok[:, None], tbl[None], NEG)
    full = full.reshape(3, N_PLANES, 2, 3 * NA_UNIT, NA_UNIT).transpose(1, 0, 3, 2, 4)
    return full.reshape(N_PLANES, 3, 3 * NA_UNIT, 2 * NA_UNIT)


SW_HEAD_ORDER = np.array([8 * m + 4 * hh + i for m in range(2) for i in range(4) for hh in range(2)])


def _sw_bias_table():
    slopes = np.asarray(2.0 ** (-8.0 * np.arange(1, N_HEADS + 1) / N_HEADS), dtype=np.float32)
    ks = np.arange(3 * SW_UNIT) - SW_WINDOW
    dist = ks[None, :] - np.arange(SW_UNIT)[:, None]
    win_ok = np.abs(dist) <= SW_WINDOW
    pos_ok = np.stack([np.ones(3 * SW_UNIT, bool), ks >= 0, ks < SW_UNIT])
    ok = win_ok[None] & pos_ok[:, None, :]
    alibi = -slopes[SW_HEAD_ORDER][:, None, None] * np.abs(dist).astype(np.float32)[None]
    full = np.where(ok[:, None], alibi[None], np.float32(NEG)).astype(np.float32)
    full = full.reshape(3, 2, 8, SW_UNIT, 3 * SW_UNIT).transpose(1, 0, 4, 2, 3)
    return jnp.asarray(full.reshape(2, 3, 3 * SW_UNIT, 8 * SW_UNIT))


def _post_kernel(o_ref, x_ref, wo_ref, g_ref, wrh_ref, wrl_ref, x1_ref, h_ref, aff_ref, afft_ref):
    oc = jnp.concatenate([o_ref[p] for p in range(N_PLANES)], axis=1)
    x1 = x_ref[...] + jnp.dot(oc, wo_ref[...], preferred_element_type=F32)
    x1_ref[...] = x1
    h = _rmsnorm(x1, g_ref[...])
    hb = h.astype(BF16)
    h_ref[...] = hb
    hl = (h - hb.astype(F32)).astype(BF16)
    logits = jnp.dot(hb, wrh_ref[...], preferred_element_type=F32) + (
        jnp.dot(hb, wrl_ref[...], preferred_element_type=F32)
        + jnp.dot(hl, wrh_ref[...], preferred_element_type=F32))
    lane = lax.broadcasted_iota(I32, logits.shape, 1)
    logits = jnp.where(lane < N_EXPERTS, logits, NEG)
    ex = jnp.exp(logits - jnp.max(logits, axis=1, keepdims=True))
    aff = ex / jnp.sum(ex, axis=1, keepdims=True)
    aff_ref[...] = aff
    afft_ref[...] = aff.T[:N_EXPERTS, :]


def _post_call(o, x, wo, g, wrh, wrl):
    t, d = x.shape
    tm = min(ROW_TILE, t)
    return pl.pallas_call(
        _post_kernel,
        name="post",
        grid=(t // tm,),
        in_specs=[pl.BlockSpec((N_PLANES, tm, LANES), lambda i: (0, i, 0)),
                  pl.BlockSpec((tm, d), lambda i: (i, 0)),
                  pl.BlockSpec((d, d), lambda i: (0, 0)),
                  pl.BlockSpec((1, d), lambda i: (0, 0)),
                  pl.BlockSpec((d, LANES), lambda i: (0, 0)),
                  pl.BlockSpec((d, LANES), lambda i: (0, 0))],
        out_specs=[pl.BlockSpec((tm, d), lambda i: (i, 0)),
                   pl.BlockSpec((tm, d), lambda i: (i, 0)),
                   pl.BlockSpec((tm, LANES), lambda i: (i, 0)),
                   pl.BlockSpec((N_EXPERTS, tm), lambda i: (0, i))],
        out_shape=[jax.ShapeDtypeStruct((t, d), F32),
                   jax.ShapeDtypeStruct((t, d), BF16),
                   jax.ShapeDtypeStruct((t, LANES), F32),
                   jax.ShapeDtypeStruct((N_EXPERTS, t), F32)],
        compiler_params=_cparams("parallel"),
    )(o, x, wo, g, wrh, wrl)


def _threshold_kernel(aff_ref, thr_ref, need_ref, *, cap):
    bits = lax.bitcast_convert_type(aff_ref[...], I32)

    def body(k, cur):
        cand = cur | jnp.left_shift(jnp.int32(1), 30 - k)
        cnt = jnp.sum(jnp.where(bits >= cand, 1, 0), axis=1, keepdims=True)
        return jnp.where(cnt >= cap, cand, cur)

    cur = lax.fori_loop(0, 31, body, jnp.zeros((N_EXPERTS, 1), I32))
    above = jnp.sum(jnp.where(bits > cur, 1, 0), axis=1, keepdims=True)
    thr_ref[...] = jnp.broadcast_to(cur, thr_ref.shape)
    need_ref[...] = jnp.broadcast_to(cap - above, need_ref.shape)


def _threshold_call(afft, cap):
    e, t = afft.shape
    return pl.pallas_call(
        functools.partial(_threshold_kernel, cap=cap),
        name="threshold",
        grid=(1,),
        in_specs=[pl.BlockSpec((e, t), lambda i: (0, 0))],
        out_specs=[pl.BlockSpec((e, LANES), lambda i: (0, 0)),
                   pl.BlockSpec((e, LANES), lambda i: (0, 0))],
        out_shape=[jax.ShapeDtypeStruct((e, LANES), I32),
                   jax.ShapeDtypeStruct((e, LANES), I32)],
        compiler_params=_cparams("arbitrary"),
    )(afft)


def _positions_kernel(aff_ref, thr_ref, need_ref, lpos_ref, lpost_ref, starts_ref, npass_ref, base_sc, eqb_sc):
    @pl.when(pl.program_id(0) == 0)
    def _():
        base_sc[...] = jnp.zeros_like(base_sc)
        eqb_sc[...] = jnp.zeros_like(eqb_sc)

    bits = lax.bitcast_convert_type(aff_ref[...], I32)
    thr = thr_ref[:, 0:1]
    need = need_ref[:, 0:1].astype(F32)
    r = lax.broadcasted_iota(I32, (ROUTE_TILE, ROUTE_TILE), 0)
    c = lax.broadcasted_iota(I32, (ROUTE_TILE, ROUTE_TILE), 1)
    before = jnp.where(r < c, 1.0, 0.0).astype(BF16)
    eq = bits == thr
    eqf = jnp.where(eq, 1.0, 0.0)
    eq_rank = eqb_sc[:, 0:1] + jnp.dot(eqf.astype(BF16), before, preferred_element_type=F32)
    sel = (bits > thr) | (eq & (eq_rank < need))
    self_ = jnp.where(sel, 1.0, 0.0)
    rank = jnp.dot(self_.astype(BF16), before, preferred_element_type=F32)
    cnt = jnp.sum(self_, axis=1, keepdims=True)
    lpos = jnp.where(sel, rank, -1.0)
    lpos_ref[...] = lpos.astype(I32)
    pad = jnp.full((LANES - N_EXPERTS, ROUTE_TILE), -1.0, F32)
    lpost_ref[...] = jnp.concatenate([lpos, pad], axis=0).T
    starts_ref[0] = base_sc[...].astype(I32)
    most = jnp.max(cnt, axis=0, keepdims=True)
    npass_ref[0] = jnp.broadcast_to(jnp.floor((most + (SLAB - 1)) * (1.0 / SLAB)), npass_ref.shape[1:]).astype(I32)
    base_sc[...] = base_sc[...] + cnt
    eqb_sc[...] = eqb_sc[...] + jnp.sum(eqf, axis=1, keepdims=True)


def _positions_call(afft, thr, need):
    e, t = afft.shape
    n_tiles = t // ROUTE_TILE
    return pl.pallas_call(
        _positions_kernel,
        name="positions",
        grid=(n_tiles,),
        in_specs=[pl.BlockSpec((e, ROUTE_TILE), lambda i: (0, i)),
                  pl.BlockSpec((e, LANES), lambda i: (0, 0)),
                  pl.BlockSpec((e, LANES), lambda i: (0, 0))],
        out_specs=[pl.BlockSpec((e, ROUTE_TILE), lambda i: (0, i)),
                   pl.BlockSpec((ROUTE_TILE, LANES), lambda i: (i, 0)),
                   pl.BlockSpec((1, e, LANES), lambda i: (i, 0, 0)),
                   pl.BlockSpec((1, SUBLANES, LANES), lambda i: (i, 0, 0))],
        out_shape=[jax.ShapeDtypeStruct((e, t), I32),
                   jax.ShapeDtypeStruct((t, LANES), F32),
                   jax.ShapeDtypeStruct((n_tiles, e, LANES), I32),
                   jax.ShapeDtypeStruct((n_tiles, SUBLANES, LANES), I32)],
        scratch_shapes=[pltpu.VMEM((e, LANES), F32), pltpu.VMEM((e, LANES), F32)],
        compiler_params=_cparams("arbitrary"),
    )(afft, thr, need)


SLAB_ROWS = SLAB * COL_BLOCKS


def _dispatch_kernel(starts_sm, npass_sm, h_ref, lpos_ref, xe_ref, slab_sc, sem, state_sm, *, cap):
    i = pl.program_id(0)
    sub = lax.broadcasted_iota(I32, (SLAB, ROUTE_TILE), 0)

    def slab_copy(buf, e, dst):
        return pltpu.make_async_copy(
            slab_sc.at[buf, pl.ds(e * SLAB_ROWS, SLAB_ROWS)],
            xe_ref.at[e, pl.ds(pl.multiple_of(dst * COL_BLOCKS, COL_BLOCKS), SLAB_ROWS)],
            sem.at[buf, e])

    def wait_all(buf):
        for e in range(N_EXPERTS):
            slab_copy(buf, e, 0).wait()

    @pl.when(i == 0)
    def _():
        slab_sc[0, pl.ds(0, SLAB_ROWS), :] = jnp.zeros((SLAB_ROWS, LANES), F32)
        pads = [pltpu.make_async_copy(slab_sc.at[0, pl.ds(0, SLAB_ROWS)],
                                      xe_ref.at[e, pl.ds(cap * COL_BLOCKS, SLAB_ROWS)], sem.at[0, e])
                for e in range(N_EXPERTS)]
        for cp in pads:
            cp.start()
        for cp in pads:
            cp.wait()
        state_sm[0] = 0
        state_sm[1] = 0

    def one_pass(s, carry):
        buf = state_sm[1]
        rows = []
        for e in range(N_EXPERTS):
            lp = lpos_ref[e:e + 1, :] - SLAB * s
            rows.append(jnp.where(lp == sub, 1.0, 0.0).astype(BF16))
        onehot = jnp.concatenate(rows, axis=0)
        res = jnp.dot(onehot, h_ref[...], preferred_element_type=F32)
        for cb in range(COL_BLOCKS):
            slab_sc[buf, pl.ds(cb, N_EXPERTS * SLAB, stride=COL_BLOCKS), :] = res[:, cb * LANES:(cb + 1) * LANES]

        @pl.when(state_sm[0] == 1)
        def _():
            wait_all(1 - buf)

        for e in range(N_EXPERTS):
            dst = jnp.minimum(starts_sm[i * N_EXPERTS + e] + SLAB * s, cap)
            slab_copy(buf, e, dst).start()
        state_sm[0] = 1
        state_sm[1] = 1 - buf
        return carry

    lax.fori_loop(0, npass_sm[i], one_pass, 0)

    @pl.when((i == pl.num_programs(0) - 1) & (state_sm[0] == 1))
    def _():
        wait_all(1 - state_sm[1])


def _dispatch_call(starts, npass, h, lpos, cap):
    t, d = h.shape
    n_tiles = t // ROUTE_TILE
    grid_spec = pltpu.PrefetchScalarGridSpec(
        num_scalar_prefetch=2,
        grid=(n_tiles,),
        in_specs=[pl.BlockSpec((ROUTE_TILE, d), lambda i, *_: (i, 0)),
                  pl.BlockSpec((N_EXPERTS, ROUTE_TILE), lambda i, *_: (0, i))],
        out_specs=pl.BlockSpec(memory_space=pl.ANY),
        scratch_shapes=[pltpu.VMEM((2, N_EXPERTS * SLAB_ROWS, LANES), F32),
                        pltpu.SemaphoreType.DMA((2, N_EXPERTS)),
                        pltpu.SMEM((2,), I32)],
    )
    return pl.pallas_call(
        functools.partial(_dispatch_kernel, cap=cap),
        name="dispatch",
        grid_spec=grid_spec,
        out_shape=jax.ShapeDtypeStruct((N_EXPERTS, (cap + SLAB) * COL_BLOCKS, LANES), F32),
        compiler_params=_cparams("arbitrary"),
    )(starts, npass, h, lpos)


def _expert_kernel(x_ref, wg_ref, wu_ref, wd_ref, o_ref):
    tm = x_ref.shape[1] // COL_BLOCKS
    x = jnp.concatenate([x_ref[0, pl.ds(cb, tm, stride=COL_BLOCKS), :] for cb in range(COL_BLOCKS)],
                        axis=1).astype(BF16)
    acc = jnp.zeros((tm, D_MODEL), F32)
    for fc in range(EXPERT_FF // FF_CHUNK):
        fs = slice(fc * FF_CHUNK, (fc + 1) * FF_CHUNK)
        g = jnp.dot(x, wg_ref[0, 0, :, fs], preferred_element_type=F32)
        u = jnp.dot(x, wu_ref[0, 0, :, fs], preferred_element_type=F32)
        hmid = (g * (1.0 / (1.0 + jnp.exp(-g))) * u).astype(BF16)
        acc = acc + jnp.dot(hmid, wd_ref[0, 0, fs, :], preferred_element_type=F32)
    for cb in range(COL_BLOCKS):
        o_ref[0, pl.ds(cb, tm, stride=COL_BLOCKS), :] = acc[:, cb * LANES:(cb + 1) * LANES]


def _expert_call(xe, wg, wu, wd, layer, cap):
    tm = min(ROW_TILE, cap)
    d, f = wg.shape[2], wg.shape[3]
    return pl.pallas_call(
        _expert_kernel,
        name="expert",
        grid=(N_EXPERTS, cap // tm),
        in_specs=[pl.BlockSpec((1, tm * COL_BLOCKS, LANES), lambda e, j: (e, j, 0)),
                  pl.BlockSpec((1, 1, d, f), lambda e, j: (layer, e, 0, 0)),
                  pl.BlockSpec((1, 1, d, f), lambda e, j: (layer, e, 0, 0)),
                  pl.BlockSpec((1, 1, f, d), lambda e, j: (layer, e, 0, 0))],
        out_specs=pl.BlockSpec((1, tm * COL_BLOCKS, LANES), lambda e, j: (e, j, 0)),
        out_shape=jax.ShapeDtypeStruct((N_EXPERTS, cap * COL_BLOCKS, LANES), F32),
        compiler_params=_cparams("parallel", "parallel"),
    )(xe, wg, wu, wd)


def _combine_kernel(starts_sm, npass_sm, x_ref, aff_ref, lpost_ref, expand_ref, g_ref, ye_ref, o_ref,
                    slab_sc, sem, *, cap, final):
    i = pl.program_id(0)
    lane = lax.broadcasted_iota(I32, (1, LANES), 1)
    slot_in_slab = (lax.broadcasted_iota(I32, (1, N_EXPERTS * SLAB), 1) % SLAB).astype(F32)
    aff = aff_ref[...]
    ghi = aff.astype(BF16)
    glo = (aff - ghi.astype(F32)).astype(BF16)
    expand = expand_ref[...]
    ghi_x = jnp.dot(ghi, expand, preferred_element_type=F32)
    glo_x = jnp.dot(glo, expand, preferred_element_type=F32)
    lpost = lpost_ref[...]
    buf = i % 2

    def slab_src(tile, s, e):
        want = starts_sm[tile * N_EXPERTS + e] + SLAB * s
        return want, jnp.minimum(want, cap - SLAB)

    def slab_copy(tile, s, to, e):
        src = slab_src(tile, s, e)[1]
        return pltpu.make_async_copy(
            ye_ref.at[e, pl.ds(pl.multiple_of(src * COL_BLOCKS, COL_BLOCKS), SLAB_ROWS)],
            slab_sc.at[to, pl.ds(e * SLAB_ROWS, SLAB_ROWS)],
            sem.at[to, e])

    def fetch(tile, s, to):
        for e in range(N_EXPERTS):
            slab_copy(tile, s, to, e).start()

    def add_pass(s, acc):
        shift = jnp.zeros((1, LANES), F32)
        for e in range(N_EXPERTS):
            want, src = slab_src(i, s, e)
            shift = jnp.where(lane == e, (want - src).astype(F32), shift)
        lo = jnp.asarray(SLAB * s, F32)
        owned = (lpost >= lo) & (lpost < lo + SLAB)
        row = jnp.where(owned, lpost - lo + shift, -1.0).astype(BF16)
        row_x = jnp.dot(row, expand, preferred_element_type=F32)
        match = row_x == slot_in_slab
        whi = jnp.where(match, ghi_x, 0.0).astype(BF16)
        wlo = jnp.where(match, glo_x, 0.0).astype(BF16)
        for e in range(N_EXPERTS):
            slab_copy(i, s, buf, e).wait()
        rows = jnp.concatenate(
            [slab_sc[buf, pl.ds(cb, N_EXPERTS * SLAB, stride=COL_BLOCKS), :] for cb in range(COL_BLOCKS)],
            axis=1).astype(BF16)
        return acc + (jnp.dot(whi, rows, preferred_element_type=F32)
                      + jnp.dot(wlo, rows, preferred_element_type=F32))

    @pl.when(i == 0)
    def _():
        fetch(0, 0, 0)

    @pl.when(i + 1 < pl.num_programs(0))
    def _():
        fetch(i + 1, 0, 1 - buf)

    y = add_pass(0, jnp.zeros(x_ref.shape, F32))

    def extra_pass(s, acc):
        fetch(i, s, buf)
        return add_pass(s, acc)

    y = lax.fori_loop(1, npass_sm[i], extra_pass, y)
    out = x_ref[...] + y
    if final:
        out = _rmsnorm(out, g_ref[...])
    o_ref[...] = out


def _combine_call(starts, npass, x1, aff, lpost, expand, g, ye, cap, final):
    t, d = x1.shape
    n_tiles = t // ROUTE_TILE
    grid_spec = pltpu.PrefetchScalarGridSpec(
        num_scalar_prefetch=2,
        grid=(n_tiles,),
        in_specs=[pl.BlockSpec((ROUTE_TILE, d), lambda i, *_: (i, 0)),
                  pl.BlockSpec((ROUTE_TILE, LANES), lambda i, *_: (i, 0)),
                  pl.BlockSpec((ROUTE_TILE, LANES), lambda i, *_: (i, 0)),
                  pl.BlockSpec((LANES, N_EXPERTS * SLAB), lambda i, *_: (0, 0)),
                  pl.BlockSpec((1, d), lambda i, *_: (0, 0)),
                  pl.BlockSpec(memory_space=pl.ANY)],
        out_specs=pl.BlockSpec((ROUTE_TILE, d), lambda i, *_: (i, 0)),
        scratch_shapes=[pltpu.VMEM((2, N_EXPERTS * SLAB_ROWS, LANES), F32),
                        pltpu.SemaphoreType.DMA((2, N_EXPERTS))],
    )
    return pl.pallas_call(
        functools.partial(_combine_kernel, cap=cap, final=final),
        name="combine",
        grid_spec=grid_spec,
        out_shape=jax.ShapeDtypeStruct((t, d), F32),
        compiler_params=_cparams("arbitrary"),
    )(starts, npass, x1, aff, lpost, expand, g, ye)


def _moe(x1, h, aff, afft, wg, wu, wd, layer, expand, g_final, final):
    t = x1.shape[0]
    cap = 2 * t // N_EXPERTS
    assert t % ROUTE_TILE == 0 and cap >= SLAB
    thr, need = _threshold_call(afft, cap)
    lpos, lpost, starts3, npass3 = _positions_call(afft, thr, need)
    starts = starts3[:, :, 0].reshape(-1)
    npass = npass3[:, 0, 0]
    xe = _dispatch_call(starts, npass, h, lpos, cap)
    ye = _expert_call(xe, wg, wu, wd, layer, cap)
    return _combine_call(starts, npass, x1, aff, lpost, expand, g_final, ye, cap, final)


def _trunk(x, p):
    batch, seq, d = x.shape
    x = x.reshape(batch * seq, d)
    depth = p["norm_mix"].shape[0]
    for i in range(depth):
        j = i // 2
        if i % 2 == 0:
            qk, vt = _qkv_call(x, p["norm_mix"][i:i + 1], p["a_wqkv"][j], N_PLANES)
            o = _attn_call("na_attn", qk, vt, p["a_bias"][j], None, batch, seq,
                           NA_UNIT, NA_UNITS_PER_STEP, N_PLANES)
            wo = p["a_wo"][j]
        else:
            qk, vt = _qkv_call(x, p["norm_mix"][i:i + 1], p["b_wqkv"][j], 2)
            o = _attn_call("sw_attn", qk, vt, p["b_bias"], p["b_sink"][j], batch, seq,
                           SW_UNIT, SW_UNITS_PER_STEP, 2)
            wo = p["b_wo"][j]
        x1, h, aff, afft = _post_call(o, x, wo, p["norm_ffn"][i:i + 1], p["wr_hi"][i], p["wr_lo"][i])
        x = _moe(x1, h, aff, afft, p["w_gate"], p["w_up"], p["w_down"], i,
                 p["expand"], p["norm_final"], i == depth - 1)
    return x.reshape(batch, seq, d)


def _prepare(norm_mix, norm_ffn, norm_final, a_wqkv, a_wo, a_rpb, b_wqkv, b_wo, b_sink,
             w_router, w_gate, w_up, w_down):
    d = D_MODEL
    scale = HEAD_DIM ** -0.5
    a_w = jnp.concatenate([a_wqkv[:, :, :d] * scale, a_wqkv[:, :, d:]], axis=2).astype(BF16)
    order = SW_HEAD_ORDER
    nb = b_wqkv.shape[0]
    bq = (b_wqkv[:, :, :d] * scale).reshape(nb, d, N_HEADS, HEAD_DIM)[:, :, order].reshape(nb, d, d)
    b_w = jnp.concatenate([bq, b_wqkv[:, :, d:]], axis=2).astype(BF16)
    b_wo_p = b_wo.reshape(nb, N_HEADS, HEAD_DIM, d)[:, order].reshape(nb, d, d).astype(BF16)
    sink = jnp.broadcast_to(b_sink.astype(F32)[:, order].reshape(nb, 2, 1, 8, 1), (nb, 2, SUBLANES, 8, SW_UNIT))
    sink = sink.reshape(nb, 2, SUBLANES, 8 * SW_UNIT)
    wr = jnp.pad(w_router.astype(F32), ((0, 0), (0, 0), (0, LANES - N_EXPERTS)))
    wr_hi = wr.astype(BF16)
    wr_lo = (wr - wr_hi.astype(F32)).astype(BF16)
    lane_expert = np.arange(N_EXPERTS * SLAB) // SLAB
    expand = jnp.asarray(np.arange(LANES)[:, None] == lane_expert[None, :], dtype=BF16)
    return {
        "norm_mix": norm_mix.astype(F32), "norm_ffn": norm_ffn.astype(F32),
        "norm_final": norm_final.astype(F32).reshape(1, d),
        "a_wqkv": a_w, "a_wo": a_wo.astype(BF16),
        "a_bias": jnp.stack([_na_bias_table(a_rpb[j]) for j in range(a_rpb.shape[0])]),
        "b_wqkv": b_w, "b_wo": b_wo_p, "b_bias": _sw_bias_table(), "b_sink": sink,
        "wr_hi": wr_hi, "wr_lo": wr_lo,
        "w_gate": w_gate.astype(BF16), "w_up": w_up.astype(BF16), "w_down": w_down.astype(BF16),
        "expand": expand,
    }


def kernel(x_prompt, x_sample, norm_mix, norm_ffn, norm_final, a_wqkv, a_wo, a_rpb, b_wqkv, b_wo, b_sink,
           w_router, w_gate, w_up, w_down):
    p = _prepare(norm_mix, norm_ffn, norm_final, a_wqkv, a_wo, a_rpb, b_wqkv, b_wo, b_sink,
                 w_router, w_gate, w_up, w_down)
    return (_trunk(x_prompt, p), _trunk(x_sample, p))
```

```python
import functools

import numpy as np
import jax
import jax.numpy as jnp
from jax import lax
from jax.experimental import pallas as pl
from jax.experimental.pallas import tpu as pltpu

F32 = jnp.float32
BF16 = jnp.bfloat16
I32 = jnp.int32

D_MODEL = 1024
HEAD_DIM = 64
N_HEADS = 16
LANES = 128
SUBLANES = 8
N_PLANES = D_MODEL // LANES
COL_BLOCKS = D_MODEL // LANES
GRID_W = 64
NA_KH = 8
NA_KW = 16
NA_UNIT = 4 * GRID_W
NA_UNITS_PER_STEP = 4
SW_KV_HEADS = 4
SW_WINDOW = 128
SW_UNIT = 128
SW_UNITS_PER_STEP = 4
N_EXPERTS = 16
EXPERT_FF = 2048
FF_CHUNK = 512
ROUTE_TILE = 256
POSITION_TILES_PER_STEP = 8
SLAB = 64
RMS_EPS = 1e-6
NEG = -1e30
LOG2E = 1.4426950408889634
ROW_TILE = 512
VMEM_LIMIT = 56 * 1024 * 1024


def _cparams(*sem):
    return pltpu.CompilerParams(dimension_semantics=sem, vmem_limit_bytes=VMEM_LIMIT)


def _rmsnorm(x, g):
    return x * lax.rsqrt(jnp.mean(x * x, axis=-1, keepdims=True) + RMS_EPS) * g


def _qkv_kernel(x_ref, g_ref, w_ref, qk_ref, vt_ref):
    y = _rmsnorm(x_ref[...], g_ref[...]).astype(BF16)
    r = jnp.dot(y, w_ref[...], preferred_element_type=F32)
    n_qk = qk_ref.shape[0]
    for p in range(n_qk):
        qk_ref[p] = r[:, p * LANES:(p + 1) * LANES].astype(BF16)
    for p in range(vt_ref.shape[0]):
        vt_ref[p] = r[:, (n_qk + p) * LANES:(n_qk + p + 1) * LANES].T.astype(BF16)


def _qkv_call(x, g, w, n_v_planes):
    t, d = x.shape
    n_qk = w.shape[1] // LANES - n_v_planes
    tm = min(ROW_TILE, t)
    return pl.pallas_call(
        _qkv_kernel,
        name="qkv",
        grid=(t // tm,),
        in_specs=[pl.BlockSpec((tm, d), lambda i: (i, 0)),
                  pl.BlockSpec((1, d), lambda i: (0, 0)),
                  pl.BlockSpec((d, w.shape[1]), lambda i: (0, 0))],
        out_specs=[pl.BlockSpec((n_qk, tm, LANES), lambda i: (0, i, 0)),
                   pl.BlockSpec((n_v_planes, LANES, tm), lambda i: (0, 0, i))],
        out_shape=[jax.ShapeDtypeStruct((n_qk, t, LANES), BF16),
                   jax.ShapeDtypeStruct((n_v_planes, LANES, t), BF16)],
        compiler_params=_cparams("parallel"),
    )(x, g, w)


def _edge_case(u, n_units):
    i = pl.program_id(2)
    last = pl.num_programs(2) - 1
    if n_units == 1:
        return jnp.where(i == 0, 1, jnp.where(i == last, 2, 0))
    if u == 0:
        return jnp.where(i == 0, 1, 0)
    if u == n_units - 1:
        return jnp.where(i == last, 2, 0)
    return 0


def _attn_kernel(*refs, unit, n_units, has_sink):
    q_ref, kp_ref, kc_ref, kn_ref, vp_ref, vc_ref, vn_ref, bias_ref = refs[:8]
    sink_ref = refs[8] if has_sink else None
    o_ref = refs[-1]
    n_planes = q_ref.shape[0]
    kcat = jnp.concatenate([kp_ref[0], kc_ref[0], kn_ref[0]], axis=0)
    vcat = jnp.concatenate([vp_ref[0], vc_ref[0], vn_ref[0]], axis=1)
    vcat = jnp.concatenate([vcat, jnp.ones((SUBLANES, vcat.shape[1]), BF16)], axis=0)
    lo_half = lax.broadcasted_iota(I32, (unit, LANES), 1) < HEAD_DIM
    zero = jnp.zeros((unit, LANES), BF16)
    def scores(u):
        kw = kcat[u * unit:(u + 3) * unit]
        qs = [q_ref[pi, u * unit:(u + 1) * unit, :] for pi in range(n_planes)]
        q_stack = jnp.concatenate(
            [jnp.where(lo_half if hh == 0 else jnp.logical_not(lo_half), q, zero) for q in qs for hh in range(2)],
            axis=0)
        return lax.dot_general(kw, q_stack, (((1,), (1,)), ((), ())), preferred_element_type=F32)

    st_next = scores(0)
    for u in range(n_units):
        case = _edge_case(u, n_units)
        vw = vcat[:, u * unit:(u + 3) * unit]
        st = st_next + bias_ref[0, case]
        if u + 1 < n_units:
            st_next = scores(u + 1)
        m = jnp.max(st, axis=0, keepdims=True)
        if has_sink:
            sink = sink_ref[0, 0:1, :]
            m = jnp.maximum(m, sink)
        e = jnp.exp2(st - m)
        ot = jnp.dot(vw, e.astype(BF16), preferred_element_type=F32)
        l = ot[LANES:LANES + 1, :]
        if has_sink:
            l = l + jnp.exp2(sink - m)
        ot = ot[0:LANES, :] / l
        for pi in range(n_planes):
            both = jnp.concatenate(
                [ot[0:HEAD_DIM, (2 * pi) * unit:(2 * pi + 1) * unit],
                 ot[HEAD_DIM:LANES, (2 * pi + 1) * unit:(2 * pi + 2) * unit]], axis=0)
            o_ref[pi, u * unit:(u + 1) * unit, :] = both.T.astype(BF16)


def _attn_call(name, qk, vt, bias, sink, batch, seq, unit, n_units, n_groups):
    t = batch * seq
    units = seq // unit
    n_units = min(n_units, units)
    steps = units // n_units
    assert seq % (unit * n_units) == 0 and units >= 4
    ppg = N_PLANES // n_groups
    qb = unit * n_units
    width = 2 * ppg * unit

    def cur(g, b, i):
        return b * steps + i

    def prev(g, b, i):
        return b * units + jnp.maximum(n_units * i - 1, 0)

    def nxt(g, b, i):
        return b * units + jnp.minimum(n_units * i + n_units, units - 1)

    in_specs = [pl.BlockSpec((ppg, qb, LANES), lambda g, b, i: (g, cur(g, b, i), 0)),
                pl.BlockSpec((1, unit, LANES), lambda g, b, i: (N_PLANES + g, prev(g, b, i), 0)),
                pl.BlockSpec((1, qb, LANES), lambda g, b, i: (N_PLANES + g, cur(g, b, i), 0)),
                pl.BlockSpec((1, unit, LANES), lambda g, b, i: (N_PLANES + g, nxt(g, b, i), 0)),
                pl.BlockSpec((1, LANES, unit), lambda g, b, i: (g, 0, prev(g, b, i))),
                pl.BlockSpec((1, LANES, qb), lambda g, b, i: (g, 0, cur(g, b, i))),
                pl.BlockSpec((1, LANES, unit), lambda g, b, i: (g, 0, nxt(g, b, i))),
                pl.BlockSpec((1, 3, 3 * unit, width), lambda g, b, i: (g, 0, 0, 0))]
    args = [qk, qk, qk, qk, vt, vt, vt, bias]
    if sink is not None:
        in_specs.append(pl.BlockSpec((1, SUBLANES, width), lambda g, b, i: (g, 0, 0)))
        args.append(sink)
    return pl.pallas_call(
        functools.partial(_attn_kernel, unit=unit, n_units=n_units, has_sink=sink is not None),
        name=name,
        grid=(n_groups, batch, steps),
        in_specs=in_specs,
        out_specs=pl.BlockSpec((ppg, qb, LANES), lambda g, b, i: (g, cur(g, b, i), 0)),
        out_shape=jax.ShapeDtypeStruct((N_PLANES, t, LANES), BF16),
        compiler_params=_cparams("arbitrary", "arbitrary", "arbitrary"),
    )(*args)


def _na_bias_table(rpb):
    a = np.arange(NA_UNIT) // GRID_W
    j = np.arange(3 * NA_UNIT) // GRID_W
    row_ok = np.stack([
        (j[None, :] >= a[:, None]) & (j[None, :] < a[:, None] + NA_KH),
        np.broadcast_to((j[None, :] >= 4) & (j[None, :] < 4 + NA_KH), (NA_UNIT, 3 * NA_UNIT)),
        np.broadcast_to(j[None, :] < NA_KH, (NA_UNIT, 3 * NA_UNIT)),
    ])
    w = np.arange(GRID_W)
    dc = np.clip(w[None, :] - w[:, None], -(NA_KW - 1), NA_KW - 1) + NA_KW - 1
    c0 = np.clip(w - NA_KW // 2, 0, GRID_W - NA_KW)
    col_ok = (w[None, :] >= c0[:, None]) & (w[None, :] < c0[:, None] + NA_KW)
    cols = jnp.take(rpb.astype(F32), jnp.asarray(dc.reshape(-1)), axis=2)
    cols = cols.reshape(N_HEADS, 2 * NA_KH - 1, GRID_W, GRID_W)
    cols = jnp.where(jnp.asarray(col_ok), cols, NEG)
    rows = jnp.stack([cols[:, 3 - ai:15 - ai] for ai in range(4)], axis=1)
    tbl = rows.transpose(0, 2, 4, 1, 3).reshape(N_HEADS, 3 * NA_UNIT, NA_UNIT)
    ok = jnp.asarray(row_ok.transpose(0, 2, 1))
    full = jnp.where(ok[:, None], tbl[None], NEG)
    full = full.reshape(3, N_PLANES, 2, 3 * NA_UNIT, NA_UNIT).transpose(1, 0, 3, 2, 4)
    return full.reshape(N_PLANES, 3, 3 * NA_UNIT, 2 * NA_UNIT)


SW_HEAD_ORDER = np.array([8 * m + 4 * hh + i for m in range(2) for i in range(4) for hh in range(2)])


def _sw_bias_table():
    slopes = np.asarray(2.0 ** (-8.0 * np.arange(1, N_HEADS + 1) / N_HEADS), dtype=np.float32)
    ks = np.arange(3 * SW_UNIT) - SW_WINDOW
    dist = ks[None, :] - np.arange(SW_UNIT)[:, None]
    win_ok = np.abs(dist) <= SW_WINDOW
    pos_ok = np.stack([np.ones(3 * SW_UNIT, bool), ks >= 0, ks < SW_UNIT])
    ok = win_ok[None] & pos_ok[:, None, :]
    alibi = -slopes[SW_HEAD_ORDER][:, None, None] * np.abs(dist).astype(np.float32)[None]
    alibi = alibi * np.float32(LOG2E)
    full = np.where(ok[:, None], alibi[None], np.float32(NEG)).astype(np.float32)
    full = full.reshape(3, 2, 8, SW_UNIT, 3 * SW_UNIT).transpose(1, 0, 4, 2, 3)
    return jnp.asarray(full.reshape(2, 3, 3 * SW_UNIT, 8 * SW_UNIT))


def _post_kernel(o_ref, x_ref, wo_ref, g_ref, wrh_ref, wrhl_ref, x1_ref, h_ref, aff_ref, afft_ref):
    oc = jnp.concatenate([o_ref[p] for p in range(N_PLANES)], axis=1)
    x1 = x_ref[...] + jnp.dot(oc, wo_ref[...], preferred_element_type=F32)
    x1_ref[...] = x1
    h = _rmsnorm(x1, g_ref[...])
    hb = h.astype(BF16)
    h_ref[...] = hb
    hl = (h - hb.astype(F32)).astype(BF16)
    nt = (((1,), (1,)), ((), ()))
    by_hb = lax.dot_general(wrhl_ref[...], hb, nt, preferred_element_type=F32)
    by_hl = lax.dot_general(wrh_ref[...], hl, nt, preferred_element_type=F32)
    logits = by_hb[0:N_EXPERTS] + (by_hb[N_EXPERTS:2 * N_EXPERTS] + by_hl)
    ex = jnp.exp(logits - jnp.max(logits, axis=0, keepdims=True))
    afft = ex / jnp.sum(ex, axis=0, keepdims=True)
    afft_ref[...] = afft
    pad = jnp.zeros((LANES - N_EXPERTS, afft.shape[1]), F32)
    aff_ref[...] = jnp.concatenate([afft, pad], axis=0).T


def _post_call(o, x, wo, g, wrh, wrhl):
    t, d = x.shape
    tm = min(ROW_TILE, t)
    return pl.pallas_call(
        _post_kernel,
        name="post",
        grid=(t // tm,),
        in_specs=[pl.BlockSpec((N_PLANES, tm, LANES), lambda i: (0, i, 0)),
                  pl.BlockSpec((tm, d), lambda i: (i, 0)),
                  pl.BlockSpec((d, d), lambda i: (0, 0)),
                  pl.BlockSpec((1, d), lambda i: (0, 0)),
                  pl.BlockSpec((N_EXPERTS, d), lambda i: (0, 0)),
                  pl.BlockSpec((2 * N_EXPERTS, d), lambda i: (0, 0))],
        out_specs=[pl.BlockSpec((tm, d), lambda i: (i, 0)),
                   pl.BlockSpec((tm, d), lambda i: (i, 0)),
                   pl.BlockSpec((tm, LANES), lambda i: (i, 0)),
                   pl.BlockSpec((N_EXPERTS, tm), lambda i: (0, i))],
        out_shape=[jax.ShapeDtypeStruct((t, d), F32),
                   jax.ShapeDtypeStruct((t, d), BF16),
                   jax.ShapeDtypeStruct((t, LANES), F32),
                   jax.ShapeDtypeStruct((N_EXPERTS, t), F32)],
        compiler_params=_cparams("parallel"),
    )(o, x, wo, g, wrh, wrhl)


def _threshold_kernel(aff_ref, thr_ref, need_ref, *, cap):
    bits = lax.bitcast_convert_type(aff_ref[...], I32)

    def body(k, cur):
        cand = cur | jnp.left_shift(jnp.int32(1), 30 - k)
        cnt = jnp.sum(jnp.where(bits >= cand, 1, 0), axis=1, keepdims=True)
        return jnp.where(cnt >= cap, cand, cur)

    cur = lax.fori_loop(0, 31, body, jnp.zeros((N_EXPERTS, 1), I32))
    above = jnp.sum(jnp.where(bits > cur, 1, 0), axis=1, keepdims=True)
    thr_ref[...] = jnp.broadcast_to(cur, thr_ref.shape)
    need_ref[...] = jnp.broadcast_to(cap - above, need_ref.shape)


def _threshold_call(afft, cap):
    e, t = afft.shape
    return pl.pallas_call(
        functools.partial(_threshold_kernel, cap=cap),
        name="threshold",
        grid=(1,),
        in_specs=[pl.BlockSpec((e, t), lambda i: (0, 0))],
        out_specs=[pl.BlockSpec((e, LANES), lambda i: (0, 0)),
                   pl.BlockSpec((e, LANES), lambda i: (0, 0))],
        out_shape=[jax.ShapeDtypeStruct((e, LANES), I32),
                   jax.ShapeDtypeStruct((e, LANES), I32)],
        compiler_params=_cparams("arbitrary"),
    )(afft)


def _positions_kernel(aff_ref, thr_ref, need_ref, lpos_ref, lpost_ref, starts_ref, npass_ref, base_sc, eqb_sc):
    @pl.when(pl.program_id(0) == 0)
    def _():
        base_sc[...] = jnp.zeros_like(base_sc)
        eqb_sc[...] = jnp.zeros_like(eqb_sc)

    thr = thr_ref[:, 0:1]
    need = need_ref[:, 0:1].astype(F32)
    r = lax.broadcasted_iota(I32, (ROUTE_TILE, ROUTE_TILE), 0)
    c = lax.broadcasted_iota(I32, (ROUTE_TILE, ROUTE_TILE), 1)
    before = jnp.where(r < c, 1.0, 0.0).astype(BF16)
    pad = jnp.full((LANES - N_EXPERTS, ROUTE_TILE), -1.0, F32)
    base = base_sc[...]
    eqb = eqb_sc[...]
    for k in range(starts_ref.shape[0]):
        cols = slice(k * ROUTE_TILE, (k + 1) * ROUTE_TILE)
        bits = lax.bitcast_convert_type(aff_ref[:, cols], I32)
        eq = bits == thr
        eqf = jnp.where(eq, 1.0, 0.0)
        eq_rank = eqb[:, 0:1] + jnp.dot(eqf.astype(BF16), before, preferred_element_type=F32)
        sel = (bits > thr) | (eq & (eq_rank < need))
        self_ = jnp.where(sel, 1.0, 0.0)
        rank = jnp.dot(self_.astype(BF16), before, preferred_element_type=F32)
        cnt = jnp.sum(self_, axis=1, keepdims=True)
        lpos = jnp.where(sel, rank, -1.0)
        lpos_ref[:, cols] = lpos.astype(I32)
        lpost_ref[cols, :] = jnp.concatenate([lpos, pad], axis=0).T
        starts_ref[k] = base.astype(I32)
        most = jnp.max(cnt, axis=0, keepdims=True)
        npass_ref[k] = jnp.broadcast_to(jnp.floor((most + (SLAB - 1)) * (1.0 / SLAB)),
                                        npass_ref.shape[1:]).astype(I32)
        base = base + cnt
        eqb = eqb + jnp.sum(eqf, axis=1, keepdims=True)
    base_sc[...] = base
    eqb_sc[...] = eqb


def _positions_call(afft, thr, need):
    e, t = afft.shape
    n_tiles = t // ROUTE_TILE
    per_step = min(POSITION_TILES_PER_STEP, n_tiles)
    assert n_tiles % per_step == 0
    span = per_step * ROUTE_TILE
    return pl.pallas_call(
        _positions_kernel,
        name="positions",
        grid=(n_tiles // per_step,),
        in_specs=[pl.BlockSpec((e, span), lambda i: (0, i)),
                  pl.BlockSpec((e, LANES), lambda i: (0, 0)),
                  pl.BlockSpec((e, LANES), lambda i: (0, 0))],
        out_specs=[pl.BlockSpec((e, span), lambda i: (0, i)),
                   pl.BlockSpec((span, LANES), lambda i: (i, 0)),
                   pl.BlockSpec((per_step, e, LANES), lambda i: (i, 0, 0)),
                   pl.BlockSpec((per_step, SUBLANES, LANES), lambda i: (i, 0, 0))],
        out_shape=[jax.ShapeDtypeStruct((e, t), I32),
                   jax.ShapeDtypeStruct((t, LANES), F32),
                   jax.ShapeDtypeStruct((n_tiles, e, LANES), I32),
                   jax.ShapeDtypeStruct((n_tiles, SUBLANES, LANES), I32)],
        scratch_shapes=[pltpu.VMEM((e, LANES), F32), pltpu.VMEM((e, LANES), F32)],
        compiler_params=_cparams("arbitrary"),
    )(afft, thr, need)


SLAB_ROWS = SLAB * COL_BLOCKS


def _dispatch_kernel(starts_sm, npass_sm, h_ref, lpos_ref, xe_ref, slab_sc, sem, state_sm, *, cap):
    i = pl.program_id(0)
    sub = lax.broadcasted_iota(I32, (SLAB, ROUTE_TILE), 0)

    def slab_copy(buf, e, dst):
        return pltpu.make_async_copy(
            slab_sc.at[buf, pl.ds(e * SLAB_ROWS, SLAB_ROWS)],
            xe_ref.at[e, pl.ds(pl.multiple_of(dst * COL_BLOCKS, COL_BLOCKS), SLAB_ROWS)],
            sem.at[buf, e])

    def wait_all(buf):
        for e in range(N_EXPERTS):
            slab_copy(buf, e, 0).wait()

    @pl.when(i == 0)
    def _():
        slab_sc[0, pl.ds(0, SLAB_ROWS), :] = jnp.zeros((SLAB_ROWS, LANES), F32)
        pads = [pltpu.make_async_copy(slab_sc.at[0, pl.ds(0, SLAB_ROWS)],
                                      xe_ref.at[e, pl.ds(cap * COL_BLOCKS, SLAB_ROWS)], sem.at[0, e])
                for e in range(N_EXPERTS)]
        for cp in pads:
            cp.start()
        for cp in pads:
            cp.wait()
        state_sm[0] = 0
        state_sm[1] = 0

    def one_pass(s, carry):
        buf = state_sm[1]
        rows = []
        for e in range(N_EXPERTS):
            lp = lpos_ref[e:e + 1, :] - SLAB * s
            rows.append(jnp.where(lp == sub, 1.0, 0.0).astype(BF16))
        onehot = jnp.concatenate(rows, axis=0)
        res = jnp.dot(onehot, h_ref[...], preferred_element_type=F32)
        for cb in range(COL_BLOCKS):
            slab_sc[buf, pl.ds(cb, N_EXPERTS * SLAB, stride=COL_BLOCKS), :] = res[:, cb * LANES:(cb + 1) * LANES]

        @pl.when(state_sm[0] == 1)
        def _():
            wait_all(1 - buf)

        for e in range(N_EXPERTS):
            dst = jnp.minimum(starts_sm[i * N_EXPERTS + e] + SLAB * s, cap)
            slab_copy(buf, e, dst).start()
        state_sm[0] = 1
        state_sm[1] = 1 - buf
        return carry

    lax.fori_loop(0, npass_sm[i], one_pass, 0)

    @pl.when((i == pl.num_programs(0) - 1) & (state_sm[0] == 1))
    def _():
        wait_all(1 - state_sm[1])


def _dispatch_call(starts, npass, h, lpos, cap):
    t, d = h.shape
    n_tiles = t // ROUTE_TILE
    grid_spec = pltpu.PrefetchScalarGridSpec(
        num_scalar_prefetch=2,
        grid=(n_tiles,),
        in_specs=[pl.BlockSpec((ROUTE_TILE, d), lambda i, *_: (i, 0)),
                  pl.BlockSpec((N_EXPERTS, ROUTE_TILE), lambda i, *_: (0, i))],
        out_specs=pl.BlockSpec(memory_space=pl.ANY),
        scratch_shapes=[pltpu.VMEM((2, N_EXPERTS * SLAB_ROWS, LANES), F32),
                        pltpu.SemaphoreType.DMA((2, N_EXPERTS)),
                        pltpu.SMEM((2,), I32)],
    )
    return pl.pallas_call(
        functools.partial(_dispatch_kernel, cap=cap),
        name="dispatch",
        grid_spec=grid_spec,
        out_shape=jax.ShapeDtypeStruct((N_EXPERTS, (cap + SLAB) * COL_BLOCKS, LANES), F32),
        compiler_params=_cparams("arbitrary"),
    )(starts, npass, h, lpos)


def _expert_kernel(x_ref, wg_ref, wu_ref, wd_ref, o_ref):
    tm = x_ref.shape[1] // COL_BLOCKS
    x = jnp.concatenate([x_ref[0, pl.ds(cb, tm, stride=COL_BLOCKS), :] for cb in range(COL_BLOCKS)],
                        axis=1).astype(BF16)
    acc = jnp.zeros((tm, D_MODEL), F32)
    for fc in range(EXPERT_FF // FF_CHUNK):
        fs = slice(fc * FF_CHUNK, (fc + 1) * FF_CHUNK)
        g = jnp.dot(x, wg_ref[0, 0, :, fs], preferred_element_type=F32)
        u = jnp.dot(x, wu_ref[0, 0, :, fs], preferred_element_type=F32)
        hmid = (g * (1.0 / (1.0 + jnp.exp(-g))) * u).astype(BF16)
        acc = acc + jnp.dot(hmid, wd_ref[0, 0, fs, :], preferred_element_type=F32)
    for cb in range(COL_BLOCKS):
        o_ref[0, pl.ds(cb, tm, stride=COL_BLOCKS), :] = acc[:, cb * LANES:(cb + 1) * LANES]


def _expert_call(xe, wg, wu, wd, layer, cap):
    tm = min(ROW_TILE, cap)
    d, f = wg.shape[2], wg.shape[3]
    return pl.pallas_call(
        _expert_kernel,
        name="expert",
        grid=(N_EXPERTS, cap // tm),
        in_specs=[pl.BlockSpec((1, tm * COL_BLOCKS, LANES), lambda e, j: (e, j, 0)),
                  pl.BlockSpec((1, 1, d, f), lambda e, j: (layer, e, 0, 0)),
                  pl.BlockSpec((1, 1, d, f), lambda e, j: (layer, e, 0, 0)),
                  pl.BlockSpec((1, 1, f, d), lambda e, j: (layer, e, 0, 0))],
        out_specs=pl.BlockSpec((1, tm * COL_BLOCKS, LANES), lambda e, j: (e, j, 0)),
        out_shape=jax.ShapeDtypeStruct((N_EXPERTS, cap * COL_BLOCKS, LANES), F32),
        compiler_params=_cparams("parallel", "parallel"),
    )(xe, wg, wu, wd)


def _combine_kernel(starts_sm, npass_sm, x_ref, aff_ref, lpost_ref, expand_ref, g_ref, ye_ref, o_ref,
                    slab_sc, sem, *, cap, final):
    i = pl.program_id(0)
    lane = lax.broadcasted_iota(I32, (1, LANES), 1)
    slot_in_slab = (lax.broadcasted_iota(I32, (1, N_EXPERTS * SLAB), 1) % SLAB).astype(F32)
    aff = aff_ref[...]
    ghi = aff.astype(BF16)
    glo = (aff - ghi.astype(F32)).astype(BF16)
    expand = expand_ref[...]
    ghi_x = jnp.dot(ghi, expand, preferred_element_type=F32)
    glo_x = jnp.dot(glo, expand, preferred_element_type=F32)
    lpost = lpost_ref[...]
    buf = i % 2

    def slab_src(tile, s, e):
        want = starts_sm[tile * N_EXPERTS + e] + SLAB * s
        return want, jnp.minimum(want, cap - SLAB)

    def slab_copy(tile, s, to, e):
        src = slab_src(tile, s, e)[1]
        return pltpu.make_async_copy(
            ye_ref.at[e, pl.ds(pl.multiple_of(src * COL_BLOCKS, COL_BLOCKS), SLAB_ROWS)],
            slab_sc.at[to, pl.ds(e * SLAB_ROWS, SLAB_ROWS)],
            sem.at[to, e])

    def fetch(tile, s, to):
        for e in range(N_EXPERTS):
            slab_copy(tile, s, to, e).start()

    def add_pass(s, acc):
        shift = jnp.zeros((1, LANES), F32)
        for e in range(N_EXPERTS):
            want, src = slab_src(i, s, e)
            shift = jnp.where(lane == e, (want - src).astype(F32), shift)
        lo = jnp.asarray(SLAB * s, F32)
        owned = (lpost >= lo) & (lpost < lo + SLAB)
        row = jnp.where(owned, lpost - lo + shift, -1.0).astype(BF16)
        row_x = jnp.dot(row, expand, preferred_element_type=F32)
        match = row_x == slot_in_slab
        whi = jnp.where(match, ghi_x, 0.0).astype(BF16)
        wlo = jnp.where(match, glo_x, 0.0).astype(BF16)
        for e in range(N_EXPERTS):
            slab_copy(i, s, buf, e).wait()
        rows = jnp.concatenate(
            [slab_sc[buf, pl.ds(cb, N_EXPERTS * SLAB, stride=COL_BLOCKS), :] for cb in range(COL_BLOCKS)],
            axis=1).astype(BF16)
        return acc + (jnp.dot(whi, rows, preferred_element_type=F32)
                      + jnp.dot(wlo, rows, preferred_element_type=F32))

    @pl.when(i == 0)
    def _():
        fetch(0, 0, 0)

    @pl.when(i + 1 < pl.num_programs(0))
    def _():
        fetch(i + 1, 0, 1 - buf)

    y = add_pass(0, jnp.zeros(x_ref.shape, F32))

    def extra_pass(s, acc):
        fetch(i, s, buf)
        return add_pass(s, acc)

    y = lax.fori_loop(1, npass_sm[i], extra_pass, y)
    out = x_ref[...] + y
    if final:
        out = _rmsnorm(out, g_ref[...])
    o_ref[...] = out


def _combine_call(starts, npass, x1, aff, lpost, expand, g, ye, cap, final):
    t, d = x1.shape
    n_tiles = t // ROUTE_TILE
    grid_spec = pltpu.PrefetchScalarGridSpec(
        num_scalar_prefetch=2,
        grid=(n_tiles,),
        in_specs=[pl.BlockSpec((ROUTE_TILE, d), lambda i, *_: (i, 0)),
                  pl.BlockSpec((ROUTE_TILE, LANES), lambda i, *_: (i, 0)),
                  pl.BlockSpec((ROUTE_TILE, LANES), lambda i, *_: (i, 0)),
                  pl.BlockSpec((LANES, N_EXPERTS * SLAB), lambda i, *_: (0, 0)),
                  pl.BlockSpec((1, d), lambda i, *_: (0, 0)),
                  pl.BlockSpec(memory_space=pl.ANY)],
        out_specs=pl.BlockSpec((ROUTE_TILE, d), lambda i, *_: (i, 0)),
        scratch_shapes=[pltpu.VMEM((2, N_EXPERTS * SLAB_ROWS, LANES), F32),
                        pltpu.SemaphoreType.DMA((2, N_EXPERTS))],
    )
    return pl.pallas_call(
        functools.partial(_combine_kernel, cap=cap, final=final),
        name="combine",
        grid_spec=grid_spec,
        out_shape=jax.ShapeDtypeStruct((t, d), F32),
        compiler_params=_cparams("arbitrary"),
    )(starts, npass, x1, aff, lpost, expand, g, ye)


def _moe(x1, h, aff, afft, wg, wu, wd, layer, expand, g_final, final):
    t = x1.shape[0]
    cap = 2 * t // N_EXPERTS
    assert t % ROUTE_TILE == 0 and cap >= SLAB
    thr, need = _threshold_call(afft, cap)
    lpos, lpost, starts3, npass3 = _positions_call(afft, thr, need)
    starts = starts3[:, :, 0].reshape(-1)
    npass = npass3[:, 0, 0]
    xe = _dispatch_call(starts, npass, h, lpos, cap)
    ye = _expert_call(xe, wg, wu, wd, layer, cap)
    return _combine_call(starts, npass, x1, aff, lpost, expand, g_final, ye, cap, final)


def _trunk(x, p):
    batch, seq, d = x.shape
    x = x.reshape(batch * seq, d)
    depth = p["norm_mix"].shape[0]
    for i in range(depth):
        j = i // 2
        if i % 2 == 0:
            qk, vt = _qkv_call(x, p["norm_mix"][i:i + 1], p["a_wqkv"][j], N_PLANES)
            o = _attn_call("na_attn", qk, vt, p["a_bias"][j], None, batch, seq,
                           NA_UNIT, NA_UNITS_PER_STEP, N_PLANES)
            wo = p["a_wo"][j]
        else:
            qk, vt = _qkv_call(x, p["norm_mix"][i:i + 1], p["b_wqkv"][j], 2)
            o = _attn_call("sw_attn", qk, vt, p["b_bias"], p["b_sink"][j], batch, seq,
                           SW_UNIT, SW_UNITS_PER_STEP, 2)
            wo = p["b_wo"][j]
        x1, h, aff, afft = _post_call(o, x, wo, p["norm_ffn"][i:i + 1], p["wr_hi"][i], p["wr_hl"][i])
        x = _moe(x1, h, aff, afft, p["w_gate"], p["w_up"], p["w_down"], i,
                 p["expand"], p["norm_final"], i == depth - 1)
    return x.reshape(batch, seq, d)


def _prepare(norm_mix, norm_ffn, norm_final, a_wqkv, a_wo, a_rpb, b_wqkv, b_wo, b_sink,
             w_router, w_gate, w_up, w_down):
    d = D_MODEL
    scale = HEAD_DIM ** -0.5 * LOG2E
    a_w = jnp.concatenate([a_wqkv[:, :, :d] * scale, a_wqkv[:, :, d:]], axis=2).astype(BF16)
    order = SW_HEAD_ORDER
    nb = b_wqkv.shape[0]
    bq = (b_wqkv[:, :, :d] * scale).reshape(nb, d, N_HEADS, HEAD_DIM)[:, :, order].reshape(nb, d, d)
    b_w = jnp.concatenate([bq, b_wqkv[:, :, d:]], axis=2).astype(BF16)
    b_wo_p = b_wo.reshape(nb, N_HEADS, HEAD_DIM, d)[:, order].reshape(nb, d, d).astype(BF16)
    sink = (b_sink.astype(F32) * LOG2E)[:, order].reshape(nb, 2, 1, 8, 1)
    sink = jnp.broadcast_to(sink, (nb, 2, SUBLANES, 8, SW_UNIT)).reshape(nb, 2, SUBLANES, 8 * SW_UNIT)
    wr = w_router.astype(F32).transpose(0, 2, 1)
    wr_hi = wr.astype(BF16)
    wr_lo = (wr - wr_hi.astype(F32)).astype(BF16)
    wr_hl = jnp.concatenate([wr_hi, wr_lo], axis=1)
    lane_expert = np.arange(N_EXPERTS * SLAB) // SLAB
    expand = jnp.asarray(np.arange(LANES)[:, None] == lane_expert[None, :], dtype=BF16)
    return {
        "norm_mix": norm_mix.astype(F32), "norm_ffn": norm_ffn.astype(F32),
        "norm_final": norm_final.astype(F32).reshape(1, d),
        "a_wqkv": a_w, "a_wo": a_wo.astype(BF16),
        "a_bias": jnp.stack([_na_bias_table(a_rpb[j] * LOG2E) for j in range(a_rpb.shape[0])]),
        "b_wqkv": b_w, "b_wo": b_wo_p, "b_bias": _sw_bias_table(), "b_sink": sink,
        "wr_hi": wr_hi, "wr_hl": wr_hl,
        "w_gate": w_gate.astype(BF16), "w_up": w_up.astype(BF16), "w_down": w_down.astype(BF16),
        "expand": expand,
    }


def kernel(x_prompt, x_sample, norm_mix, norm_ffn, norm_final, a_wqkv, a_wo, a_rpb, b_wqkv, b_wo, b_sink,
           w_router, w_gate, w_up, w_down):
    p = _prepare(norm_mix, norm_ffn, norm_final, a_wqkv, a_wo, a_rpb, b_wqkv, b_wo, b_sink,
                 w_router, w_gate, w_up, w_down)
    return (_trunk(x_prompt, p), _trunk(x_sample, p))
```

```python
import functools

import numpy as np
import jax
import jax.numpy as jnp
from jax import lax
from jax.experimental import pallas as pl
from jax.experimental.pallas import tpu as pltpu

F32 = jnp.float32
BF16 = jnp.bfloat16
I32 = jnp.int32

D_MODEL = 1024
HEAD_DIM = 64
N_HEADS = 16
LANES = 128
SUBLANES = 8
N_PLANES = D_MODEL // LANES
COL_BLOCKS = D_MODEL // LANES
GRID_W = 64
NA_KH = 8
NA_KW = 16
NA_UNIT = 4 * GRID_W
NA_UNITS_PER_STEP = 8
SW_KV_HEADS = 4
SW_WINDOW = 128
SW_UNIT = 128
SW_UNITS_PER_STEP = 8
N_EXPERTS = 16
EXPERT_FF = 2048
FF_CHUNK = 512
ROUTE_TILE = 256
POSITION_TILES_PER_STEP = 8
SLAB = 48
RMS_EPS = 1e-6
NEG = -1e30
LOG2E = 1.4426950408889634
ROW_TILE = 512
DENSE_ROW_TILE = 1024
VMEM_LIMIT = 56 * 1024 * 1024


def _cparams(*sem):
    return pltpu.CompilerParams(dimension_semantics=sem, vmem_limit_bytes=VMEM_LIMIT)


def _rmsnorm(x, g):
    return x * lax.rsqrt(jnp.mean(x * x, axis=-1, keepdims=True) + RMS_EPS) * g


def _qkv_kernel(x_ref, g_ref, w_ref, qk_ref, vt_ref):
    y = _rmsnorm(x_ref[...], g_ref[...]).astype(BF16)
    r = jnp.dot(y, w_ref[...], preferred_element_type=F32)
    n_qk = qk_ref.shape[0]
    for p in range(n_qk):
        qk_ref[p] = r[:, p * LANES:(p + 1) * LANES].astype(BF16)
    for p in range(vt_ref.shape[0]):
        vt_ref[p] = r[:, (n_qk + p) * LANES:(n_qk + p + 1) * LANES].T.astype(BF16)


def _qkv_call(x, g, w, n_v_planes):
    t, d = x.shape
    n_qk = w.shape[1] // LANES - n_v_planes
    tm = min(DENSE_ROW_TILE, t)
    return pl.pallas_call(
        _qkv_kernel,
        name="qkv",
        grid=(t // tm,),
        in_specs=[pl.BlockSpec((tm, d), lambda i: (i, 0)),
                  pl.BlockSpec((1, d), lambda i: (0, 0)),
                  pl.BlockSpec((d, w.shape[1]), lambda i: (0, 0))],
        out_specs=[pl.BlockSpec((n_qk, tm, LANES), lambda i: (0, i, 0)),
                   pl.BlockSpec((n_v_planes, LANES, tm), lambda i: (0, 0, i))],
        out_shape=[jax.ShapeDtypeStruct((n_qk, t, LANES), BF16),
                   jax.ShapeDtypeStruct((n_v_planes, LANES, t), BF16)],
        compiler_params=_cparams("parallel"),
    )(x, g, w)


def _edge_case(u, n_units):
    i = pl.program_id(2)
    last = pl.num_programs(2) - 1
    if n_units == 1:
        return jnp.where(i == 0, 1, jnp.where(i == last, 2, 0))
    if u == 0:
        return jnp.where(i == 0, 1, 0)
    if u == n_units - 1:
        return jnp.where(i == last, 2, 0)
    return 0


def _attn_kernel(*refs, unit, n_units, has_sink):
    q_ref, kp_ref, kc_ref, kn_ref, vp_ref, vc_ref, vn_ref, bias_ref = refs[:8]
    sink_ref = refs[8] if has_sink else None
    o_ref = refs[-1]
    n_planes = q_ref.shape[0]
    kcat = jnp.concatenate([kp_ref[0], kc_ref[0], kn_ref[0]], axis=0)
    vcat = jnp.concatenate([vp_ref[0], vc_ref[0], vn_ref[0]], axis=1)
    vcat = jnp.concatenate([vcat, jnp.ones((SUBLANES, vcat.shape[1]), BF16)], axis=0)
    lo_half = lax.broadcasted_iota(I32, (unit, LANES), 1) < HEAD_DIM
    zero = jnp.zeros((unit, LANES), BF16)
    def scores(u):
        kw = kcat[u * unit:(u + 3) * unit]
        qs = [q_ref[pi, u * unit:(u + 1) * unit, :] for pi in range(n_planes)]
        q_stack = jnp.concatenate(
            [jnp.where(lo_half if hh == 0 else jnp.logical_not(lo_half), q, zero) for q in qs for hh in range(2)],
            axis=0)
        return lax.dot_general(kw, q_stack, (((1,), (1,)), ((), ())), preferred_element_type=F32)

    st_next = scores(0)
    for u in range(n_units):
        case = _edge_case(u, n_units)
        vw = vcat[:, u * unit:(u + 3) * unit]
        st = st_next + bias_ref[0, case]
        if u + 1 < n_units:
            st_next = scores(u + 1)
        m = jnp.max(st, axis=0, keepdims=True)
        if has_sink:
            sink = sink_ref[0, 0:1, :]
            m = jnp.maximum(m, sink)
        e = jnp.exp2(st - m)
        ot = jnp.dot(vw, e.astype(BF16), preferred_element_type=F32)
        l = ot[LANES:LANES + 1, :]
        if has_sink:
            l = l + jnp.exp2(sink - m)
        ot = ot[0:LANES, :] / l
        for pi in range(n_planes):
            both = jnp.concatenate(
                [ot[0:HEAD_DIM, (2 * pi) * unit:(2 * pi + 1) * unit],
                 ot[HEAD_DIM:LANES, (2 * pi + 1) * unit:(2 * pi + 2) * unit]], axis=0)
            o_ref[pi, u * unit:(u + 1) * unit, :] = both.T.astype(BF16)


def _attn_call(name, qk, vt, bias, sink, batch, seq, unit, n_units, n_groups):
    t = batch * seq
    units = seq // unit
    n_units = min(n_units, units)
    steps = units // n_units
    assert seq % (unit * n_units) == 0 and units >= 4
    ppg = N_PLANES // n_groups
    qb = unit * n_units
    width = 2 * ppg * unit

    def cur(g, b, i):
        return b * steps + i

    def prev(g, b, i):
        return b * units + jnp.maximum(n_units * i - 1, 0)

    def nxt(g, b, i):
        return b * units + jnp.minimum(n_units * i + n_units, units - 1)

    in_specs = [pl.BlockSpec((ppg, qb, LANES), lambda g, b, i: (g, cur(g, b, i), 0)),
                pl.BlockSpec((1, unit, LANES), lambda g, b, i: (N_PLANES + g, prev(g, b, i), 0)),
                pl.BlockSpec((1, qb, LANES), lambda g, b, i: (N_PLANES + g, cur(g, b, i), 0)),
                pl.BlockSpec((1, unit, LANES), lambda g, b, i: (N_PLANES + g, nxt(g, b, i), 0)),
                pl.BlockSpec((1, LANES, unit), lambda g, b, i: (g, 0, prev(g, b, i))),
                pl.BlockSpec((1, LANES, qb), lambda g, b, i: (g, 0, cur(g, b, i))),
                pl.BlockSpec((1, LANES, unit), lambda g, b, i: (g, 0, nxt(g, b, i))),
                pl.BlockSpec((1, 3, 3 * unit, width), lambda g, b, i: (g, 0, 0, 0))]
    args = [qk, qk, qk, qk, vt, vt, vt, bias]
    if sink is not None:
        in_specs.append(pl.BlockSpec((1, SUBLANES, width), lambda g, b, i: (g, 0, 0)))
        args.append(sink)
    return pl.pallas_call(
        functools.partial(_attn_kernel, unit=unit, n_units=n_units, has_sink=sink is not None),
        name=name,
        grid=(n_groups, batch, steps),
        in_specs=in_specs,
        out_specs=pl.BlockSpec((ppg, qb, LANES), lambda g, b, i: (g, cur(g, b, i), 0)),
        out_shape=jax.ShapeDtypeStruct((N_PLANES, t, LANES), BF16),
        compiler_params=_cparams("arbitrary", "arbitrary", "arbitrary"),
    )(*args)


def _na_bias_table(rpb):
    a = np.arange(NA_UNIT) // GRID_W
    j = np.arange(3 * NA_UNIT) // GRID_W
    row_ok = np.stack([
        (j[None, :] >= a[:, None]) & (j[None, :] < a[:, None] + NA_KH),
        np.broadcast_to((j[None, :] >= 4) & (j[None, :] < 4 + NA_KH), (NA_UNIT, 3 * NA_UNIT)),
        np.broadcast_to(j[None, :] < NA_KH, (NA_UNIT, 3 * NA_UNIT)),
    ])
    w = np.arange(GRID_W)
    dc = np.clip(w[None, :] - w[:, None], -(NA_KW - 1), NA_KW - 1) + NA_KW - 1
    c0 = np.clip(w - NA_KW // 2, 0, GRID_W - NA_KW)
    col_ok = (w[None, :] >= c0[:, None]) & (w[None, :] < c0[:, None] + NA_KW)
    cols = jnp.take(rpb.astype(F32), jnp.asarray(dc.reshape(-1)), axis=2)
    cols = cols.reshape(N_HEADS, 2 * NA_KH - 1, GRID_W, GRID_W)
    cols = jnp.where(jnp.asarray(col_ok), cols, NEG)
    rows = jnp.stack([cols[:, 3 - ai:15 - ai] for ai in range(4)], axis=1)
    tbl = rows.transpose(0, 2, 4, 1, 3).reshape(N_HEADS, 3 * NA_UNIT, NA_UNIT)
    ok = jnp.asarray(row_ok.transpose(0, 2, 1))
    full = jnp.where(ok[:, None], tbl[None], NEG)
    full = full.reshape(3, N_PLANES, 2, 3 * NA_UNIT, NA_UNIT).transpose(1, 0, 3, 2, 4)
    return full.reshape(N_PLANES, 3, 3 * NA_UNIT, 2 * NA_UNIT)


SW_HEAD_ORDER = np.array([8 * m + 4 * hh + i for m in range(2) for i in range(4) for hh in range(2)])


def _sw_bias_table():
    slopes = np.asarray(2.0 ** (-8.0 * np.arange(1, N_HEADS + 1) / N_HEADS), dtype=np.float32)
    ks = np.arange(3 * SW_UNIT) - SW_WINDOW
    dist = ks[None, :] - np.arange(SW_UNIT)[:, None]
    win_ok = np.abs(dist) <= SW_WINDOW
    pos_ok = np.stack([np.ones(3 * SW_UNIT, bool), ks >= 0, ks < SW_UNIT])
    ok = win_ok[None] & pos_ok[:, None, :]
    alibi = -slopes[SW_HEAD_ORDER][:, None, None] * np.abs(dist).astype(np.float32)[None]
    alibi = alibi * np.float32(LOG2E)
    full = np.where(ok[:, None], alibi[None], np.float32(NEG)).astype(np.float32)
    full = full.reshape(3, 2, 8, SW_UNIT, 3 * SW_UNIT).transpose(1, 0, 4, 2, 3)
    return jnp.asarray(full.reshape(2, 3, 3 * SW_UNIT, 8 * SW_UNIT))


def _post_kernel(o_ref, x_ref, wo_ref, g_ref, wrh_ref, wrhl_ref, x1_ref, h_ref, aff_ref, afft_ref):
    oc = jnp.concatenate([o_ref[p] for p in range(N_PLANES)], axis=1)
    x1 = x_ref[...] + jnp.dot(oc, wo_ref[...], preferred_element_type=F32)
    x1_ref[...] = x1
    h = _rmsnorm(x1, g_ref[...])
    hb = h.astype(BF16)
    h_ref[...] = hb
    hl = (h - hb.astype(F32)).astype(BF16)
    nt = (((1,), (1,)), ((), ()))
    by_hb = lax.dot_general(wrhl_ref[...], hb, nt, preferred_element_type=F32)
    by_hl = lax.dot_general(wrh_ref[...], hl, nt, preferred_element_type=F32)
    logits = by_hb[0:N_EXPERTS] + (by_hb[N_EXPERTS:2 * N_EXPERTS] + by_hl)
    ex = jnp.exp(logits - jnp.max(logits, axis=0, keepdims=True))
    afft = ex / jnp.sum(ex, axis=0, keepdims=True)
    afft_ref[...] = afft
    pad = jnp.zeros((LANES - N_EXPERTS, afft.shape[1]), F32)
    aff_ref[...] = jnp.concatenate([afft, pad], axis=0).T


def _post_call(o, x, wo, g, wrh, wrhl):
    t, d = x.shape
    tm = min(DENSE_ROW_TILE, t)
    return pl.pallas_call(
        _post_kernel,
        name="post",
        grid=(t // tm,),
        in_specs=[pl.BlockSpec((N_PLANES, tm, LANES), lambda i: (0, i, 0)),
                  pl.BlockSpec((tm, d), lambda i: (i, 0)),
                  pl.BlockSpec((d, d), lambda i: (0, 0)),
                  pl.BlockSpec((1, d), lambda i: (0, 0)),
                  pl.BlockSpec((N_EXPERTS, d), lambda i: (0, 0)),
                  pl.BlockSpec((2 * N_EXPERTS, d), lambda i: (0, 0))],
        out_specs=[pl.BlockSpec((tm, d), lambda i: (i, 0)),
                   pl.BlockSpec((tm, d), lambda i: (i, 0)),
                   pl.BlockSpec((tm, LANES), lambda i: (i, 0)),
                   pl.BlockSpec((N_EXPERTS, tm), lambda i: (0, i))],
        out_shape=[jax.ShapeDtypeStruct((t, d), F32),
                   jax.ShapeDtypeStruct((t, d), BF16),
                   jax.ShapeDtypeStruct((t, LANES), F32),
                   jax.ShapeDtypeStruct((N_EXPERTS, t), F32)],
        compiler_params=_cparams("parallel"),
    )(o, x, wo, g, wrh, wrhl)


def _threshold_kernel(aff_ref, thr_ref, need_ref, *, cap):
    bits = lax.bitcast_convert_type(aff_ref[...], I32)

    def body(k, cur):
        cand = cur | jnp.left_shift(jnp.int32(1), 30 - k)
        cnt = jnp.sum(jnp.where(bits >= cand, 1, 0), axis=1, keepdims=True)
        return jnp.where(cnt >= cap, cand, cur)

    cur = lax.fori_loop(0, 31, body, jnp.zeros((N_EXPERTS, 1), I32))
    above = jnp.sum(jnp.where(bits > cur, 1, 0), axis=1, keepdims=True)
    thr_ref[...] = jnp.broadcast_to(cur, thr_ref.shape)
    need_ref[...] = jnp.broadcast_to(cap - above, need_ref.shape)


def _threshold_call(afft, cap):
    e, t = afft.shape
    return pl.pallas_call(
        functools.partial(_threshold_kernel, cap=cap),
        name="threshold",
        grid=(1,),
        in_specs=[pl.BlockSpec((e, t), lambda i: (0, 0))],
        out_specs=[pl.BlockSpec((e, LANES), lambda i: (0, 0)),
                   pl.BlockSpec((e, LANES), lambda i: (0, 0))],
        out_shape=[jax.ShapeDtypeStruct((e, LANES), I32),
                   jax.ShapeDtypeStruct((e, LANES), I32)],
        compiler_params=_cparams("arbitrary"),
    )(afft)


def _positions_kernel(aff_ref, thr_ref, need_ref, lpos_ref, lpost_ref, starts_ref, npass_ref, base_sc, eqb_sc):
    @pl.when(pl.program_id(0) == 0)
    def _():
        base_sc[...] = jnp.zeros_like(base_sc)
        eqb_sc[...] = jnp.zeros_like(eqb_sc)

    thr = thr_ref[:, 0:1]
    need = need_ref[:, 0:1].astype(F32)
    r = lax.broadcasted_iota(I32, (ROUTE_TILE, ROUTE_TILE), 0)
    c = lax.broadcasted_iota(I32, (ROUTE_TILE, ROUTE_TILE), 1)
    before = jnp.where(r < c, 1.0, 0.0).astype(BF16)
    pad = jnp.full((LANES - N_EXPERTS, ROUTE_TILE), -1.0, F32)
    base = base_sc[...]
    eqb = eqb_sc[...]
    for k in range(starts_ref.shape[0]):
        cols = slice(k * ROUTE_TILE, (k + 1) * ROUTE_TILE)
        bits = lax.bitcast_convert_type(aff_ref[:, cols], I32)
        eq = bits == thr
        eqf = jnp.where(eq, 1.0, 0.0)
        eq_rank = eqb[:, 0:1] + jnp.dot(eqf.astype(BF16), before, preferred_element_type=F32)
        sel = (bits > thr) | (eq & (eq_rank < need))
        self_ = jnp.where(sel, 1.0, 0.0)
        rank = jnp.dot(self_.astype(BF16), before, preferred_element_type=F32)
        cnt = jnp.sum(self_, axis=1, keepdims=True)
        lpos = jnp.where(sel, rank, -1.0)
        lpos_ref[:, cols] = lpos.astype(I32)
        lpost_ref[cols, :] = jnp.concatenate([lpos, pad], axis=0).T
        starts_ref[k] = base.astype(I32)
        most = jnp.max(cnt, axis=0, keepdims=True)
        passes = sum(jnp.where(most > float(first), 1.0, 0.0) for first in range(0, ROUTE_TILE, SLAB))
        npass_ref[k] = jnp.broadcast_to(passes, npass_ref.shape[1:]).astype(I32)
        base = base + cnt
        eqb = eqb + jnp.sum(eqf, axis=1, keepdims=True)
    base_sc[...] = base
    eqb_sc[...] = eqb


def _positions_call(afft, thr, need):
    e, t = afft.shape
    n_tiles = t // ROUTE_TILE
    per_step = min(POSITION_TILES_PER_STEP, n_tiles)
    assert n_tiles % per_step == 0
    span = per_step * ROUTE_TILE
    return pl.pallas_call(
        _positions_kernel,
        name="positions",
        grid=(n_tiles // per_step,),
        in_specs=[pl.BlockSpec((e, span), lambda i: (0, i)),
                  pl.BlockSpec((e, LANES), lambda i: (0, 0)),
                  pl.BlockSpec((e, LANES), lambda i: (0, 0))],
        out_specs=[pl.BlockSpec((e, span), lambda i: (0, i)),
                   pl.BlockSpec((span, LANES), lambda i: (i, 0)),
                   pl.BlockSpec((per_step, e, LANES), lambda i: (i, 0, 0)),
                   pl.BlockSpec((per_step, SUBLANES, LANES), lambda i: (i, 0, 0))],
        out_shape=[jax.ShapeDtypeStruct((e, t), I32),
                   jax.ShapeDtypeStruct((t, LANES), F32),
                   jax.ShapeDtypeStruct((n_tiles, e, LANES), I32),
                   jax.ShapeDtypeStruct((n_tiles, SUBLANES, LANES), I32)],
        scratch_shapes=[pltpu.VMEM((e, LANES), F32), pltpu.VMEM((e, LANES), F32)],
        compiler_params=_cparams("arbitrary"),
    )(afft, thr, need)


SLAB_ROWS = SLAB * COL_BLOCKS


def _dispatch_kernel(starts_sm, npass_sm, h_ref, lpos_ref, xe_ref, slab_sc, sem, state_sm, *, cap):
    i = pl.program_id(0)
    sub = lax.broadcasted_iota(I32, (SLAB, ROUTE_TILE), 0)

    def slab_copy(buf, e, dst):
        return pltpu.make_async_copy(
            slab_sc.at[buf, pl.ds(e * SLAB_ROWS, SLAB_ROWS)],
            xe_ref.at[e, pl.ds(pl.multiple_of(dst * COL_BLOCKS, COL_BLOCKS), SLAB_ROWS)],
            sem.at[buf, e])

    def wait_all(buf):
        for e in range(N_EXPERTS):
            slab_copy(buf, e, 0).wait()

    @pl.when(i == 0)
    def _():
        slab_sc[0, pl.ds(0, SLAB_ROWS), :] = jnp.zeros((SLAB_ROWS, LANES), F32)
        pads = [pltpu.make_async_copy(slab_sc.at[0, pl.ds(0, SLAB_ROWS)],
                                      xe_ref.at[e, pl.ds(cap * COL_BLOCKS, SLAB_ROWS)], sem.at[0, e])
                for e in range(N_EXPERTS)]
        for cp in pads:
            cp.start()
        for cp in pads:
            cp.wait()
        state_sm[0] = 0
        state_sm[1] = 0

    def one_pass(s, carry):
        buf = state_sm[1]
        rows = []
        for e in range(N_EXPERTS):
            lp = lpos_ref[e:e + 1, :] - SLAB * s
            rows.append(jnp.where(lp == sub, 1.0, 0.0).astype(BF16))
        onehot = jnp.concatenate(rows, axis=0)
        res = jnp.dot(onehot, h_ref[...], preferred_element_type=F32)
        for cb in range(COL_BLOCKS):
            slab_sc[buf, pl.ds(cb, N_EXPERTS * SLAB, stride=COL_BLOCKS), :] = res[:, cb * LANES:(cb + 1) * LANES]

        @pl.when(state_sm[0] == 1)
        def _():
            wait_all(1 - buf)

        for e in range(N_EXPERTS):
            dst = jnp.minimum(starts_sm[i * N_EXPERTS + e] + SLAB * s, cap)
            slab_copy(buf, e, dst).start()
        state_sm[0] = 1
        state_sm[1] = 1 - buf
        return carry

    lax.fori_loop(0, npass_sm[i], one_pass, 0)

    @pl.when((i == pl.num_programs(0) - 1) & (state_sm[0] == 1))
    def _():
        wait_all(1 - state_sm[1])


def _dispatch_call(starts, npass, h, lpos, cap):
    t, d = h.shape
    n_tiles = t // ROUTE_TILE
    grid_spec = pltpu.PrefetchScalarGridSpec(
        num_scalar_prefetch=2,
        grid=(n_tiles,),
        in_specs=[pl.BlockSpec((ROUTE_TILE, d), lambda i, *_: (i, 0)),
                  pl.BlockSpec((N_EXPERTS, ROUTE_TILE), lambda i, *_: (0, i))],
        out_specs=pl.BlockSpec(memory_space=pl.ANY),
        scratch_shapes=[pltpu.VMEM((2, N_EXPERTS * SLAB_ROWS, LANES), F32),
                        pltpu.SemaphoreType.DMA((2, N_EXPERTS)),
                        pltpu.SMEM((2,), I32)],
    )
    return pl.pallas_call(
        functools.partial(_dispatch_kernel, cap=cap),
        name="dispatch",
        grid_spec=grid_spec,
        out_shape=jax.ShapeDtypeStruct((N_EXPERTS, (cap + SLAB) * COL_BLOCKS, LANES), F32),
        compiler_params=_cparams("arbitrary"),
    )(starts, npass, h, lpos)


def _expert_kernel(x_ref, wg_ref, wu_ref, wd_ref, o_ref):
    tm = x_ref.shape[1] // COL_BLOCKS
    x = jnp.concatenate([x_ref[0, pl.ds(cb, tm, stride=COL_BLOCKS), :] for cb in range(COL_BLOCKS)],
                        axis=1).astype(BF16)
    acc = jnp.zeros((tm, D_MODEL), F32)
    for fc in range(EXPERT_FF // FF_CHUNK):
        fs = slice(fc * FF_CHUNK, (fc + 1) * FF_CHUNK)
        g = jnp.dot(x, wg_ref[0, 0, :, fs], preferred_element_type=F32)
        u = jnp.dot(x, wu_ref[0, 0, :, fs], preferred_element_type=F32)
        hmid = (g * (1.0 / (1.0 + jnp.exp(-g))) * u).astype(BF16)
        acc = acc + jnp.dot(hmid, wd_ref[0, 0, fs, :], preferred_element_type=F32)
    for cb in range(COL_BLOCKS):
        o_ref[0, pl.ds(cb, tm, stride=COL_BLOCKS), :] = acc[:, cb * LANES:(cb + 1) * LANES]


def _expert_call(xe, wg, wu, wd, layer, cap):
    tm = min(ROW_TILE, cap)
    d, f = wg.shape[2], wg.shape[3]
    return pl.pallas_call(
        _expert_kernel,
        name="expert",
        grid=(N_EXPERTS, cap // tm),
        in_specs=[pl.BlockSpec((1, tm * COL_BLOCKS, LANES), lambda e, j: (e, j, 0)),
                  pl.BlockSpec((1, 1, d, f), lambda e, j: (layer, e, 0, 0)),
                  pl.BlockSpec((1, 1, d, f), lambda e, j: (layer, e, 0, 0)),
                  pl.BlockSpec((1, 1, f, d), lambda e, j: (layer, e, 0, 0))],
        out_specs=pl.BlockSpec((1, tm * COL_BLOCKS, LANES), lambda e, j: (e, j, 0)),
        out_shape=jax.ShapeDtypeStruct((N_EXPERTS, cap * COL_BLOCKS, LANES), F32),
        compiler_params=_cparams("parallel", "parallel"),
    )(xe, wg, wu, wd)


def _combine_kernel(starts_sm, npass_sm, x_ref, aff_ref, lpost_ref, expand_ref, g_ref, ye_ref, o_ref,
                    slab_sc, sem, *, cap, final):
    i = pl.program_id(0)
    lane = lax.broadcasted_iota(I32, (1, LANES), 1)
    slot_in_slab = (lax.broadcasted_iota(I32, (1, N_EXPERTS * SLAB), 1) % SLAB).astype(F32)
    aff = aff_ref[...]
    ghi = aff.astype(BF16)
    glo = (aff - ghi.astype(F32)).astype(BF16)
    expand = expand_ref[...]
    ghi_x = jnp.dot(ghi, expand, preferred_element_type=F32)
    glo_x = jnp.dot(glo, expand, preferred_element_type=F32)
    lpost = lpost_ref[...]
    buf = i % 2

    def slab_src(tile, s, e):
        want = starts_sm[tile * N_EXPERTS + e] + SLAB * s
        return want, jnp.minimum(want, cap - SLAB)

    def slab_copy(tile, s, to, e):
        src = slab_src(tile, s, e)[1]
        return pltpu.make_async_copy(
            ye_ref.at[e, pl.ds(pl.multiple_of(src * COL_BLOCKS, COL_BLOCKS), SLAB_ROWS)],
            slab_sc.at[to, pl.ds(e * SLAB_ROWS, SLAB_ROWS)],
            sem.at[to, e])

    def fetch(tile, s, to):
        for e in range(N_EXPERTS):
            slab_copy(tile, s, to, e).start()

    def add_pass(s, acc):
        shift = jnp.zeros((1, LANES), F32)
        for e in range(N_EXPERTS):
            want, src = slab_src(i, s, e)
            shift = jnp.where(lane == e, (want - src).astype(F32), shift)
        lo = jnp.asarray(SLAB * s, F32)
        owned = (lpost >= lo) & (lpost < lo + SLAB)
        row = jnp.where(owned, lpost - lo + shift, -1.0).astype(BF16)
        row_x = jnp.dot(row, expand, preferred_element_type=F32)
        match = row_x == slot_in_slab
        whi = jnp.where(match, ghi_x, 0.0).astype(BF16)
        wlo = jnp.where(match, glo_x, 0.0).astype(BF16)
        for e in range(N_EXPERTS):
            slab_copy(i, s, buf, e).wait()
        rows = jnp.concatenate(
            [slab_sc[buf, pl.ds(cb, N_EXPERTS * SLAB, stride=COL_BLOCKS), :] for cb in range(COL_BLOCKS)],
            axis=1).astype(BF16)
        return acc + (jnp.dot(whi, rows, preferred_element_type=F32)
                      + jnp.dot(wlo, rows, preferred_element_type=F32))

    @pl.when(i == 0)
    def _():
        fetch(0, 0, 0)

    @pl.when(i + 1 < pl.num_programs(0))
    def _():
        fetch(i + 1, 0, 1 - buf)

    y = add_pass(0, jnp.zeros(x_ref.shape, F32))

    def extra_pass(s, acc):
        fetch(i, s, buf)
        return add_pass(s, acc)

    y = lax.fori_loop(1, npass_sm[i], extra_pass, y)
    out = x_ref[...] + y
    if final:
        out = _rmsnorm(out, g_ref[...])
    o_ref[...] = out


def _combine_call(starts, npass, x1, aff, lpost, expand, g, ye, cap, final):
    t, d = x1.shape
    n_tiles = t // ROUTE_TILE
    grid_spec = pltpu.PrefetchScalarGridSpec(
        num_scalar_prefetch=2,
        grid=(n_tiles,),
        in_specs=[pl.BlockSpec((ROUTE_TILE, d), lambda i, *_: (i, 0)),
                  pl.BlockSpec((ROUTE_TILE, LANES), lambda i, *_: (i, 0)),
                  pl.BlockSpec((ROUTE_TILE, LANES), lambda i, *_: (i, 0)),
                  pl.BlockSpec((LANES, N_EXPERTS * SLAB), lambda i, *_: (0, 0)),
                  pl.BlockSpec((1, d), lambda i, *_: (0, 0)),
                  pl.BlockSpec(memory_space=pl.ANY)],
        out_specs=pl.BlockSpec((ROUTE_TILE, d), lambda i, *_: (i, 0)),
        scratch_shapes=[pltpu.VMEM((2, N_EXPERTS * SLAB_ROWS, LANES), F32),
                        pltpu.SemaphoreType.DMA((2, N_EXPERTS))],
    )
    return pl.pallas_call(
        functools.partial(_combine_kernel, cap=cap, final=final),
        name="combine",
        grid_spec=grid_spec,
        out_shape=jax.ShapeDtypeStruct((t, d), F32),
        compiler_params=_cparams("arbitrary"),
    )(starts, npass, x1, aff, lpost, expand, g, ye)


def _moe(x1, h, aff, afft, wg, wu, wd, layer, expand, g_final, final):
    t = x1.shape[0]
    cap = 2 * t // N_EXPERTS
    assert t % ROUTE_TILE == 0 and cap >= SLAB
    thr, need = _threshold_call(afft, cap)
    lpos, lpost, starts3, npass3 = _positions_call(afft, thr, need)
    starts = starts3[:, :, 0].reshape(-1)
    npass = npass3[:, 0, 0]
    xe = _dispatch_call(starts, npass, h, lpos, cap)
    ye = _expert_call(xe, wg, wu, wd, layer, cap)
    return _combine_call(starts, npass, x1, aff, lpost, expand, g_final, ye, cap, final)


def _trunk(x, p):
    batch, seq, d = x.shape
    x = x.reshape(batch * seq, d)
    depth = p["norm_mix"].shape[0]
    for i in range(depth):
        j = i // 2
        if i % 2 == 0:
            qk, vt = _qkv_call(x, p["norm_mix"][i:i + 1], p["a_wqkv"][j], N_PLANES)
            o = _attn_call("na_attn", qk, vt, p["a_bias"][j], None, batch, seq,
                           NA_UNIT, NA_UNITS_PER_STEP, N_PLANES)
            wo = p["a_wo"][j]
        else:
            qk, vt = _qkv_call(x, p["norm_mix"][i:i + 1], p["b_wqkv"][j], 2)
            o = _attn_call("sw_attn", qk, vt, p["b_bias"], p["b_sink"][j], batch, seq,
                           SW_UNIT, SW_UNITS_PER_STEP, 2)
            wo = p["b_wo"][j]
        x1, h, aff, afft = _post_call(o, x, wo, p["norm_ffn"][i:i + 1], p["wr_hi"][i], p["wr_hl"][i])
        x = _moe(x1, h, aff, afft, p["w_gate"], p["w_up"], p["w_down"], i,
                 p["expand"], p["norm_final"], i == depth - 1)
    return x.reshape(batch, seq, d)


def _prepare(norm_mix, norm_ffn, norm_final, a_wqkv, a_wo, a_rpb, b_wqkv, b_wo, b_sink,
             w_router, w_gate, w_up, w_down):
    d = D_MODEL
    scale = HEAD_DIM ** -0.5 * LOG2E
    a_w = jnp.concatenate([a_wqkv[:, :, :d] * scale, a_wqkv[:, :, d:]], axis=2).astype(BF16)
    order = SW_HEAD_ORDER
    nb = b_wqkv.shape[0]
    bq = (b_wqkv[:, :, :d] * scale).reshape(nb, d, N_HEADS, HEAD_DIM)[:, :, order].reshape(nb, d, d)
    b_w = jnp.concatenate([bq, b_wqkv[:, :, d:]], axis=2).astype(BF16)
    b_wo_p = b_wo.reshape(nb, N_HEADS, HEAD_DIM, d)[:, order].reshape(nb, d, d).astype(BF16)
    sink = (b_sink.astype(F32) * LOG2E)[:, order].reshape(nb, 2, 1, 8, 1)
    sink = jnp.broadcast_to(sink, (nb, 2, SUBLANES, 8, SW_UNIT)).reshape(nb, 2, SUBLANES, 8 * SW_UNIT)
    wr = w_router.astype(F32).transpose(0, 2, 1)
    wr_hi = wr.astype(BF16)
    wr_lo = (wr - wr_hi.astype(F32)).astype(BF16)
    wr_hl = jnp.concatenate([wr_hi, wr_lo], axis=1)
    lane_expert = np.arange(N_EXPERTS * SLAB) // SLAB
    expand = jnp.asarray(np.arange(LANES)[:, None] == lane_expert[None, :], dtype=BF16)
    return {
        "norm_mix": norm_mix.astype(F32), "norm_ffn": norm_ffn.astype(F32),
        "norm_final": norm_final.astype(F32).reshape(1, d),
        "a_wqkv": a_w, "a_wo": a_wo.astype(BF16),
        "a_bias": jnp.stack([_na_bias_table(a_rpb[j] * LOG2E) for j in range(a_rpb.shape[0])]),
        "b_wqkv": b_w, "b_wo": b_wo_p, "b_bias": _sw_bias_table(), "b_sink": sink,
        "wr_hi": wr_hi, "wr_hl": wr_hl,
        "w_gate": w_gate.astype(BF16), "w_up": w_up.astype(BF16), "w_down": w_down.astype(BF16),
        "expand": expand,
    }


def kernel(x_prompt, x_sample, norm_mix, norm_ffn, norm_final, a_wqkv, a_wo, a_rpb, b_wqkv, b_wo, b_sink,
           w_router, w_gate, w_up, w_down):
    p = _prepare(norm_mix, norm_ffn, norm_final, a_wqkv, a_wo, a_rpb, b_wqkv, b_wo, b_sink,
                 w_router, w_gate, w_up, w_down)
    return (_trunk(x_prompt, p), _trunk(x_sample, p))
```

```python
import functools

import numpy as np
import jax
import jax.numpy as jnp
from jax import lax
from jax.experimental import pallas as pl
from jax.experimental.pallas import tpu as pltpu

F32 = jnp.float32
BF16 = jnp.bfloat16
I32 = jnp.int32

D_MODEL = 1024
HEAD_DIM = 64
N_HEADS = 16
LANES = 128
SUBLANES = 8
N_PLANES = D_MODEL // LANES
COL_BLOCKS = D_MODEL // LANES
GRID_W = 64
NA_KH = 8
NA_KW = 16
NA_UNIT = 4 * GRID_W
NA_UNITS_PER_STEP = 8
SW_KV_HEADS = 4
SW_WINDOW = 128
SW_UNIT = 128
SW_UNITS_PER_STEP = 8
N_EXPERTS = 16
EXPERT_FF = 2048
FF_CHUNK = 512
ROUTE_TILE = 256
POSITION_TILES_PER_STEP = 8
SLAB = 56
RMS_EPS = 1e-6
NEG = -1e30
LOG2E = 1.4426950408889634
ROW_TILE = 512
DENSE_ROW_TILE = 1024
VMEM_LIMIT = 56 * 1024 * 1024


def _cparams(*sem):
    return pltpu.CompilerParams(dimension_semantics=sem, vmem_limit_bytes=VMEM_LIMIT)


def _rmsnorm(x, g):
    return x * lax.rsqrt(jnp.mean(x * x, axis=-1, keepdims=True) + RMS_EPS) * g


def _qkv_kernel(x_ref, g_ref, w_ref, qk_ref, vt_ref):
    y = _rmsnorm(x_ref[...], g_ref[...]).astype(BF16)
    r = jnp.dot(y, w_ref[...], preferred_element_type=F32)
    n_qk = qk_ref.shape[0]
    for p in range(n_qk):
        qk_ref[p] = r[:, p * LANES:(p + 1) * LANES].astype(BF16)
    for p in range(vt_ref.shape[0]):
        vt_ref[p] = r[:, (n_qk + p) * LANES:(n_qk + p + 1) * LANES].T.astype(BF16)


def _qkv_call(x, g, w, n_v_planes):
    t, d = x.shape
    n_qk = w.shape[1] // LANES - n_v_planes
    tm = min(DENSE_ROW_TILE, t)
    return pl.pallas_call(
        _qkv_kernel,
        name="qkv",
        grid=(t // tm,),
        in_specs=[pl.BlockSpec((tm, d), lambda i: (i, 0)),
                  pl.BlockSpec((1, d), lambda i: (0, 0)),
                  pl.BlockSpec((d, w.shape[1]), lambda i: (0, 0))],
        out_specs=[pl.BlockSpec((n_qk, tm, LANES), lambda i: (0, i, 0)),
                   pl.BlockSpec((n_v_planes, LANES, tm), lambda i: (0, 0, i))],
        out_shape=[jax.ShapeDtypeStruct((n_qk, t, LANES), BF16),
                   jax.ShapeDtypeStruct((n_v_planes, LANES, t), BF16)],
        compiler_params=_cparams("parallel"),
    )(x, g, w)


def _edge_case(u, n_units):
    i = pl.program_id(2)
    last = pl.num_programs(2) - 1
    if n_units == 1:
        return jnp.where(i == 0, 1, jnp.where(i == last, 2, 0))
    if u == 0:
        return jnp.where(i == 0, 1, 0)
    if u == n_units - 1:
        return jnp.where(i == last, 2, 0)
    return 0


def _attn_kernel(*refs, unit, n_units, has_sink):
    q_ref, kp_ref, kc_ref, kn_ref, vp_ref, vc_ref, vn_ref, bias_ref = refs[:8]
    sink_ref = refs[8] if has_sink else None
    o_ref = refs[-1]
    n_planes = q_ref.shape[0]
    kcat = jnp.concatenate([kp_ref[0], kc_ref[0], kn_ref[0]], axis=0)
    vcat = jnp.concatenate([vp_ref[0], vc_ref[0], vn_ref[0]], axis=1)
    vcat = jnp.concatenate([vcat, jnp.ones((SUBLANES, vcat.shape[1]), BF16)], axis=0)
    lo_half = lax.broadcasted_iota(I32, (unit, LANES), 1) < HEAD_DIM
    zero = jnp.zeros((unit, LANES), BF16)
    def scores(u):
        kw = kcat[u * unit:(u + 3) * unit]
        qs = [q_ref[pi, u * unit:(u + 1) * unit, :] for pi in range(n_planes)]
        q_stack = jnp.concatenate(
            [jnp.where(lo_half if hh == 0 else jnp.logical_not(lo_half), q, zero) for q in qs for hh in range(2)],
            axis=0)
        return lax.dot_general(kw, q_stack, (((1,), (1,)), ((), ())), preferred_element_type=F32)

    st_next = scores(0)
    for u in range(n_units):
        case = _edge_case(u, n_units)
        vw = vcat[:, u * unit:(u + 3) * unit]
        st = st_next + bias_ref[0, case]
        if u + 1 < n_units:
            st_next = scores(u + 1)
        m = jnp.max(st, axis=0, keepdims=True)
        if has_sink:
            sink = sink_ref[0, 0:1, :]
            m = jnp.maximum(m, sink)
        e = jnp.exp2(st - m)
        ot = jnp.dot(vw, e.astype(BF16), preferred_element_type=F32)
        l = ot[LANES:LANES + 1, :]
        if has_sink:
            l = l + jnp.exp2(sink - m)
        ot = ot[0:LANES, :] / l
        for pi in range(n_planes):
            both = jnp.concatenate(
                [ot[0:HEAD_DIM, (2 * pi) * unit:(2 * pi + 1) * unit],
                 ot[HEAD_DIM:LANES, (2 * pi + 1) * unit:(2 * pi + 2) * unit]], axis=0)
            o_ref[pi, u * unit:(u + 1) * unit, :] = both.T.astype(BF16)


def _attn_call(name, qk, vt, bias, sink, batch, seq, unit, n_units, n_groups):
    t = batch * seq
    units = seq // unit
    n_units = min(n_units, units)
    steps = units // n_units
    assert seq % (unit * n_units) == 0 and units >= 4
    ppg = N_PLANES // n_groups
    qb = unit * n_units
    width = 2 * ppg * unit

    def cur(g, b, i):
        return b * steps + i

    def prev(g, b, i):
        return b * units + jnp.maximum(n_units * i - 1, 0)

    def nxt(g, b, i):
        return b * units + jnp.minimum(n_units * i + n_units, units - 1)

    in_specs = [pl.BlockSpec((ppg, qb, LANES), lambda g, b, i: (g, cur(g, b, i), 0)),
                pl.BlockSpec((1, unit, LANES), lambda g, b, i: (N_PLANES + g, prev(g, b, i), 0)),
                pl.BlockSpec((1, qb, LANES), lambda g, b, i: (N_PLANES + g, cur(g, b, i), 0)),
                pl.BlockSpec((1, unit, LANES), lambda g, b, i: (N_PLANES + g, nxt(g, b, i), 0)),
                pl.BlockSpec((1, LANES, unit), lambda g, b, i: (g, 0, prev(g, b, i))),
                pl.BlockSpec((1, LANES, qb), lambda g, b, i: (g, 0, cur(g, b, i))),
                pl.BlockSpec((1, LANES, unit), lambda g, b, i: (g, 0, nxt(g, b, i))),
                pl.BlockSpec((1, 3, 3 * unit, width), lambda g, b, i: (g, 0, 0, 0))]
    args = [qk, qk, qk, qk, vt, vt, vt, bias]
    if sink is not None:
        in_specs.append(pl.BlockSpec((1, SUBLANES, width), lambda g, b, i: (g, 0, 0)))
        args.append(sink)
    return pl.pallas_call(
        functools.partial(_attn_kernel, unit=unit, n_units=n_units, has_sink=sink is not None),
        name=name,
        grid=(n_groups, batch, steps),
        in_specs=in_specs,
        out_specs=pl.BlockSpec((ppg, qb, LANES), lambda g, b, i: (g, cur(g, b, i), 0)),
        out_shape=jax.ShapeDtypeStruct((N_PLANES, t, LANES), BF16),
        compiler_params=_cparams("arbitrary", "arbitrary", "arbitrary"),
    )(*args)


def _na_bias_table(rpb):
    a = np.arange(NA_UNIT) // GRID_W
    j = np.arange(3 * NA_UNIT) // GRID_W
    row_ok = np.stack([
        (j[None, :] >= a[:, None]) & (j[None, :] < a[:, None] + NA_KH),
        np.broadcast_to((j[None, :] >= 4) & (j[None, :] < 4 + NA_KH), (NA_UNIT, 3 * NA_UNIT)),
        np.broadcast_to(j[None, :] < NA_KH, (NA_UNIT, 3 * NA_UNIT)),
    ])
    w = np.arange(GRID_W)
    dc = np.clip(w[None, :] - w[:, None], -(NA_KW - 1), NA_KW - 1) + NA_KW - 1
    c0 = np.clip(w - NA_KW // 2, 0, GRID_W - NA_KW)
    col_ok = (w[None, :] >= c0[:, None]) & (w[None, :] < c0[:, None] + NA_KW)
    cols = jnp.take(rpb.astype(F32), jnp.asarray(dc.reshape(-1)), axis=2)
    cols = cols.reshape(N_HEADS, 2 * NA_KH - 1, GRID_W, GRID_W)
    cols = jnp.where(jnp.asarray(col_ok), cols, NEG)
    rows = jnp.stack([cols[:, 3 - ai:15 - ai] for ai in range(4)], axis=1)
    tbl = rows.transpose(0, 2, 4, 1, 3).reshape(N_HEADS, 3 * NA_UNIT, NA_UNIT)
    ok = jnp.asarray(row_ok.transpose(0, 2, 1))
    full = jnp.where(ok[:, None], tbl[None], NEG)
    full = full.reshape(3, N_PLANES, 2, 3 * NA_UNIT, NA_UNIT).transpose(1, 0, 3, 2, 4)
    return full.reshape(N_PLANES, 3, 3 * NA_UNIT, 2 * NA_UNIT)


SW_HEAD_ORDER = np.array([8 * m + 4 * hh + i for m in range(2) for i in range(4) for hh in range(2)])


def _sw_bias_table():
    slopes = np.asarray(2.0 ** (-8.0 * np.arange(1, N_HEADS + 1) / N_HEADS), dtype=np.float32)
    ks = np.arange(3 * SW_UNIT) - SW_WINDOW
    dist = ks[None, :] - np.arange(SW_UNIT)[:, None]
    win_ok = np.abs(dist) <= SW_WINDOW
    pos_ok = np.stack([np.ones(3 * SW_UNIT, bool), ks >= 0, ks < SW_UNIT])
    ok = win_ok[None] & pos_ok[:, None, :]
    alibi = -slopes[SW_HEAD_ORDER][:, None, None] * np.abs(dist).astype(np.float32)[None]
    alibi = alibi * np.float32(LOG2E)
    full = np.where(ok[:, None], alibi[None], np.float32(NEG)).astype(np.float32)
    full = full.reshape(3, 2, 8, SW_UNIT, 3 * SW_UNIT).transpose(1, 0, 4, 2, 3)
    return jnp.asarray(full.reshape(2, 3, 3 * SW_UNIT, 8 * SW_UNIT))


def _post_kernel(o_ref, x_ref, wo_ref, g_ref, wrh_ref, wrhl_ref, x1_ref, h_ref, aff_ref, afft_ref):
    oc = jnp.concatenate([o_ref[p] for p in range(N_PLANES)], axis=1)
    x1 = x_ref[...] + jnp.dot(oc, wo_ref[...], preferred_element_type=F32)
    x1_ref[...] = x1
    h = _rmsnorm(x1, g_ref[...])
    hb = h.astype(BF16)
    h_ref[...] = hb
    hl = (h - hb.astype(F32)).astype(BF16)
    nt = (((1,), (1,)), ((), ()))
    by_hb = lax.dot_general(wrhl_ref[...], hb, nt, preferred_element_type=F32)
    by_hl = lax.dot_general(wrh_ref[...], hl, nt, preferred_element_type=F32)
    logits = by_hb[0:N_EXPERTS] + (by_hb[N_EXPERTS:2 * N_EXPERTS] + by_hl)
    ex = jnp.exp(logits - jnp.max(logits, axis=0, keepdims=True))
    afft = ex / jnp.sum(ex, axis=0, keepdims=True)
    afft_ref[...] = afft
    pad = jnp.zeros((LANES - N_EXPERTS, afft.shape[1]), F32)
    aff_ref[...] = jnp.concatenate([afft, pad], axis=0).T


def _post_call(o, x, wo, g, wrh, wrhl):
    t, d = x.shape
    tm = min(DENSE_ROW_TILE, t)
    return pl.pallas_call(
        _post_kernel,
        name="post",
        grid=(t // tm,),
        in_specs=[pl.BlockSpec((N_PLANES, tm, LANES), lambda i: (0, i, 0)),
                  pl.BlockSpec((tm, d), lambda i: (i, 0)),
                  pl.BlockSpec((d, d), lambda i: (0, 0)),
                  pl.BlockSpec((1, d), lambda i: (0, 0)),
                  pl.BlockSpec((N_EXPERTS, d), lambda i: (0, 0)),
                  pl.BlockSpec((2 * N_EXPERTS, d), lambda i: (0, 0))],
        out_specs=[pl.BlockSpec((tm, d), lambda i: (i, 0)),
                   pl.BlockSpec((tm, d), lambda i: (i, 0)),
                   pl.BlockSpec((tm, LANES), lambda i: (i, 0)),
                   pl.BlockSpec((N_EXPERTS, tm), lambda i: (0, i))],
        out_shape=[jax.ShapeDtypeStruct((t, d), F32),
                   jax.ShapeDtypeStruct((t, d), BF16),
                   jax.ShapeDtypeStruct((t, LANES), F32),
                   jax.ShapeDtypeStruct((N_EXPERTS, t), F32)],
        compiler_params=_cparams("parallel"),
    )(o, x, wo, g, wrh, wrhl)


def _threshold_kernel(aff_ref, thr_ref, need_ref, *, cap):
    bits = lax.bitcast_convert_type(aff_ref[...], I32)

    def body(k, cur):
        cand = cur | jnp.left_shift(jnp.int32(1), 30 - k)
        cnt = jnp.sum(jnp.where(bits >= cand, 1, 0), axis=1, keepdims=True)
        return jnp.where(cnt >= cap, cand, cur)

    cur = lax.fori_loop(0, 31, body, jnp.zeros((N_EXPERTS, 1), I32))
    above = jnp.sum(jnp.where(bits > cur, 1, 0), axis=1, keepdims=True)
    thr_ref[...] = jnp.broadcast_to(cur, thr_ref.shape)
    need_ref[...] = jnp.broadcast_to(cap - above, need_ref.shape)


def _threshold_call(afft, cap):
    e, t = afft.shape
    return pl.pallas_call(
        functools.partial(_threshold_kernel, cap=cap),
        name="threshold",
        grid=(1,),
        in_specs=[pl.BlockSpec((e, t), lambda i: (0, 0))],
        out_specs=[pl.BlockSpec((e, LANES), lambda i: (0, 0)),
                   pl.BlockSpec((e, LANES), lambda i: (0, 0))],
        out_shape=[jax.ShapeDtypeStruct((e, LANES), I32),
                   jax.ShapeDtypeStruct((e, LANES), I32)],
        compiler_params=_cparams("arbitrary"),
    )(afft)


def _positions_kernel(aff_ref, thr_ref, need_ref, lpos_ref, lpost_ref, starts_ref, npass_ref, base_sc, eqb_sc):
    @pl.when(pl.program_id(0) == 0)
    def _():
        base_sc[...] = jnp.zeros_like(base_sc)
        eqb_sc[...] = jnp.zeros_like(eqb_sc)

    thr = thr_ref[:, 0:1]
    need = need_ref[:, 0:1].astype(F32)
    r = lax.broadcasted_iota(I32, (ROUTE_TILE, ROUTE_TILE), 0)
    c = lax.broadcasted_iota(I32, (ROUTE_TILE, ROUTE_TILE), 1)
    before = jnp.where(r < c, 1.0, 0.0).astype(BF16)
    pad = jnp.full((LANES - N_EXPERTS, ROUTE_TILE), -1.0, F32)
    base = base_sc[...]
    eqb = eqb_sc[...]
    for k in range(starts_ref.shape[0]):
        cols = slice(k * ROUTE_TILE, (k + 1) * ROUTE_TILE)
        bits = lax.bitcast_convert_type(aff_ref[:, cols], I32)
        eq = bits == thr
        eqf = jnp.where(eq, 1.0, 0.0)
        eq_rank = eqb[:, 0:1] + jnp.dot(eqf.astype(BF16), before, preferred_element_type=F32)
        sel = (bits > thr) | (eq & (eq_rank < need))
        self_ = jnp.where(sel, 1.0, 0.0)
        rank = jnp.dot(self_.astype(BF16), before, preferred_element_type=F32)
        cnt = jnp.sum(self_, axis=1, keepdims=True)
        lpos = jnp.where(sel, rank, -1.0)
        lpos_ref[:, cols] = lpos.astype(I32)
        lpost_ref[cols, :] = jnp.concatenate([lpos, pad], axis=0).T
        starts_ref[k] = base.astype(I32)
        most = jnp.max(cnt, axis=0, keepdims=True)
        passes = sum(jnp.where(most > float(first), 1.0, 0.0) for first in range(0, ROUTE_TILE, SLAB))
        npass_ref[k] = jnp.broadcast_to(passes, npass_ref.shape[1:]).astype(I32)
        base = base + cnt
        eqb = eqb + jnp.sum(eqf, axis=1, keepdims=True)
    base_sc[...] = base
    eqb_sc[...] = eqb


def _positions_call(afft, thr, need):
    e, t = afft.shape
    n_tiles = t // ROUTE_TILE
    per_step = min(POSITION_TILES_PER_STEP, n_tiles)
    assert n_tiles % per_step == 0
    span = per_step * ROUTE_TILE
    return pl.pallas_call(
        _positions_kernel,
        name="positions",
        grid=(n_tiles // per_step,),
        in_specs=[pl.BlockSpec((e, span), lambda i: (0, i)),
                  pl.BlockSpec((e, LANES), lambda i: (0, 0)),
                  pl.BlockSpec((e, LANES), lambda i: (0, 0))],
        out_specs=[pl.BlockSpec((e, span), lambda i: (0, i)),
                   pl.BlockSpec((span, LANES), lambda i: (i, 0)),
                   pl.BlockSpec((per_step, e, LANES), lambda i: (i, 0, 0)),
                   pl.BlockSpec((per_step, SUBLANES, LANES), lambda i: (i, 0, 0))],
        out_shape=[jax.ShapeDtypeStruct((e, t), I32),
                   jax.ShapeDtypeStruct((t, LANES), F32),
                   jax.ShapeDtypeStruct((n_tiles, e, LANES), I32),
                   jax.ShapeDtypeStruct((n_tiles, SUBLANES, LANES), I32)],
        scratch_shapes=[pltpu.VMEM((e, LANES), F32), pltpu.VMEM((e, LANES), F32)],
        compiler_params=_cparams("arbitrary"),
    )(afft, thr, need)


SLAB_ROWS = SLAB * COL_BLOCKS
READ_SLAB = SLAB + SUBLANES


def _dispatch_kernel(starts_sm, npass_sm, h_ref, lpos_ref, xe_ref, slab_sc, sem, state_sm, *, cap):
    i = pl.program_id(0)
    sub = lax.broadcasted_iota(I32, (SLAB, ROUTE_TILE), 0)

    def slab_copy(buf, e, dst):
        return pltpu.make_async_copy(
            slab_sc.at[buf, pl.ds(e * SLAB_ROWS, SLAB_ROWS)],
            xe_ref.at[e, pl.ds(pl.multiple_of(dst * COL_BLOCKS, COL_BLOCKS), SLAB_ROWS)],
            sem.at[buf, e])

    def wait_all(buf):
        for e in range(N_EXPERTS):
            slab_copy(buf, e, 0).wait()

    @pl.when(i == 0)
    def _():
        slab_sc[0, pl.ds(0, SLAB_ROWS), :] = jnp.zeros((SLAB_ROWS, LANES), F32)
        pads = [pltpu.make_async_copy(slab_sc.at[0, pl.ds(0, SLAB_ROWS)],
                                      xe_ref.at[e, pl.ds(cap * COL_BLOCKS, SLAB_ROWS)], sem.at[0, e])
                for e in range(N_EXPERTS)]
        for cp in pads:
            cp.start()
        for cp in pads:
            cp.wait()
        state_sm[0] = 0
        state_sm[1] = 0

    def one_pass(s, carry):
        buf = state_sm[1]
        rows = []
        for e in range(N_EXPERTS):
            lp = lpos_ref[e:e + 1, :] - SLAB * s
            rows.append(jnp.where(lp == sub, 1.0, 0.0).astype(BF16))
        onehot = jnp.concatenate(rows, axis=0)
        res = jnp.dot(onehot, h_ref[...], preferred_element_type=F32)
        for cb in range(COL_BLOCKS):
            slab_sc[buf, pl.ds(cb, N_EXPERTS * SLAB, stride=COL_BLOCKS), :] = res[:, cb * LANES:(cb + 1) * LANES]

        @pl.when(state_sm[0] == 1)
        def _():
            wait_all(1 - buf)

        for e in range(N_EXPERTS):
            dst = jnp.minimum(starts_sm[i * N_EXPERTS + e] + SLAB * s, cap)
            slab_copy(buf, e, dst).start()
        state_sm[0] = 1
        state_sm[1] = 1 - buf
        return carry

    lax.fori_loop(0, npass_sm[i], one_pass, 0)

    @pl.when((i == pl.num_programs(0) - 1) & (state_sm[0] == 1))
    def _():
        wait_all(1 - state_sm[1])


def _dispatch_call(starts, npass, h, lpos, cap):
    t, d = h.shape
    n_tiles = t // ROUTE_TILE
    grid_spec = pltpu.PrefetchScalarGridSpec(
        num_scalar_prefetch=2,
        grid=(n_tiles,),
        in_specs=[pl.BlockSpec((ROUTE_TILE, d), lambda i, *_: (i, 0)),
                  pl.BlockSpec((N_EXPERTS, ROUTE_TILE), lambda i, *_: (0, i))],
        out_specs=pl.BlockSpec(memory_space=pl.ANY),
        scratch_shapes=[pltpu.VMEM((2, N_EXPERTS * SLAB_ROWS, LANES), F32),
                        pltpu.SemaphoreType.DMA((2, N_EXPERTS)),
                        pltpu.SMEM((2,), I32)],
    )
    return pl.pallas_call(
        functools.partial(_dispatch_kernel, cap=cap),
        name="dispatch",
        grid_spec=grid_spec,
        out_shape=jax.ShapeDtypeStruct((N_EXPERTS, (cap + SLAB) * COL_BLOCKS, LANES), F32),
        compiler_params=_cparams("arbitrary"),
    )(starts, npass, h, lpos)


def _expert_kernel(x_ref, wg_ref, wu_ref, wd_ref, o_ref):
    tm = x_ref.shape[1] // COL_BLOCKS
    x = jnp.concatenate([x_ref[0, pl.ds(cb, tm, stride=COL_BLOCKS), :] for cb in range(COL_BLOCKS)],
                        axis=1).astype(BF16)
    acc = jnp.zeros((tm, D_MODEL), F32)
    for fc in range(EXPERT_FF // FF_CHUNK):
        fs = slice(fc * FF_CHUNK, (fc + 1) * FF_CHUNK)
        g = jnp.dot(x, wg_ref[0, 0, :, fs], preferred_element_type=F32)
        u = jnp.dot(x, wu_ref[0, 0, :, fs], preferred_element_type=F32)
        hmid = (g * (1.0 / (1.0 + jnp.exp(-g))) * u).astype(BF16)
        acc = acc + jnp.dot(hmid, wd_ref[0, 0, fs, :], preferred_element_type=F32)
    for cb in range(COL_BLOCKS):
        o_ref[0, cb] = acc[:, cb * LANES:(cb + 1) * LANES]


def _expert_call(xe, wg, wu, wd, layer, cap):
    tm = min(ROW_TILE, cap)
    d, f = wg.shape[2], wg.shape[3]
    return pl.pallas_call(
        _expert_kernel,
        name="expert",
        grid=(N_EXPERTS, cap // tm),
        in_specs=[pl.BlockSpec((1, tm * COL_BLOCKS, LANES), lambda e, j: (e, j, 0)),
                  pl.BlockSpec((1, 1, d, f), lambda e, j: (layer, e, 0, 0)),
                  pl.BlockSpec((1, 1, d, f), lambda e, j: (layer, e, 0, 0)),
                  pl.BlockSpec((1, 1, f, d), lambda e, j: (layer, e, 0, 0))],
        out_specs=pl.BlockSpec((1, COL_BLOCKS, tm, LANES), lambda e, j: (e, 0, j, 0)),
        out_shape=jax.ShapeDtypeStruct((N_EXPERTS, COL_BLOCKS, cap, LANES), F32),
        compiler_params=_cparams("parallel", "parallel"),
    )(xe, wg, wu, wd)


def _combine_kernel(starts_sm, npass_sm, x_ref, aff_ref, lpost_ref, expand_ref, g_ref, ye_ref, o_ref,
                    slab_sc, sem, *, cap, final):
    i = pl.program_id(0)
    lane = lax.broadcasted_iota(I32, (1, LANES), 1)
    slot_in_slab = (lax.broadcasted_iota(I32, (1, N_EXPERTS * READ_SLAB), 1) % READ_SLAB).astype(F32)
    aff = aff_ref[...]
    ghi = aff.astype(BF16)
    glo = (aff - ghi.astype(F32)).astype(BF16)
    expand = expand_ref[...]
    lpost = lpost_ref[...]
    buf = i % 2

    def slab_src(tile, s, e):
        want = starts_sm[tile * N_EXPERTS + e] + SLAB * s
        src = jnp.minimum((want // SUBLANES) * SUBLANES, cap - READ_SLAB)
        return want, pl.multiple_of(src, SUBLANES)

    def slab_copy(tile, s, to, e):
        src = slab_src(tile, s, e)[1]
        return pltpu.make_async_copy(
            ye_ref.at[e, :, pl.ds(src, READ_SLAB), :],
            slab_sc.at[to, :, pl.ds(e * READ_SLAB, READ_SLAB), :],
            sem.at[to, e])

    def fetch(tile, s, to):
        for e in range(N_EXPERTS):
            slab_copy(tile, s, to, e).start()

    def slab_row(s):
        shift = jnp.zeros((1, LANES), F32)
        for e in range(N_EXPERTS):
            want, src = slab_src(i, s, e)
            shift = jnp.where(lane == e, (want - src).astype(F32), shift)
        lo = jnp.asarray(SLAB * s, F32)
        owned = (lpost >= lo) & (lpost < lo + SLAB)
        return jnp.where(owned, lpost - lo + shift, -1.0).astype(BF16)

    spread = jnp.dot(jnp.concatenate([ghi, glo, slab_row(0)], axis=0), expand, preferred_element_type=F32)
    ghi_x = spread[0:ROUTE_TILE]
    glo_x = spread[ROUTE_TILE:2 * ROUTE_TILE]

    def add_pass(s, row_x, acc):
        match = row_x == slot_in_slab
        w = jnp.concatenate([jnp.where(match, ghi_x, 0.0).astype(BF16),
                             jnp.where(match, glo_x, 0.0).astype(BF16)], axis=0)
        for e in range(N_EXPERTS):
            slab_copy(i, s, buf, e).wait()
        rows = jnp.concatenate([slab_sc[buf, cb] for cb in range(COL_BLOCKS)],
                               axis=1).astype(BF16)
        both = jnp.dot(w, rows, preferred_element_type=F32)
        return acc + (both[0:ROUTE_TILE] + both[ROUTE_TILE:2 * ROUTE_TILE])

    @pl.when(i == 0)
    def _():
        fetch(0, 0, 0)

    @pl.when(i + 1 < pl.num_programs(0))
    def _():
        fetch(i + 1, 0, 1 - buf)

    y = add_pass(0, spread[2 * ROUTE_TILE:3 * ROUTE_TILE], jnp.zeros(x_ref.shape, F32))

    def extra_pass(s, acc):
        fetch(i, s, buf)
        return add_pass(s, jnp.dot(slab_row(s), expand, preferred_element_type=F32), acc)

    y = lax.fori_loop(1, npass_sm[i], extra_pass, y)
    out = x_ref[...] + y
    if final:
        out = _rmsnorm(out, g_ref[...])
    o_ref[...] = out


def _combine_call(starts, npass, x1, aff, lpost, expand, g, ye, cap, final):
    t, d = x1.shape
    n_tiles = t // ROUTE_TILE
    grid_spec = pltpu.PrefetchScalarGridSpec(
        num_scalar_prefetch=2,
        grid=(n_tiles,),
        in_specs=[pl.BlockSpec((ROUTE_TILE, d), lambda i, *_: (i, 0)),
                  pl.BlockSpec((ROUTE_TILE, LANES), lambda i, *_: (i, 0)),
                  pl.BlockSpec((ROUTE_TILE, LANES), lambda i, *_: (i, 0)),
                  pl.BlockSpec((LANES, N_EXPERTS * READ_SLAB), lambda i, *_: (0, 0)),
                  pl.BlockSpec((1, d), lambda i, *_: (0, 0)),
                  pl.BlockSpec(memory_space=pl.ANY)],
        out_specs=pl.BlockSpec((ROUTE_TILE, d), lambda i, *_: (i, 0)),
        scratch_shapes=[pltpu.VMEM((2, COL_BLOCKS, N_EXPERTS * READ_SLAB, LANES), F32),
                        pltpu.SemaphoreType.DMA((2, N_EXPERTS))],
    )
    return pl.pallas_call(
        functools.partial(_combine_kernel, cap=cap, final=final),
        name="combine",
        grid_spec=grid_spec,
        out_shape=jax.ShapeDtypeStruct((t, d), F32),
        compiler_params=_cparams("arbitrary"),
    )(starts, npass, x1, aff, lpost, expand, g, ye)


def _moe(x1, h, aff, afft, wg, wu, wd, layer, expand, g_final, final):
    t = x1.shape[0]
    cap = 2 * t // N_EXPERTS
    assert t % ROUTE_TILE == 0 and cap >= READ_SLAB and cap % SUBLANES == 0
    thr, need = _threshold_call(afft, cap)
    lpos, lpost, starts3, npass3 = _positions_call(afft, thr, need)
    starts = starts3[:, :, 0].reshape(-1)
    npass = npass3[:, 0, 0]
    xe = _dispatch_call(starts, npass, h, lpos, cap)
    ye = _expert_call(xe, wg, wu, wd, layer, cap)
    return _combine_call(starts, npass, x1, aff, lpost, expand, g_final, ye, cap, final)


def _trunk(x, p):
    batch, seq, d = x.shape
    x = x.reshape(batch * seq, d)
    depth = p["norm_mix"].shape[0]
    for i in range(depth):
        j = i // 2
        if i % 2 == 0:
            qk, vt = _qkv_call(x, p["norm_mix"][i:i + 1], p["a_wqkv"][j], N_PLANES)
            o = _attn_call("na_attn", qk, vt, p["a_bias"][j], None, batch, seq,
                           NA_UNIT, NA_UNITS_PER_STEP, N_PLANES)
            wo = p["a_wo"][j]
        else:
            qk, vt = _qkv_call(x, p["norm_mix"][i:i + 1], p["b_wqkv"][j], 2)
            o = _attn_call("sw_attn", qk, vt, p["b_bias"], p["b_sink"][j], batch, seq,
                           SW_UNIT, SW_UNITS_PER_STEP, 2)
            wo = p["b_wo"][j]
        x1, h, aff, afft = _post_call(o, x, wo, p["norm_ffn"][i:i + 1], p["wr_hi"][i], p["wr_hl"][i])
        x = _moe(x1, h, aff, afft, p["w_gate"], p["w_up"], p["w_down"], i,
                 p["expand"], p["norm_final"], i == depth - 1)
    return x.reshape(batch, seq, d)


def _prepare(norm_mix, norm_ffn, norm_final, a_wqkv, a_wo, a_rpb, b_wqkv, b_wo, b_sink,
             w_router, w_gate, w_up, w_down):
    d = D_MODEL
    scale = HEAD_DIM ** -0.5 * LOG2E
    a_w = jnp.concatenate([a_wqkv[:, :, :d] * scale, a_wqkv[:, :, d:]], axis=2).astype(BF16)
    order = SW_HEAD_ORDER
    nb = b_wqkv.shape[0]
    bq = (b_wqkv[:, :, :d] * scale).reshape(nb, d, N_HEADS, HEAD_DIM)[:, :, order].reshape(nb, d, d)
    b_w = jnp.concatenate([bq, b_wqkv[:, :, d:]], axis=2).astype(BF16)
    b_wo_p = b_wo.reshape(nb, N_HEADS, HEAD_DIM, d)[:, order].reshape(nb, d, d).astype(BF16)
    sink = (b_sink.astype(F32) * LOG2E)[:, order].reshape(nb, 2, 1, 8, 1)
    sink = jnp.broadcast_to(sink, (nb, 2, SUBLANES, 8, SW_UNIT)).reshape(nb, 2, SUBLANES, 8 * SW_UNIT)
    wr = w_router.astype(F32).transpose(0, 2, 1)
    wr_hi = wr.astype(BF16)
    wr_lo = (wr - wr_hi.astype(F32)).astype(BF16)
    wr_hl = jnp.concatenate([wr_hi, wr_lo], axis=1)
    lane_expert = np.arange(N_EXPERTS * READ_SLAB) // READ_SLAB
    expand = jnp.asarray(np.arange(LANES)[:, None] == lane_expert[None, :], dtype=BF16)
    return {
        "norm_mix": norm_mix.astype(F32), "norm_ffn": norm_ffn.astype(F32),
        "norm_final": norm_final.astype(F32).reshape(1, d),
        "a_wqkv": a_w, "a_wo": a_wo.astype(BF16),
        "a_bias": jnp.stack([_na_bias_table(a_rpb[j] * LOG2E) for j in range(a_rpb.shape[0])]),
        "b_wqkv": b_w, "b_wo": b_wo_p, "b_bias": _sw_bias_table(), "b_sink": sink,
        "wr_hi": wr_hi, "wr_hl": wr_hl,
        "w_gate": w_gate.astype(BF16), "w_up": w_up.astype(BF16), "w_down": w_down.astype(BF16),
        "expand": expand,
    }


def kernel(x_prompt, x_sample, norm_mix, norm_ffn, norm_final, a_wqkv, a_wo, a_rpb, b_wqkv, b_wo, b_sink,
           w_router, w_gate, w_up, w_down):
    p = _prepare(norm_mix, norm_ffn, norm_final, a_wqkv, a_wo, a_rpb, b_wqkv, b_wo, b_sink,
                 w_router, w_gate, w_up, w_down)
    return (_trunk(x_prompt, p), _trunk(x_sample, p))
```

```python
import functools

import numpy as np
import jax
import jax.numpy as jnp
from jax import lax
from jax.experimental import pallas as pl
from jax.experimental.pallas import tpu as pltpu

F32 = jnp.float32
BF16 = jnp.bfloat16
I32 = jnp.int32

D_MODEL = 1024
HEAD_DIM = 64
N_HEADS = 16
LANES = 128
SUBLANES = 8
N_PLANES = D_MODEL // LANES
COL_BLOCKS = D_MODEL // LANES
GRID_W = 64
NA_KH = 8
NA_KW = 16
NA_UNIT = 4 * GRID_W
NA_UNITS_PER_STEP = 8
SW_KV_HEADS = 4
SW_WINDOW = 128
SW_UNIT = 128
SW_UNITS_PER_STEP = 8
N_EXPERTS = 16
EXPERT_FF = 2048
FF_CHUNK = 512
ROUTE_TILE = 256
POSITION_TILES_PER_STEP = 8
SLAB = 56
RMS_EPS = 1e-6
NEG = -1e30
LOG2E = 1.4426950408889634
ROW_TILE = 512
DENSE_ROW_TILE = 1024
VMEM_LIMIT = 56 * 1024 * 1024


def _cparams(*sem):
    return pltpu.CompilerParams(dimension_semantics=sem, vmem_limit_bytes=VMEM_LIMIT)


def _rmsnorm(x, g):
    return x * lax.rsqrt(jnp.mean(x * x, axis=-1, keepdims=True) + RMS_EPS) * g


def _qkv_kernel(x_ref, g_ref, w_ref, qk_ref, vt_ref):
    y = _rmsnorm(x_ref[...], g_ref[...]).astype(BF16)
    r = jnp.dot(y, w_ref[...], preferred_element_type=F32)
    n_qk = qk_ref.shape[0]
    for p in range(n_qk):
        qk_ref[p] = r[:, p * LANES:(p + 1) * LANES].astype(BF16)
    for p in range(vt_ref.shape[0]):
        vt_ref[p] = r[:, (n_qk + p) * LANES:(n_qk + p + 1) * LANES].T.astype(BF16)


def _qkv_call(x, g, w, n_v_planes):
    t, d = x.shape
    n_qk = w.shape[1] // LANES - n_v_planes
    tm = min(DENSE_ROW_TILE, t)
    return pl.pallas_call(
        _qkv_kernel,
        name="qkv",
        grid=(t // tm,),
        in_specs=[pl.BlockSpec((tm, d), lambda i: (i, 0)),
                  pl.BlockSpec((1, d), lambda i: (0, 0)),
                  pl.BlockSpec((d, w.shape[1]), lambda i: (0, 0))],
        out_specs=[pl.BlockSpec((n_qk, tm, LANES), lambda i: (0, i, 0)),
                   pl.BlockSpec((n_v_planes, LANES, tm), lambda i: (0, 0, i))],
        out_shape=[jax.ShapeDtypeStruct((n_qk, t, LANES), BF16),
                   jax.ShapeDtypeStruct((n_v_planes, LANES, t), BF16)],
        compiler_params=_cparams("parallel"),
    )(x, g, w)


def _edge_case(u, n_units):
    i = pl.program_id(2)
    last = pl.num_programs(2) - 1
    if n_units == 1:
        return jnp.where(i == 0, 1, jnp.where(i == last, 2, 0))
    if u == 0:
        return jnp.where(i == 0, 1, 0)
    if u == n_units - 1:
        return jnp.where(i == last, 2, 0)
    return 0


def _attn_kernel(*refs, unit, n_units, has_sink):
    q_ref, kp_ref, kc_ref, kn_ref, vp_ref, vc_ref, vn_ref, bias_ref = refs[:8]
    sink_ref = refs[8] if has_sink else None
    o_ref = refs[-1]
    n_planes = q_ref.shape[0]
    kcat = jnp.concatenate([kp_ref[0], kc_ref[0], kn_ref[0]], axis=0)
    vcat = jnp.concatenate([vp_ref[0], vc_ref[0], vn_ref[0]], axis=1)
    vcat = jnp.concatenate([vcat, jnp.ones((SUBLANES, vcat.shape[1]), BF16)], axis=0)
    lo_half = lax.broadcasted_iota(I32, (unit, LANES), 1) < HEAD_DIM
    zero = jnp.zeros((unit, LANES), BF16)
    def scores(u):
        kw = kcat[u * unit:(u + 3) * unit]
        qs = [q_ref[pi, u * unit:(u + 1) * unit, :] for pi in range(n_planes)]
        q_stack = jnp.concatenate(
            [jnp.where(lo_half if hh == 0 else jnp.logical_not(lo_half), q, zero) for q in qs for hh in range(2)],
            axis=0)
        return lax.dot_general(kw, q_stack, (((1,), (1,)), ((), ())), preferred_element_type=F32)

    st_next = scores(0)
    for u in range(n_units):
        case = _edge_case(u, n_units)
        vw = vcat[:, u * unit:(u + 3) * unit]
        st = st_next + bias_ref[0, case]
        if u + 1 < n_units:
            st_next = scores(u + 1)
        m = jnp.max(st, axis=0, keepdims=True)
        if has_sink:
            sink = sink_ref[0, 0:1, :]
            m = jnp.maximum(m, sink)
        e = jnp.exp2(st - m)
        ot = jnp.dot(vw, e.astype(BF16), preferred_element_type=F32)
        l = ot[LANES:LANES + 1, :]
        if has_sink:
            l = l + jnp.exp2(sink - m)
        ot = ot[0:LANES, :] / l
        for pi in range(n_planes):
            both = jnp.concatenate(
                [ot[0:HEAD_DIM, (2 * pi) * unit:(2 * pi + 1) * unit],
                 ot[HEAD_DIM:LANES, (2 * pi + 1) * unit:(2 * pi + 2) * unit]], axis=0)
            o_ref[pi, u * unit:(u + 1) * unit, :] = both.T.astype(BF16)


def _attn_call(name, qk, vt, bias, sink, batch, seq, unit, n_units, n_groups):
    t = batch * seq
    units = seq // unit
    n_units = min(n_units, units)
    steps = units // n_units
    assert seq % (unit * n_units) == 0 and units >= 4
    ppg = N_PLANES // n_groups
    qb = unit * n_units
    width = 2 * ppg * unit

    def cur(g, b, i):
        return b * steps + i

    def prev(g, b, i):
        return b * units + jnp.maximum(n_units * i - 1, 0)

    def nxt(g, b, i):
        return b * units + jnp.minimum(n_units * i + n_units, units - 1)

    in_specs = [pl.BlockSpec((ppg, qb, LANES), lambda g, b, i: (g, cur(g, b, i), 0)),
                pl.BlockSpec((1, unit, LANES), lambda g, b, i: (N_PLANES + g, prev(g, b, i), 0)),
                pl.BlockSpec((1, qb, LANES), lambda g, b, i: (N_PLANES + g, cur(g, b, i), 0)),
                pl.BlockSpec((1, unit, LANES), lambda g, b, i: (N_PLANES + g, nxt(g, b, i), 0)),
                pl.BlockSpec((1, LANES, unit), lambda g, b, i: (g, 0, prev(g, b, i))),
                pl.BlockSpec((1, LANES, qb), lambda g, b, i: (g, 0, cur(g, b, i))),
                pl.BlockSpec((1, LANES, unit), lambda g, b, i: (g, 0, nxt(g, b, i))),
                pl.BlockSpec((1, 3, 3 * unit, width), lambda g, b, i: (g, 0, 0, 0))]
    args = [qk, qk, qk, qk, vt, vt, vt, bias]
    if sink is not None:
        in_specs.append(pl.BlockSpec((1, SUBLANES, width), lambda g, b, i: (g, 0, 0)))
        args.append(sink)
    return pl.pallas_call(
        functools.partial(_attn_kernel, unit=unit, n_units=n_units, has_sink=sink is not None),
        name=name,
        grid=(n_groups, batch, steps),
        in_specs=in_specs,
        out_specs=pl.BlockSpec((ppg, qb, LANES), lambda g, b, i: (g, cur(g, b, i), 0)),
        out_shape=jax.ShapeDtypeStruct((N_PLANES, t, LANES), BF16),
        compiler_params=_cparams("arbitrary", "arbitrary", "arbitrary"),
    )(*args)


def _na_bias_table(rpb):
    a = np.arange(NA_UNIT) // GRID_W
    j = np.arange(3 * NA_UNIT) // GRID_W
    row_ok = np.stack([
        (j[None, :] >= a[:, None]) & (j[None, :] < a[:, None] + NA_KH),
        np.broadcast_to((j[None, :] >= 4) & (j[None, :] < 4 + NA_KH), (NA_UNIT, 3 * NA_UNIT)),
        np.broadcast_to(j[None, :] < NA_KH, (NA_UNIT, 3 * NA_UNIT)),
    ])
    w = np.arange(GRID_W)
    dc = np.clip(w[None, :] - w[:, None], -(NA_KW - 1), NA_KW - 1) + NA_KW - 1
    c0 = np.clip(w - NA_KW // 2, 0, GRID_W - NA_KW)
    col_ok = (w[None, :] >= c0[:, None]) & (w[None, :] < c0[:, None] + NA_KW)
    cols = jnp.take(rpb.astype(F32), jnp.asarray(dc.reshape(-1)), axis=2)
    cols = cols.reshape(N_HEADS, 2 * NA_KH - 1, GRID_W, GRID_W)
    cols = jnp.where(jnp.asarray(col_ok), cols, NEG)
    rows = jnp.stack([cols[:, 3 - ai:15 - ai] for ai in range(4)], axis=1)
    tbl = rows.transpose(0, 2, 4, 1, 3).reshape(N_HEADS, 3 * NA_UNIT, NA_UNIT)
    ok = jnp.asarray(row_ok.transpose(0, 2, 1))
    full = jnp.where(ok[:, None], tbl[None], NEG)
    full = full.reshape(3, N_PLANES, 2, 3 * NA_UNIT, NA_UNIT).transpose(1, 0, 3, 2, 4)
    return full.reshape(N_PLANES, 3, 3 * NA_UNIT, 2 * NA_UNIT)


SW_HEAD_ORDER = np.array([8 * m + 4 * hh + i for m in range(2) for i in range(4) for hh in range(2)])


def _sw_bias_table():
    slopes = np.asarray(2.0 ** (-8.0 * np.arange(1, N_HEADS + 1) / N_HEADS), dtype=np.float32)
    ks = np.arange(3 * SW_UNIT) - SW_WINDOW
    dist = ks[None, :] - np.arange(SW_UNIT)[:, None]
    win_ok = np.abs(dist) <= SW_WINDOW
    pos_ok = np.stack([np.ones(3 * SW_UNIT, bool), ks >= 0, ks < SW_UNIT])
    ok = win_ok[None] & pos_ok[:, None, :]
    alibi = -slopes[SW_HEAD_ORDER][:, None, None] * np.abs(dist).astype(np.float32)[None]
    alibi = alibi * np.float32(LOG2E)
    full = np.where(ok[:, None], alibi[None], np.float32(NEG)).astype(np.float32)
    full = full.reshape(3, 2, 8, SW_UNIT, 3 * SW_UNIT).transpose(1, 0, 4, 2, 3)
    return jnp.asarray(full.reshape(2, 3, 3 * SW_UNIT, 8 * SW_UNIT))


def _post_kernel(o_ref, x_ref, wo_ref, g_ref, wrh_ref, wrhl_ref, x1_ref, h_ref, aff_ref, afft_ref):
    oc = jnp.concatenate([o_ref[p] for p in range(N_PLANES)], axis=1)
    x1 = x_ref[...] + jnp.dot(oc, wo_ref[...], preferred_element_type=F32)
    x1_ref[...] = x1
    h = _rmsnorm(x1, g_ref[...])
    hb = h.astype(BF16)
    h_ref[...] = hb
    hl = (h - hb.astype(F32)).astype(BF16)
    nt = (((1,), (1,)), ((), ()))
    by_hb = lax.dot_general(wrhl_ref[...], hb, nt, preferred_element_type=F32)
    by_hl = lax.dot_general(wrh_ref[...], hl, nt, preferred_element_type=F32)
    logits = by_hb[0:N_EXPERTS] + (by_hb[N_EXPERTS:2 * N_EXPERTS] + by_hl)
    ex = jnp.exp(logits - jnp.max(logits, axis=0, keepdims=True))
    afft = ex / jnp.sum(ex, axis=0, keepdims=True)
    afft_ref[...] = afft
    pad = jnp.zeros((LANES - N_EXPERTS, afft.shape[1]), F32)
    aff_ref[...] = jnp.concatenate([afft, pad], axis=0).T


def _post_call(o, x, wo, g, wrh, wrhl):
    t, d = x.shape
    tm = min(DENSE_ROW_TILE, t)
    return pl.pallas_call(
        _post_kernel,
        name="post",
        grid=(t // tm,),
        in_specs=[pl.BlockSpec((N_PLANES, tm, LANES), lambda i: (0, i, 0)),
                  pl.BlockSpec((tm, d), lambda i: (i, 0)),
                  pl.BlockSpec((d, d), lambda i: (0, 0)),
                  pl.BlockSpec((1, d), lambda i: (0, 0)),
                  pl.BlockSpec((N_EXPERTS, d), lambda i: (0, 0)),
                  pl.BlockSpec((2 * N_EXPERTS, d), lambda i: (0, 0))],
        out_specs=[pl.BlockSpec((tm, d), lambda i: (i, 0)),
                   pl.BlockSpec((tm, d), lambda i: (i, 0)),
                   pl.BlockSpec((tm, LANES), lambda i: (i, 0)),
                   pl.BlockSpec((N_EXPERTS, tm), lambda i: (0, i))],
        out_shape=[jax.ShapeDtypeStruct((t, d), F32),
                   jax.ShapeDtypeStruct((t, d), BF16),
                   jax.ShapeDtypeStruct((t, LANES), F32),
                   jax.ShapeDtypeStruct((N_EXPERTS, t), F32)],
        compiler_params=_cparams("parallel"),
    )(o, x, wo, g, wrh, wrhl)


def _threshold_kernel(aff_ref, thr_ref, need_ref, *, cap):
    bits = lax.bitcast_convert_type(aff_ref[...], I32)

    def body(k, cur):
        cand = cur | jnp.left_shift(jnp.int32(1), 30 - k)
        cnt = jnp.sum(jnp.where(bits >= cand, 1, 0), axis=1, keepdims=True)
        return jnp.where(cnt >= cap, cand, cur)

    cur = lax.fori_loop(0, 31, body, jnp.zeros((N_EXPERTS, 1), I32))
    above = jnp.sum(jnp.where(bits > cur, 1, 0), axis=1, keepdims=True)
    thr_ref[...] = jnp.broadcast_to(cur, thr_ref.shape)
    need_ref[...] = jnp.broadcast_to(cap - above, need_ref.shape)


def _threshold_call(afft, cap):
    e, t = afft.shape
    return pl.pallas_call(
        functools.partial(_threshold_kernel, cap=cap),
        name="threshold",
        grid=(1,),
        in_specs=[pl.BlockSpec((e, t), lambda i: (0, 0))],
        out_specs=[pl.BlockSpec((e, LANES), lambda i: (0, 0)),
                   pl.BlockSpec((e, LANES), lambda i: (0, 0))],
        out_shape=[jax.ShapeDtypeStruct((e, LANES), I32),
                   jax.ShapeDtypeStruct((e, LANES), I32)],
        compiler_params=_cparams("arbitrary"),
    )(afft)


def _positions_kernel(aff_ref, thr_ref, need_ref, lpos_ref, lpost_ref, starts_ref, npass_ref, base_sc, eqb_sc):
    @pl.when(pl.program_id(0) == 0)
    def _():
        base_sc[...] = jnp.zeros_like(base_sc)
        eqb_sc[...] = jnp.zeros_like(eqb_sc)

    thr = thr_ref[:, 0:1]
    need = need_ref[:, 0:1].astype(F32)
    r = lax.broadcasted_iota(I32, (ROUTE_TILE, ROUTE_TILE), 0)
    c = lax.broadcasted_iota(I32, (ROUTE_TILE, ROUTE_TILE), 1)
    before = jnp.where(r < c, 1.0, 0.0).astype(BF16)
    pad = jnp.full((LANES - N_EXPERTS, ROUTE_TILE), -1.0, F32)
    base = base_sc[...]
    eqb = eqb_sc[...]
    for k in range(starts_ref.shape[0]):
        cols = slice(k * ROUTE_TILE, (k + 1) * ROUTE_TILE)
        bits = lax.bitcast_convert_type(aff_ref[:, cols], I32)
        eq = bits == thr
        eqf = jnp.where(eq, 1.0, 0.0)
        eq_rank = eqb[:, 0:1] + jnp.dot(eqf.astype(BF16), before, preferred_element_type=F32)
        sel = (bits > thr) | (eq & (eq_rank < need))
        self_ = jnp.where(sel, 1.0, 0.0)
        rank = jnp.dot(self_.astype(BF16), before, preferred_element_type=F32)
        cnt = jnp.sum(self_, axis=1, keepdims=True)
        lpos = jnp.where(sel, rank, -1.0)
        lpos_ref[:, cols] = lpos.astype(I32)
        lpost_ref[cols, :] = jnp.concatenate([lpos, pad], axis=0).T
        starts_ref[k] = base.astype(I32)
        most = jnp.max(cnt, axis=0, keepdims=True)
        passes = sum(jnp.where(most > float(first), 1.0, 0.0) for first in range(0, ROUTE_TILE, SLAB))
        npass_ref[k] = jnp.broadcast_to(passes, npass_ref.shape[1:]).astype(I32)
        base = base + cnt
        eqb = eqb + jnp.sum(eqf, axis=1, keepdims=True)
    base_sc[...] = base
    eqb_sc[...] = eqb


def _positions_call(afft, thr, need):
    e, t = afft.shape
    n_tiles = t // ROUTE_TILE
    per_step = min(POSITION_TILES_PER_STEP, n_tiles)
    assert n_tiles % per_step == 0
    span = per_step * ROUTE_TILE
    return pl.pallas_call(
        _positions_kernel,
        name="positions",
        grid=(n_tiles // per_step,),
        in_specs=[pl.BlockSpec((e, span), lambda i: (0, i)),
                  pl.BlockSpec((e, LANES), lambda i: (0, 0)),
                  pl.BlockSpec((e, LANES), lambda i: (0, 0))],
        out_specs=[pl.BlockSpec((e, span), lambda i: (0, i)),
                   pl.BlockSpec((span, LANES), lambda i: (i, 0)),
                   pl.BlockSpec((per_step, e, LANES), lambda i: (i, 0, 0)),
                   pl.BlockSpec((per_step, SUBLANES, LANES), lambda i: (i, 0, 0))],
        out_shape=[jax.ShapeDtypeStruct((e, t), I32),
                   jax.ShapeDtypeStruct((t, LANES), F32),
                   jax.ShapeDtypeStruct((n_tiles, e, LANES), I32),
                   jax.ShapeDtypeStruct((n_tiles, SUBLANES, LANES), I32)],
        scratch_shapes=[pltpu.VMEM((e, LANES), F32), pltpu.VMEM((e, LANES), F32)],
        compiler_params=_cparams("arbitrary"),
    )(afft, thr, need)


SLAB_ROWS = SLAB * COL_BLOCKS
READ_SLAB = SLAB + SUBLANES


def _dispatch_kernel(starts_sm, npass_sm, h_ref, lpos_ref, xe_ref, slab_sc, sem, state_sm, *, cap):
    i = pl.program_id(0)
    sub = lax.broadcasted_iota(I32, (SLAB, ROUTE_TILE), 0)

    def slab_copy(buf, e, dst):
        return pltpu.make_async_copy(
            slab_sc.at[buf, pl.ds(e * SLAB_ROWS, SLAB_ROWS)],
            xe_ref.at[e, pl.ds(pl.multiple_of(dst * COL_BLOCKS, COL_BLOCKS), SLAB_ROWS)],
            sem.at[buf, e])

    def wait_all(buf):
        for e in range(N_EXPERTS):
            slab_copy(buf, e, 0).wait()

    @pl.when(i == 0)
    def _():
        slab_sc[0, pl.ds(0, SLAB_ROWS), :] = jnp.zeros((SLAB_ROWS, LANES), F32)
        pads = [pltpu.make_async_copy(slab_sc.at[0, pl.ds(0, SLAB_ROWS)],
                                      xe_ref.at[e, pl.ds(cap * COL_BLOCKS, SLAB_ROWS)], sem.at[0, e])
                for e in range(N_EXPERTS)]
        for cp in pads:
            cp.start()
        for cp in pads:
            cp.wait()
        state_sm[0] = 0
        state_sm[1] = 0

    def one_pass(s, carry):
        buf = state_sm[1]
        rows = []
        for e in range(N_EXPERTS):
            lp = lpos_ref[e:e + 1, :] - SLAB * s
            rows.append(jnp.where(lp == sub, 1.0, 0.0).astype(BF16))
        onehot = jnp.concatenate(rows, axis=0)
        res = jnp.dot(onehot, h_ref[...], preferred_element_type=F32)
        for cb in range(COL_BLOCKS):
            slab_sc[buf, pl.ds(cb, N_EXPERTS * SLAB, stride=COL_BLOCKS), :] = res[:, cb * LANES:(cb + 1) * LANES]

        @pl.when(state_sm[0] == 1)
        def _():
            wait_all(1 - buf)

        for e in range(N_EXPERTS):
            dst = jnp.minimum(starts_sm[i * N_EXPERTS + e] + SLAB * s, cap)
            slab_copy(buf, e, dst).start()
        state_sm[0] = 1
        state_sm[1] = 1 - buf
        return carry

    lax.fori_loop(0, npass_sm[i], one_pass, 0)

    @pl.when((i == pl.num_programs(0) - 1) & (state_sm[0] == 1))
    def _():
        wait_all(1 - state_sm[1])


def _dispatch_call(starts, npass, h, lpos, cap):
    t, d = h.shape
    n_tiles = t // ROUTE_TILE
    grid_spec = pltpu.PrefetchScalarGridSpec(
        num_scalar_prefetch=2,
        grid=(n_tiles,),
        in_specs=[pl.BlockSpec((ROUTE_TILE, d), lambda i, *_: (i, 0)),
                  pl.BlockSpec((N_EXPERTS, ROUTE_TILE), lambda i, *_: (0, i))],
        out_specs=pl.BlockSpec(memory_space=pl.ANY),
        scratch_shapes=[pltpu.VMEM((2, N_EXPERTS * SLAB_ROWS, LANES), F32),
                        pltpu.SemaphoreType.DMA((2, N_EXPERTS)),
                        pltpu.SMEM((2,), I32)],
    )
    return pl.pallas_call(
        functools.partial(_dispatch_kernel, cap=cap),
        name="dispatch",
        grid_spec=grid_spec,
        out_shape=jax.ShapeDtypeStruct((N_EXPERTS, (cap + SLAB) * COL_BLOCKS, LANES), F32),
        compiler_params=_cparams("arbitrary"),
    )(starts, npass, h, lpos)


def _expert_kernel(x_ref, wg_hbm, wu_hbm, wd_hbm, o_ref, wgb, wub, wdb, sg, su, sd, sem, *, layer, n_chunks):
    e = pl.program_id(0)
    j = pl.program_id(1)
    n_experts = pl.num_programs(0)
    d_rows = D_MODEL // n_chunks
    f_rows = EXPERT_FF // n_chunks

    def chunk_copies(ex, c):
        slot = (ex * n_chunks + c) % 2
        return [pltpu.make_async_copy(wg_hbm.at[layer, ex, pl.ds(c * d_rows, d_rows)], sg.at[slot], sem.at[0, slot]),
                pltpu.make_async_copy(wu_hbm.at[layer, ex, pl.ds(c * d_rows, d_rows)], su.at[slot], sem.at[1, slot]),
                pltpu.make_async_copy(wd_hbm.at[layer, ex, pl.ds(c * f_rows, f_rows)], sd.at[slot], sem.at[2, slot])]

    def start(ex, c):
        for cp in chunk_copies(ex, c):
            cp.start()

    def finish(ex, c):
        for cp in chunk_copies(ex, c):
            cp.wait()
        slot = (ex * n_chunks + c) % 2
        to = ex % 2
        wgb[to, pl.ds(c * d_rows, d_rows), :] = sg[slot].astype(BF16)
        wub[to, pl.ds(c * d_rows, d_rows), :] = su[slot].astype(BF16)
        wdb[to, pl.ds(c * f_rows, f_rows), :] = sd[slot].astype(BF16)

    @pl.when((e == 0) & (j == 0))
    def _():
        for c in range(n_chunks):
            start(0, c)
            finish(0, c)
        start(1, 0)

    nxt = (e + 1) % n_experts

    @pl.when(j + 1 < n_chunks)
    def _():
        start(nxt, j + 1)

    @pl.when((j + 1 == n_chunks) & (e + 1 < n_experts))
    def _():
        start((e + 2) % n_experts, 0)

    finish(nxt, j)

    cur = e % 2
    tm = x_ref.shape[1] // COL_BLOCKS
    x = jnp.concatenate([x_ref[0, pl.ds(cb, tm, stride=COL_BLOCKS), :] for cb in range(COL_BLOCKS)],
                        axis=1).astype(BF16)
    acc = jnp.zeros((tm, D_MODEL), F32)
    for fc in range(EXPERT_FF // FF_CHUNK):
        fs = slice(fc * FF_CHUNK, (fc + 1) * FF_CHUNK)
        g = jnp.dot(x, wgb[cur, :, fs], preferred_element_type=F32)
        u = jnp.dot(x, wub[cur, :, fs], preferred_element_type=F32)
        hmid = (g * (1.0 / (1.0 + jnp.exp(-g))) * u).astype(BF16)
        acc = acc + jnp.dot(hmid, wdb[cur, fs, :], preferred_element_type=F32)
    for cb in range(COL_BLOCKS):
        o_ref[0, cb] = acc[:, cb * LANES:(cb + 1) * LANES]


def _expert_call(xe, wg, wu, wd, layer, cap):
    tm = min(ROW_TILE, cap)
    n_chunks = cap // tm
    d, f = wg.shape[2], wg.shape[3]
    assert d % n_chunks == 0 and f % n_chunks == 0
    return pl.pallas_call(
        functools.partial(_expert_kernel, layer=layer, n_chunks=n_chunks),
        name="expert",
        grid=(N_EXPERTS, n_chunks),
        in_specs=[pl.BlockSpec((1, tm * COL_BLOCKS, LANES), lambda e, j: (e, j, 0)),
                  pl.BlockSpec(memory_space=pl.ANY),
                  pl.BlockSpec(memory_space=pl.ANY),
                  pl.BlockSpec(memory_space=pl.ANY)],
        out_specs=pl.BlockSpec((1, COL_BLOCKS, tm, LANES), lambda e, j: (e, 0, j, 0)),
        out_shape=jax.ShapeDtypeStruct((N_EXPERTS, COL_BLOCKS, cap, LANES), F32),
        scratch_shapes=[pltpu.VMEM((2, d, f), BF16), pltpu.VMEM((2, d, f), BF16), pltpu.VMEM((2, f, d), BF16),
                        pltpu.VMEM((2, d // n_chunks, f), F32), pltpu.VMEM((2, d // n_chunks, f), F32),
                        pltpu.VMEM((2, f // n_chunks, d), F32),
                        pltpu.SemaphoreType.DMA((3, 2))],
        compiler_params=_cparams("arbitrary", "arbitrary"),
    )(xe, wg, wu, wd)


def _combine_kernel(starts_sm, npass_sm, x_ref, aff_ref, lpost_ref, expand_ref, g_ref, ye_ref, o_ref,
                    slab_sc, sem, *, cap, final):
    i = pl.program_id(0)
    lane = lax.broadcasted_iota(I32, (1, LANES), 1)
    slot_in_slab = (lax.broadcasted_iota(I32, (1, N_EXPERTS * READ_SLAB), 1) % READ_SLAB).astype(F32)
    aff = aff_ref[...]
    ghi = aff.astype(BF16)
    glo = (aff - ghi.astype(F32)).astype(BF16)
    expand = expand_ref[...]
    lpost = lpost_ref[...]
    buf = i % 2

    def slab_src(tile, s, e):
        want = starts_sm[tile * N_EXPERTS + e] + SLAB * s
        src = jnp.minimum((want // SUBLANES) * SUBLANES, cap - READ_SLAB)
        return want, pl.multiple_of(src, SUBLANES)

    def slab_copy(tile, s, to, e):
        src = slab_src(tile, s, e)[1]
        return pltpu.make_async_copy(
            ye_ref.at[e, :, pl.ds(src, READ_SLAB), :],
            slab_sc.at[to, :, pl.ds(e * READ_SLAB, READ_SLAB), :],
            sem.at[to, e])

    def fetch(tile, s, to):
        for e in range(N_EXPERTS):
            slab_copy(tile, s, to, e).start()

    def slab_row(s):
        shift = jnp.zeros((1, LANES), F32)
        for e in range(N_EXPERTS):
            want, src = slab_src(i, s, e)
            shift = jnp.where(lane == e, (want - src).astype(F32), shift)
        lo = jnp.asarray(SLAB * s, F32)
        owned = (lpost >= lo) & (lpost < lo + SLAB)
        return jnp.where(owned, lpost - lo + shift, -1.0).astype(BF16)

    spread = jnp.dot(jnp.concatenate([ghi, glo, slab_row(0)], axis=0), expand, preferred_element_type=F32)
    ghi_x = spread[0:ROUTE_TILE]
    glo_x = spread[ROUTE_TILE:2 * ROUTE_TILE]

    def add_pass(s, row_x, acc):
        match = row_x == slot_in_slab
        w = jnp.concatenate([jnp.where(match, ghi_x, 0.0).astype(BF16),
                             jnp.where(match, glo_x, 0.0).astype(BF16)], axis=0)
        for e in range(N_EXPERTS):
            slab_copy(i, s, buf, e).wait()
        rows = jnp.concatenate([slab_sc[buf, cb] for cb in range(COL_BLOCKS)],
                               axis=1).astype(BF16)
        both = jnp.dot(w, rows, preferred_element_type=F32)
        return acc + (both[0:ROUTE_TILE] + both[ROUTE_TILE:2 * ROUTE_TILE])

    @pl.when(i == 0)
    def _():
        fetch(0, 0, 0)

    @pl.when(i + 1 < pl.num_programs(0))
    def _():
        fetch(i + 1, 0, 1 - buf)

    y = add_pass(0, spread[2 * ROUTE_TILE:3 * ROUTE_TILE], jnp.zeros(x_ref.shape, F32))

    def extra_pass(s, acc):
        fetch(i, s, buf)
        return add_pass(s, jnp.dot(slab_row(s), expand, preferred_element_type=F32), acc)

    y = lax.fori_loop(1, npass_sm[i], extra_pass, y)
    out = x_ref[...] + y
    if final:
        out = _rmsnorm(out, g_ref[...])
    o_ref[...] = out


def _combine_call(starts, npass, x1, aff, lpost, expand, g, ye, cap, final):
    t, d = x1.shape
    n_tiles = t // ROUTE_TILE
    grid_spec = pltpu.PrefetchScalarGridSpec(
        num_scalar_prefetch=2,
        grid=(n_tiles,),
        in_specs=[pl.BlockSpec((ROUTE_TILE, d), lambda i, *_: (i, 0)),
                  pl.BlockSpec((ROUTE_TILE, LANES), lambda i, *_: (i, 0)),
                  pl.BlockSpec((ROUTE_TILE, LANES), lambda i, *_: (i, 0)),
                  pl.BlockSpec((LANES, N_EXPERTS * READ_SLAB), lambda i, *_: (0, 0)),
                  pl.BlockSpec((1, d), lambda i, *_: (0, 0)),
                  pl.BlockSpec(memory_space=pl.ANY)],
        out_specs=pl.BlockSpec((ROUTE_TILE, d), lambda i, *_: (i, 0)),
        scratch_shapes=[pltpu.VMEM((2, COL_BLOCKS, N_EXPERTS * READ_SLAB, LANES), F32),
                        pltpu.SemaphoreType.DMA((2, N_EXPERTS))],
    )
    return pl.pallas_call(
        functools.partial(_combine_kernel, cap=cap, final=final),
        name="combine",
        grid_spec=grid_spec,
        out_shape=jax.ShapeDtypeStruct((t, d), F32),
        compiler_params=_cparams("arbitrary"),
    )(starts, npass, x1, aff, lpost, expand, g, ye)


def _moe(x1, h, aff, afft, wg, wu, wd, layer, expand, g_final, final):
    t = x1.shape[0]
    cap = 2 * t // N_EXPERTS
    assert t % ROUTE_TILE == 0 and cap >= READ_SLAB and cap % SUBLANES == 0
    thr, need = _threshold_call(afft, cap)
    lpos, lpost, starts3, npass3 = _positions_call(afft, thr, need)
    starts = starts3[:, :, 0].reshape(-1)
    npass = npass3[:, 0, 0]
    xe = _dispatch_call(starts, npass, h, lpos, cap)
    ye = _expert_call(xe, wg, wu, wd, layer, cap)
    return _combine_call(starts, npass, x1, aff, lpost, expand, g_final, ye, cap, final)


def _trunk(x, p):
    batch, seq, d = x.shape
    x = x.reshape(batch * seq, d)
    depth = p["norm_mix"].shape[0]
    for i in range(depth):
        j = i // 2
        if i % 2 == 0:
            qk, vt = _qkv_call(x, p["norm_mix"][i:i + 1], p["a_wqkv"][j], N_PLANES)
            o = _attn_call("na_attn", qk, vt, p["a_bias"][j], None, batch, seq,
                           NA_UNIT, NA_UNITS_PER_STEP, N_PLANES)
            wo = p["a_wo"][j]
        else:
            qk, vt = _qkv_call(x, p["norm_mix"][i:i + 1], p["b_wqkv"][j], 2)
            o = _attn_call("sw_attn", qk, vt, p["b_bias"], p["b_sink"][j], batch, seq,
                           SW_UNIT, SW_UNITS_PER_STEP, 2)
            wo = p["b_wo"][j]
        x1, h, aff, afft = _post_call(o, x, wo, p["norm_ffn"][i:i + 1], p["wr_hi"][i], p["wr_hl"][i])
        x = _moe(x1, h, aff, afft, p["w_gate"], p["w_up"], p["w_down"], i,
                 p["expand"], p["norm_final"], i == depth - 1)
    return x.reshape(batch, seq, d)


def _prepare(norm_mix, norm_ffn, norm_final, a_wqkv, a_wo, a_rpb, b_wqkv, b_wo, b_sink,
             w_router, w_gate, w_up, w_down):
    d = D_MODEL
    scale = HEAD_DIM ** -0.5 * LOG2E
    a_w = jnp.concatenate([a_wqkv[:, :, :d] * scale, a_wqkv[:, :, d:]], axis=2).astype(BF16)
    order = SW_HEAD_ORDER
    nb = b_wqkv.shape[0]
    bq = (b_wqkv[:, :, :d] * scale).reshape(nb, d, N_HEADS, HEAD_DIM)[:, :, order].reshape(nb, d, d)
    b_w = jnp.concatenate([bq, b_wqkv[:, :, d:]], axis=2).astype(BF16)
    b_wo_p = b_wo.reshape(nb, N_HEADS, HEAD_DIM, d)[:, order].reshape(nb, d, d).astype(BF16)
    sink = (b_sink.astype(F32) * LOG2E)[:, order].reshape(nb, 2, 1, 8, 1)
    sink = jnp.broadcast_to(sink, (nb, 2, SUBLANES, 8, SW_UNIT)).reshape(nb, 2, SUBLANES, 8 * SW_UNIT)
    wr = w_router.astype(F32).transpose(0, 2, 1)
    wr_hi = wr.astype(BF16)
    wr_lo = (wr - wr_hi.astype(F32)).astype(BF16)
    wr_hl = jnp.concatenate([wr_hi, wr_lo], axis=1)
    lane_expert = np.arange(N_EXPERTS * READ_SLAB) // READ_SLAB
    expand = jnp.asarray(np.arange(LANES)[:, None] == lane_expert[None, :], dtype=BF16)
    return {
        "norm_mix": norm_mix.astype(F32), "norm_ffn": norm_ffn.astype(F32),
        "norm_final": norm_final.astype(F32).reshape(1, d),
        "a_wqkv": a_w, "a_wo": a_wo.astype(BF16),
        "a_bias": jnp.stack([_na_bias_table(a_rpb[j] * LOG2E) for j in range(a_rpb.shape[0])]),
        "b_wqkv": b_w, "b_wo": b_wo_p, "b_bias": _sw_bias_table(), "b_sink": sink,
        "wr_hi": wr_hi, "wr_hl": wr_hl,
        "w_gate": w_gate.astype(F32), "w_up": w_up.astype(F32), "w_down": w_down.astype(F32),
        "expand": expand,
    }


def kernel(x_prompt, x_sample, norm_mix, norm_ffn, norm_final, a_wqkv, a_wo, a_rpb, b_wqkv, b_wo, b_sink,
           w_router, w_gate, w_up, w_down):
    p = _prepare(norm_mix, norm_ffn, norm_final, a_wqkv, a_wo, a_rpb, b_wqkv, b_wo, b_sink,
                 w_router, w_gate, w_up, w_down)
    return (_trunk(x_prompt, p), _trunk(x_sample, p))
```

```python
import functools

import numpy as np
import jax
import jax.numpy as jnp
from jax import lax
from jax.experimental import pallas as pl
from jax.experimental.pallas import tpu as pltpu

F32 = jnp.float32
BF16 = jnp.bfloat16
I32 = jnp.int32

D_MODEL = 1024
HEAD_DIM = 64
N_HEADS = 16
LANES = 128
SUBLANES = 8
N_PLANES = D_MODEL // LANES
COL_BLOCKS = D_MODEL // LANES
GRID_W = 64
NA_KH = 8
NA_KW = 16
NA_UNIT = 4 * GRID_W
NA_UNITS_PER_STEP = 8
SW_KV_HEADS = 4
SW_WINDOW = 128
SW_UNIT = 128
SW_UNITS_PER_STEP = 8
N_EXPERTS = 16
EXPERT_FF = 2048
FF_CHUNK = 512
ROUTE_TILE = 256
POSITION_TILES_PER_STEP = 8
SLAB = 56
RMS_EPS = 1e-6
NEG = -1e30
LOG2E = 1.4426950408889634
ROW_TILE = 512
DENSE_ROW_TILE = 1024
VMEM_LIMIT = 56 * 1024 * 1024


def _cparams(*sem):
    return pltpu.CompilerParams(dimension_semantics=sem, vmem_limit_bytes=VMEM_LIMIT)


def _rmsnorm(x, g):
    return x * lax.rsqrt(jnp.mean(x * x, axis=-1, keepdims=True) + RMS_EPS) * g


def _qkv_kernel(x_ref, g_ref, w_ref, qk_ref, vt_ref):
    y = _rmsnorm(x_ref[...], g_ref[...]).astype(BF16)
    r = jnp.dot(y, w_ref[...], preferred_element_type=F32)
    n_qk = qk_ref.shape[0]
    for p in range(n_qk):
        qk_ref[p] = r[:, p * LANES:(p + 1) * LANES].astype(BF16)
    for p in range(vt_ref.shape[0]):
        vt_ref[p] = r[:, (n_qk + p) * LANES:(n_qk + p + 1) * LANES].T.astype(BF16)


def _qkv_call(x, g, w, n_v_planes):
    t, d = x.shape
    n_qk = w.shape[1] // LANES - n_v_planes
    tm = min(DENSE_ROW_TILE, t)
    return pl.pallas_call(
        _qkv_kernel,
        name="qkv",
        grid=(t // tm,),
        in_specs=[pl.BlockSpec((tm, d), lambda i: (i, 0)),
                  pl.BlockSpec((1, d), lambda i: (0, 0)),
                  pl.BlockSpec((d, w.shape[1]), lambda i: (0, 0))],
        out_specs=[pl.BlockSpec((n_qk, tm, LANES), lambda i: (0, i, 0)),
                   pl.BlockSpec((n_v_planes, LANES, tm), lambda i: (0, 0, i))],
        out_shape=[jax.ShapeDtypeStruct((n_qk, t, LANES), BF16),
                   jax.ShapeDtypeStruct((n_v_planes, LANES, t), BF16)],
        compiler_params=_cparams("parallel"),
    )(x, g, w)


def _edge_case(u, n_units):
    i = pl.program_id(2)
    last = pl.num_programs(2) - 1
    if n_units == 1:
        return jnp.where(i == 0, 1, jnp.where(i == last, 2, 0))
    if u == 0:
        return jnp.where(i == 0, 1, 0)
    if u == n_units - 1:
        return jnp.where(i == last, 2, 0)
    return 0


def _attn_kernel(*refs, unit, n_units, has_edge, has_sink):
    q_ref, kp_ref, kc_ref, kn_ref, vp_ref, vc_ref, vn_ref, bias_ref = refs[:8]
    edge_ref = refs[8] if has_edge else None
    sink_ref = refs[8 + has_edge] if has_sink else None
    o_ref = refs[-1]
    n_planes = q_ref.shape[0]
    kcat = jnp.concatenate([kp_ref[0], kc_ref[0], kn_ref[0]], axis=0)
    vcat = jnp.concatenate([vp_ref[0], vc_ref[0], vn_ref[0]], axis=1)
    vcat = jnp.concatenate([vcat, jnp.ones((SUBLANES, vcat.shape[1]), BF16)], axis=0)
    lo_half = lax.broadcasted_iota(I32, (unit, LANES), 1) < HEAD_DIM
    zero = jnp.zeros((unit, LANES), BF16)
    def scores(u):
        kw = kcat[u * unit:(u + 3) * unit]
        qs = [q_ref[pi, u * unit:(u + 1) * unit, :] for pi in range(n_planes)]
        q_stack = jnp.concatenate(
            [jnp.where(lo_half if hh == 0 else jnp.logical_not(lo_half), q, zero) for q in qs for hh in range(2)],
            axis=0)
        return lax.dot_general(kw, q_stack, (((1,), (1,)), ((), ())), preferred_element_type=F32)

    st_next = scores(0)
    for u in range(n_units):
        case = _edge_case(u, n_units)
        vw = vcat[:, u * unit:(u + 3) * unit]
        if has_edge:
            st = st_next + bias_ref[0] + edge_ref[case]
        else:
            st = st_next + bias_ref[0, case]
        if u + 1 < n_units:
            st_next = scores(u + 1)
        m = jnp.max(st, axis=0, keepdims=True)
        if has_sink:
            sink = sink_ref[0, 0:1, :]
            m = jnp.maximum(m, sink)
        e = jnp.exp2(st - m)
        ot = jnp.dot(vw, e.astype(BF16), preferred_element_type=F32)
        l = ot[LANES:LANES + 1, :]
        if has_sink:
            l = l + jnp.exp2(sink - m)
        ot = ot[0:LANES, :] / l
        for pi in range(n_planes):
            both = jnp.concatenate(
                [ot[0:HEAD_DIM, (2 * pi) * unit:(2 * pi + 1) * unit],
                 ot[HEAD_DIM:LANES, (2 * pi + 1) * unit:(2 * pi + 2) * unit]], axis=0)
            o_ref[pi, u * unit:(u + 1) * unit, :] = both.T.astype(BF16)


def _attn_call(name, qk, vt, bias, edge, sink, batch, seq, unit, n_units, n_groups):
    t = batch * seq
    units = seq // unit
    n_units = min(n_units, units)
    steps = units // n_units
    assert seq % (unit * n_units) == 0 and units >= 4
    ppg = N_PLANES // n_groups
    qb = unit * n_units
    width = 2 * ppg * unit

    def cur(g, b, i):
        return b * steps + i

    def prev(g, b, i):
        return b * units + jnp.maximum(n_units * i - 1, 0)

    def nxt(g, b, i):
        return b * units + jnp.minimum(n_units * i + n_units, units - 1)

    in_specs = [pl.BlockSpec((ppg, qb, LANES), lambda g, b, i: (g, cur(g, b, i), 0)),
                pl.BlockSpec((1, unit, LANES), lambda g, b, i: (N_PLANES + g, prev(g, b, i), 0)),
                pl.BlockSpec((1, qb, LANES), lambda g, b, i: (N_PLANES + g, cur(g, b, i), 0)),
                pl.BlockSpec((1, unit, LANES), lambda g, b, i: (N_PLANES + g, nxt(g, b, i), 0)),
                pl.BlockSpec((1, LANES, unit), lambda g, b, i: (g, 0, prev(g, b, i))),
                pl.BlockSpec((1, LANES, qb), lambda g, b, i: (g, 0, cur(g, b, i))),
                pl.BlockSpec((1, LANES, unit), lambda g, b, i: (g, 0, nxt(g, b, i)))]
    args = [qk, qk, qk, qk, vt, vt, vt, bias]
    if edge is None:
        in_specs.append(pl.BlockSpec((1, 3, 3 * unit, width), lambda g, b, i: (g, 0, 0, 0)))
    else:
        in_specs.append(pl.BlockSpec((1, 3 * unit, width), lambda g, b, i: (g, 0, 0)))
        in_specs.append(pl.BlockSpec((3, 3 * unit, width), lambda g, b, i: (0, 0, 0)))
        args.append(edge)
    if sink is not None:
        in_specs.append(pl.BlockSpec((1, SUBLANES, width), lambda g, b, i: (g, 0, 0)))
        args.append(sink)
    return pl.pallas_call(
        functools.partial(_attn_kernel, unit=unit, n_units=n_units, has_edge=edge is not None,
                          has_sink=sink is not None),
        name=name,
        grid=(n_groups, batch, steps),
        in_specs=in_specs,
        out_specs=pl.BlockSpec((ppg, qb, LANES), lambda g, b, i: (g, cur(g, b, i), 0)),
        out_shape=jax.ShapeDtypeStruct((N_PLANES, t, LANES), BF16),
        compiler_params=_cparams("arbitrary", "arbitrary", "arbitrary"),
    )(*args)


def _na_row_ok():
    a = np.arange(NA_UNIT) // GRID_W
    j = np.arange(3 * NA_UNIT) // GRID_W
    return np.stack([
        (j[None, :] >= a[:, None]) & (j[None, :] < a[:, None] + NA_KH),
        np.broadcast_to((j[None, :] >= 4) & (j[None, :] < 4 + NA_KH), (NA_UNIT, 3 * NA_UNIT)),
        np.broadcast_to(j[None, :] < NA_KH, (NA_UNIT, 3 * NA_UNIT)),
    ])


def _na_edge_mask():
    ok = _na_row_ok().transpose(0, 2, 1)
    mask = np.where(ok, np.float32(0.0), np.float32(NEG)).astype(np.float32)
    return jnp.asarray(np.concatenate([mask, mask], axis=2))


def _na_bias_table(rpb):
    w = np.arange(GRID_W)
    dc = np.clip(w[None, :] - w[:, None], -(NA_KW - 1), NA_KW - 1) + NA_KW - 1
    c0 = np.clip(w - NA_KW // 2, 0, GRID_W - NA_KW)
    col_ok = (w[None, :] >= c0[:, None]) & (w[None, :] < c0[:, None] + NA_KW)
    cols = jnp.take(rpb.astype(F32), jnp.asarray(dc.reshape(-1)), axis=2)
    cols = cols.reshape(N_HEADS, 2 * NA_KH - 1, GRID_W, GRID_W)
    cols = jnp.where(jnp.asarray(col_ok), cols, NEG)
    rows = jnp.stack([cols[:, 3 - ai:15 - ai] for ai in range(4)], axis=1)
    tbl = rows.transpose(0, 2, 4, 1, 3).reshape(N_HEADS, 3 * NA_UNIT, NA_UNIT)
    tbl = tbl.reshape(N_PLANES, 2, 3 * NA_UNIT, NA_UNIT).transpose(0, 2, 1, 3)
    return tbl.reshape(N_PLANES, 3 * NA_UNIT, 2 * NA_UNIT)


SW_HEAD_ORDER = np.array([8 * m + 4 * hh + i for m in range(2) for i in range(4) for hh in range(2)])


def _sw_bias_table():
    slopes = np.asarray(2.0 ** (-8.0 * np.arange(1, N_HEADS + 1) / N_HEADS), dtype=np.float32)
    ks = np.arange(3 * SW_UNIT) - SW_WINDOW
    dist = ks[None, :] - np.arange(SW_UNIT)[:, None]
    win_ok = np.abs(dist) <= SW_WINDOW
    pos_ok = np.stack([np.ones(3 * SW_UNIT, bool), ks >= 0, ks < SW_UNIT])
    ok = win_ok[None] & pos_ok[:, None, :]
    alibi = -slopes[SW_HEAD_ORDER][:, None, None] * np.abs(dist).astype(np.float32)[None]
    alibi = alibi * np.float32(LOG2E)
    full = np.where(ok[:, None], alibi[None], np.float32(NEG)).astype(np.float32)
    full = full.reshape(3, 2, 8, SW_UNIT, 3 * SW_UNIT).transpose(1, 0, 4, 2, 3)
    return jnp.asarray(full.reshape(2, 3, 3 * SW_UNIT, 8 * SW_UNIT))


def _post_kernel(o_ref, x_ref, wo_ref, g_ref, wrh_ref, wrhl_ref, x1_ref, h_ref, aff_ref, afft_ref):
    oc = jnp.concatenate([o_ref[p] for p in range(N_PLANES)], axis=1)
    x1 = x_ref[...] + jnp.dot(oc, wo_ref[...], preferred_element_type=F32)
    x1_ref[...] = x1
    h = _rmsnorm(x1, g_ref[...])
    hb = h.astype(BF16)
    h_ref[...] = hb
    hl = (h - hb.astype(F32)).astype(BF16)
    nt = (((1,), (1,)), ((), ()))
    by_hb = lax.dot_general(wrhl_ref[...], hb, nt, preferred_element_type=F32)
    by_hl = lax.dot_general(wrh_ref[...], hl, nt, preferred_element_type=F32)
    logits = by_hb[0:N_EXPERTS] + (by_hb[N_EXPERTS:2 * N_EXPERTS] + by_hl)
    ex = jnp.exp(logits - jnp.max(logits, axis=0, keepdims=True))
    afft = ex / jnp.sum(ex, axis=0, keepdims=True)
    afft_ref[...] = afft
    pad = jnp.zeros((LANES - N_EXPERTS, afft.shape[1]), F32)
    aff_ref[...] = jnp.concatenate([afft, pad], axis=0).T


def _post_call(o, x, wo, g, wrh, wrhl):
    t, d = x.shape
    tm = min(DENSE_ROW_TILE, t)
    return pl.pallas_call(
        _post_kernel,
        name="post",
        grid=(t // tm,),
        in_specs=[pl.BlockSpec((N_PLANES, tm, LANES), lambda i: (0, i, 0)),
                  pl.BlockSpec((tm, d), lambda i: (i, 0)),
                  pl.BlockSpec((d, d), lambda i: (0, 0)),
                  pl.BlockSpec((1, d), lambda i: (0, 0)),
                  pl.BlockSpec((N_EXPERTS, d), lambda i: (0, 0)),
                  pl.BlockSpec((2 * N_EXPERTS, d), lambda i: (0, 0))],
        out_specs=[pl.BlockSpec((tm, d), lambda i: (i, 0)),
                   pl.BlockSpec((tm, d), lambda i: (i, 0)),
                   pl.BlockSpec((tm, LANES), lambda i: (i, 0)),
                   pl.BlockSpec((N_EXPERTS, tm), lambda i: (0, i))],
        out_shape=[jax.ShapeDtypeStruct((t, d), F32),
                   jax.ShapeDtypeStruct((t, d), BF16),
                   jax.ShapeDtypeStruct((t, LANES), F32),
                   jax.ShapeDtypeStruct((N_EXPERTS, t), F32)],
        compiler_params=_cparams("parallel"),
    )(o, x, wo, g, wrh, wrhl)


def _threshold_kernel(aff_ref, thr_ref, need_ref, *, cap):
    bits = lax.bitcast_convert_type(aff_ref[...], I32)

    def body(k, cur):
        cand = cur | jnp.left_shift(jnp.int32(1), 30 - k)
        cnt = jnp.sum(jnp.where(bits >= cand, 1, 0), axis=1, keepdims=True)
        return jnp.where(cnt >= cap, cand, cur)

    cur = lax.fori_loop(0, 31, body, jnp.zeros((N_EXPERTS, 1), I32))
    above = jnp.sum(jnp.where(bits > cur, 1, 0), axis=1, keepdims=True)
    thr_ref[...] = jnp.broadcast_to(cur, thr_ref.shape)
    need_ref[...] = jnp.broadcast_to(cap - above, need_ref.shape)


def _threshold_call(afft, cap):
    e, t = afft.shape
    return pl.pallas_call(
        functools.partial(_threshold_kernel, cap=cap),
        name="threshold",
        grid=(1,),
        in_specs=[pl.BlockSpec((e, t), lambda i: (0, 0))],
        out_specs=[pl.BlockSpec((e, LANES), lambda i: (0, 0)),
                   pl.BlockSpec((e, LANES), lambda i: (0, 0))],
        out_shape=[jax.ShapeDtypeStruct((e, LANES), I32),
                   jax.ShapeDtypeStruct((e, LANES), I32)],
        compiler_params=_cparams("arbitrary"),
    )(afft)


def _positions_kernel(aff_ref, thr_ref, need_ref, lpos_ref, lpost_ref, starts_ref, npass_ref, base_sc, eqb_sc):
    @pl.when(pl.program_id(0) == 0)
    def _():
        base_sc[...] = jnp.zeros_like(base_sc)
        eqb_sc[...] = jnp.zeros_like(eqb_sc)

    thr = thr_ref[:, 0:1]
    need = need_ref[:, 0:1].astype(F32)
    r = lax.broadcasted_iota(I32, (ROUTE_TILE, ROUTE_TILE), 0)
    c = lax.broadcasted_iota(I32, (ROUTE_TILE, ROUTE_TILE), 1)
    before = jnp.where(r < c, 1.0, 0.0).astype(BF16)
    pad = jnp.full((LANES - N_EXPERTS, ROUTE_TILE), -1.0, F32)
    base = base_sc[...]
    eqb = eqb_sc[...]
    for k in range(starts_ref.shape[0]):
        cols = slice(k * ROUTE_TILE, (k + 1) * ROUTE_TILE)
        bits = lax.bitcast_convert_type(aff_ref[:, cols], I32)
        eq = bits == thr
        eqf = jnp.where(eq, 1.0, 0.0)
        eq_rank = eqb[:, 0:1] + jnp.dot(eqf.astype(BF16), before, preferred_element_type=F32)
        sel = (bits > thr) | (eq & (eq_rank < need))
        self_ = jnp.where(sel, 1.0, 0.0)
        rank = jnp.dot(self_.astype(BF16), before, preferred_element_type=F32)
        cnt = jnp.sum(self_, axis=1, keepdims=True)
        lpos = jnp.where(sel, rank, -1.0)
        lpos_ref[:, cols] = lpos.astype(I32)
        lpost_ref[cols, :] = jnp.concatenate([lpos, pad], axis=0).T
        starts_ref[k] = base.astype(I32)
        most = jnp.max(cnt, axis=0, keepdims=True)
        passes = sum(jnp.where(most > float(first), 1.0, 0.0) for first in range(0, ROUTE_TILE, SLAB))
        npass_ref[k] = jnp.broadcast_to(passes, npass_ref.shape[1:]).astype(I32)
        base = base + cnt
        eqb = eqb + jnp.sum(eqf, axis=1, keepdims=True)
    base_sc[...] = base
    eqb_sc[...] = eqb


def _positions_call(afft, thr, need):
    e, t = afft.shape
    n_tiles = t // ROUTE_TILE
    per_step = min(POSITION_TILES_PER_STEP, n_tiles)
    assert n_tiles % per_step == 0
    span = per_step * ROUTE_TILE
    return pl.pallas_call(
        _positions_kernel,
        name="positions",
        grid=(n_tiles // per_step,),
        in_specs=[pl.BlockSpec((e, span), lambda i: (0, i)),
                  pl.BlockSpec((e, LANES), lambda i: (0, 0)),
                  pl.BlockSpec((e, LANES), lambda i: (0, 0))],
        out_specs=[pl.BlockSpec((e, span), lambda i: (0, i)),
                   pl.BlockSpec((span, LANES), lambda i: (i, 0)),
                   pl.BlockSpec((per_step, e, LANES), lambda i: (i, 0, 0)),
                   pl.BlockSpec((per_step, SUBLANES, LANES), lambda i: (i, 0, 0))],
        out_shape=[jax.ShapeDtypeStruct((e, t), I32),
                   jax.ShapeDtypeStruct((t, LANES), F32),
                   jax.ShapeDtypeStruct((n_tiles, e, LANES), I32),
                   jax.ShapeDtypeStruct((n_tiles, SUBLANES, LANES), I32)],
        scratch_shapes=[pltpu.VMEM((e, LANES), F32), pltpu.VMEM((e, LANES), F32)],
        compiler_params=_cparams("arbitrary"),
    )(afft, thr, need)


SLAB_ROWS = SLAB * COL_BLOCKS
READ_SLAB = SLAB + SUBLANES


def _dispatch_kernel(starts_sm, npass_sm, h_ref, lpos_ref, xe_ref, slab_sc, sem, state_sm, *, cap):
    i = pl.program_id(0)
    sub = lax.broadcasted_iota(I32, (SLAB, ROUTE_TILE), 0)

    def slab_copy(buf, e, dst):
        return pltpu.make_async_copy(
            slab_sc.at[buf, pl.ds(e * SLAB_ROWS, SLAB_ROWS)],
            xe_ref.at[e, pl.ds(pl.multiple_of(dst * COL_BLOCKS, COL_BLOCKS), SLAB_ROWS)],
            sem.at[buf, e])

    def wait_all(buf):
        for e in range(N_EXPERTS):
            slab_copy(buf, e, 0).wait()

    @pl.when(i == 0)
    def _():
        slab_sc[0, pl.ds(0, SLAB_ROWS), :] = jnp.zeros((SLAB_ROWS, LANES), F32)
        pads = [pltpu.make_async_copy(slab_sc.at[0, pl.ds(0, SLAB_ROWS)],
                                      xe_ref.at[e, pl.ds(cap * COL_BLOCKS, SLAB_ROWS)], sem.at[0, e])
                for e in range(N_EXPERTS)]
        for cp in pads:
            cp.start()
        for cp in pads:
            cp.wait()
        state_sm[0] = 0
        state_sm[1] = 0

    def one_pass(s, carry):
        buf = state_sm[1]
        rows = []
        for e in range(N_EXPERTS):
            lp = lpos_ref[e:e + 1, :] - SLAB * s
            rows.append(jnp.where(lp == sub, 1.0, 0.0).astype(BF16))
        onehot = jnp.concatenate(rows, axis=0)
        res = jnp.dot(onehot, h_ref[...], preferred_element_type=F32)
        for cb in range(COL_BLOCKS):
            slab_sc[buf, pl.ds(cb, N_EXPERTS * SLAB, stride=COL_BLOCKS), :] = res[:, cb * LANES:(cb + 1) * LANES]

        @pl.when(state_sm[0] == 1)
        def _():
            wait_all(1 - buf)

        for e in range(N_EXPERTS):
            dst = jnp.minimum(starts_sm[i * N_EXPERTS + e] + SLAB * s, cap)
            slab_copy(buf, e, dst).start()
        state_sm[0] = 1
        state_sm[1] = 1 - buf
        return carry

    lax.fori_loop(0, npass_sm[i], one_pass, 0)

    @pl.when((i == pl.num_programs(0) - 1) & (state_sm[0] == 1))
    def _():
        wait_all(1 - state_sm[1])


def _dispatch_call(starts, npass, h, lpos, cap):
    t, d = h.shape
    n_tiles = t // ROUTE_TILE
    grid_spec = pltpu.PrefetchScalarGridSpec(
        num_scalar_prefetch=2,
        grid=(n_tiles,),
        in_specs=[pl.BlockSpec((ROUTE_TILE, d), lambda i, *_: (i, 0)),
                  pl.BlockSpec((N_EXPERTS, ROUTE_TILE), lambda i, *_: (0, i))],
        out_specs=pl.BlockSpec(memory_space=pl.ANY),
        scratch_shapes=[pltpu.VMEM((2, N_EXPERTS * SLAB_ROWS, LANES), F32),
                        pltpu.SemaphoreType.DMA((2, N_EXPERTS)),
                        pltpu.SMEM((2,), I32)],
    )
    return pl.pallas_call(
        functools.partial(_dispatch_kernel, cap=cap),
        name="dispatch",
        grid_spec=grid_spec,
        out_shape=jax.ShapeDtypeStruct((N_EXPERTS, (cap + SLAB) * COL_BLOCKS, LANES), F32),
        compiler_params=_cparams("arbitrary"),
    )(starts, npass, h, lpos)


def _expert_kernel(x_ref, wg_hbm, wu_hbm, wd_hbm, o_ref, wg0, wu0, wd0, wg1, wu1, wd1, sg, su, sd, sem,
                   *, layer, n_chunks):
    e = pl.program_id(0)
    j = pl.program_id(1)
    n_experts = pl.num_programs(0)
    d_rows = D_MODEL // n_chunks
    f_rows = EXPERT_FF // n_chunks
    sets = ((wg0, wu0, wd0), (wg1, wu1, wd1))

    def chunk_copies(ex, c):
        slot = (ex * n_chunks + c) % 2
        return [pltpu.make_async_copy(wg_hbm.at[layer, ex, pl.ds(c * d_rows, d_rows)], sg.at[slot], sem.at[0, slot]),
                pltpu.make_async_copy(wu_hbm.at[layer, ex, pl.ds(c * d_rows, d_rows)], su.at[slot], sem.at[1, slot]),
                pltpu.make_async_copy(wd_hbm.at[layer, ex, pl.ds(c * f_rows, f_rows)], sd.at[slot], sem.at[2, slot])]

    def start(ex, c):
        for cp in chunk_copies(ex, c):
            cp.start()

    def finish(ex, c, into):
        for cp in chunk_copies(ex, c):
            cp.wait()
        slot = (ex * n_chunks + c) % 2
        wg, wu, wd = into
        wg[pl.ds(c * d_rows, d_rows), :] = sg[slot].astype(BF16)
        wu[pl.ds(c * d_rows, d_rows), :] = su[slot].astype(BF16)
        wd[pl.ds(c * f_rows, f_rows), :] = sd[slot].astype(BF16)

    @pl.when((e == 0) & (j == 0))
    def _():
        for c in range(n_chunks):
            start(0, c)
            finish(0, c, sets[0])
        start(1, 0)

    nxt = (e + 1) % n_experts

    @pl.when(j + 1 < n_chunks)
    def _():
        start(nxt, j + 1)

    @pl.when((j + 1 == n_chunks) & (e + 1 < n_experts))
    def _():
        start((e + 2) % n_experts, 0)

    def step(cur, other):
        finish(nxt, j, other)
        wg, wu, wd = cur
        tm = x_ref.shape[1] // COL_BLOCKS
        x = jnp.concatenate([x_ref[0, pl.ds(cb, tm, stride=COL_BLOCKS), :] for cb in range(COL_BLOCKS)],
                            axis=1).astype(BF16)
        acc = jnp.zeros((tm, D_MODEL), F32)
        for fc in range(EXPERT_FF // FF_CHUNK):
            fs = slice(fc * FF_CHUNK, (fc + 1) * FF_CHUNK)
            g = jnp.dot(x, wg[:, fs], preferred_element_type=F32)
            u = jnp.dot(x, wu[:, fs], preferred_element_type=F32)
            hmid = (g * (1.0 / (1.0 + jnp.exp(-g))) * u).astype(BF16)
            acc = acc + jnp.dot(hmid, wd[fs, :], preferred_element_type=F32)
        for cb in range(COL_BLOCKS):
            o_ref[0, cb] = acc[:, cb * LANES:(cb + 1) * LANES]

    @pl.when(e % 2 == 0)
    def _():
        step(sets[0], sets[1])

    @pl.when(e % 2 == 1)
    def _():
        step(sets[1], sets[0])


def _expert_call(xe, wg, wu, wd, layer, cap):
    tm = min(ROW_TILE, cap)
    n_chunks = cap // tm
    d, f = wg.shape[2], wg.shape[3]
    assert d % n_chunks == 0 and f % n_chunks == 0
    return pl.pallas_call(
        functools.partial(_expert_kernel, layer=layer, n_chunks=n_chunks),
        name="expert",
        grid=(N_EXPERTS, n_chunks),
        in_specs=[pl.BlockSpec((1, tm * COL_BLOCKS, LANES), lambda e, j: (e, j, 0)),
                  pl.BlockSpec(memory_space=pl.ANY),
                  pl.BlockSpec(memory_space=pl.ANY),
                  pl.BlockSpec(memory_space=pl.ANY)],
        out_specs=pl.BlockSpec((1, COL_BLOCKS, tm, LANES), lambda e, j: (e, 0, j, 0)),
        out_shape=jax.ShapeDtypeStruct((N_EXPERTS, COL_BLOCKS, cap, LANES), F32),
        scratch_shapes=[pltpu.VMEM((d, f), BF16), pltpu.VMEM((d, f), BF16), pltpu.VMEM((f, d), BF16),
                        pltpu.VMEM((d, f), BF16), pltpu.VMEM((d, f), BF16), pltpu.VMEM((f, d), BF16),
                        pltpu.VMEM((2, d // n_chunks, f), F32), pltpu.VMEM((2, d // n_chunks, f), F32),
                        pltpu.VMEM((2, f // n_chunks, d), F32),
                        pltpu.SemaphoreType.DMA((3, 2))],
        compiler_params=_cparams("arbitrary", "arbitrary"),
    )(xe, wg, wu, wd)


def _combine_kernel(starts_sm, npass_sm, x_ref, aff_ref, lpost_ref, expand_ref, g_ref, ye_ref, o_ref,
                    slab_sc, sem, *, cap, final):
    i = pl.program_id(0)
    lane = lax.broadcasted_iota(I32, (1, LANES), 1)
    slot_in_slab = (lax.broadcasted_iota(I32, (1, N_EXPERTS * READ_SLAB), 1) % READ_SLAB).astype(F32)
    aff = aff_ref[...]
    ghi = aff.astype(BF16)
    glo = (aff - ghi.astype(F32)).astype(BF16)
    expand = expand_ref[...]
    lpost = lpost_ref[...]
    buf = i % 2

    def slab_src(tile, s, e):
        want = starts_sm[tile * N_EXPERTS + e] + SLAB * s
        src = jnp.minimum((want // SUBLANES) * SUBLANES, cap - READ_SLAB)
        return want, pl.multiple_of(src, SUBLANES)

    def slab_copy(tile, s, to, e):
        src = slab_src(tile, s, e)[1]
        return pltpu.make_async_copy(
            ye_ref.at[e, :, pl.ds(src, READ_SLAB), :],
            slab_sc.at[to, :, pl.ds(e * READ_SLAB, READ_SLAB), :],
            sem.at[to, e])

    def fetch(tile, s, to):
        for e in range(N_EXPERTS):
            slab_copy(tile, s, to, e).start()

    def slab_row(s):
        shift = jnp.zeros((1, LANES), F32)
        for e in range(N_EXPERTS):
            want, src = slab_src(i, s, e)
            shift = jnp.where(lane == e, (want - src).astype(F32), shift)
        lo = jnp.asarray(SLAB * s, F32)
        owned = (lpost >= lo) & (lpost < lo + SLAB)
        return jnp.where(owned, lpost - lo + shift, -1.0).astype(BF16)

    spread = jnp.dot(jnp.concatenate([ghi, glo, slab_row(0)], axis=0), expand, preferred_element_type=F32)
    ghi_x = spread[0:ROUTE_TILE]
    glo_x = spread[ROUTE_TILE:2 * ROUTE_TILE]

    def add_pass(s, row_x, acc):
        match = row_x == slot_in_slab
        w = jnp.concatenate([jnp.where(match, ghi_x, 0.0).astype(BF16),
                             jnp.where(match, glo_x, 0.0).astype(BF16)], axis=0)
        for e in range(N_EXPERTS):
            slab_copy(i, s, buf, e).wait()
        rows = jnp.concatenate([slab_sc[buf, cb] for cb in range(COL_BLOCKS)],
                               axis=1).astype(BF16)
        both = jnp.dot(w, rows, preferred_element_type=F32)
        return acc + (both[0:ROUTE_TILE] + both[ROUTE_TILE:2 * ROUTE_TILE])

    @pl.when(i == 0)
    def _():
        fetch(0, 0, 0)

    @pl.when(i + 1 < pl.num_programs(0))
    def _():
        fetch(i + 1, 0, 1 - buf)

    y = add_pass(0, spread[2 * ROUTE_TILE:3 * ROUTE_TILE], jnp.zeros(x_ref.shape, F32))

    def extra_pass(s, acc):
        fetch(i, s, buf)
        return add_pass(s, jnp.dot(slab_row(s), expand, preferred_element_type=F32), acc)

    y = lax.fori_loop(1, npass_sm[i], extra_pass, y)
    out = x_ref[...] + y
    if final:
        out = _rmsnorm(out, g_ref[...])
    o_ref[...] = out


def _combine_call(starts, npass, x1, aff, lpost, expand, g, ye, cap, final):
    t, d = x1.shape
    n_tiles = t // ROUTE_TILE
    grid_spec = pltpu.PrefetchScalarGridSpec(
        num_scalar_prefetch=2,
        grid=(n_tiles,),
        in_specs=[pl.BlockSpec((ROUTE_TILE, d), lambda i, *_: (i, 0)),
                  pl.BlockSpec((ROUTE_TILE, LANES), lambda i, *_: (i, 0)),
                  pl.BlockSpec((ROUTE_TILE, LANES), lambda i, *_: (i, 0)),
                  pl.BlockSpec((LANES, N_EXPERTS * READ_SLAB), lambda i, *_: (0, 0)),
                  pl.BlockSpec((1, d), lambda i, *_: (0, 0)),
                  pl.BlockSpec(memory_space=pl.ANY)],
        out_specs=pl.BlockSpec((ROUTE_TILE, d), lambda i, *_: (i, 0)),
        scratch_shapes=[pltpu.VMEM((2, COL_BLOCKS, N_EXPERTS * READ_SLAB, LANES), F32),
                        pltpu.SemaphoreType.DMA((2, N_EXPERTS))],
    )
    return pl.pallas_call(
        functools.partial(_combine_kernel, cap=cap, final=final),
        name="combine",
        grid_spec=grid_spec,
        out_shape=jax.ShapeDtypeStruct((t, d), F32),
        compiler_params=_cparams("arbitrary"),
    )(starts, npass, x1, aff, lpost, expand, g, ye)


def _moe(x1, h, aff, afft, wg, wu, wd, layer, expand, g_final, final):
    t = x1.shape[0]
    cap = 2 * t // N_EXPERTS
    assert t % ROUTE_TILE == 0 and cap >= READ_SLAB and cap % SUBLANES == 0
    thr, need = _threshold_call(afft, cap)
    lpos, lpost, starts3, npass3 = _positions_call(afft, thr, need)
    starts = starts3[:, :, 0].reshape(-1)
    npass = npass3[:, 0, 0]
    xe = _dispatch_call(starts, npass, h, lpos, cap)
    ye = _expert_call(xe, wg, wu, wd, layer, cap)
    return _combine_call(starts, npass, x1, aff, lpost, expand, g_final, ye, cap, final)


def _trunk(x, p):
    batch, seq, d = x.shape
    x = x.reshape(batch * seq, d)
    depth = p["norm_mix"].shape[0]
    for i in range(depth):
        j = i // 2
        if i % 2 == 0:
            qk, vt = _qkv_call(x, p["norm_mix"][i:i + 1], p["a_wqkv"][j], N_PLANES)
            o = _attn_call("na_attn", qk, vt, p["a_bias"][j], p["a_edge"], None, batch, seq,
                           NA_UNIT, NA_UNITS_PER_STEP, N_PLANES)
            wo = p["a_wo"][j]
        else:
            qk, vt = _qkv_call(x, p["norm_mix"][i:i + 1], p["b_wqkv"][j], 2)
            o = _attn_call("sw_attn", qk, vt, p["b_bias"], None, p["b_sink"][j], batch, seq,
                           SW_UNIT, SW_UNITS_PER_STEP, 2)
            wo = p["b_wo"][j]
        x1, h, aff, afft = _post_call(o, x, wo, p["norm_ffn"][i:i + 1], p["wr_hi"][i], p["wr_hl"][i])
        x = _moe(x1, h, aff, afft, p["w_gate"], p["w_up"], p["w_down"], i,
                 p["expand"], p["norm_final"], i == depth - 1)
    return x.reshape(batch, seq, d)


def _prepare(norm_mix, norm_ffn, norm_final, a_wqkv, a_wo, a_rpb, b_wqkv, b_wo, b_sink,
             w_router, w_gate, w_up, w_down):
    d = D_MODEL
    scale = HEAD_DIM ** -0.5 * LOG2E
    a_w = jnp.concatenate([a_wqkv[:, :, :d] * scale, a_wqkv[:, :, d:]], axis=2).astype(BF16)
    order = SW_HEAD_ORDER
    nb = b_wqkv.shape[0]
    bq = (b_wqkv[:, :, :d] * scale).reshape(nb, d, N_HEADS, HEAD_DIM)[:, :, order].reshape(nb, d, d)
    b_w = jnp.concatenate([bq, b_wqkv[:, :, d:]], axis=2).astype(BF16)
    b_wo_p = b_wo.reshape(nb, N_HEADS, HEAD_DIM, d)[:, order].reshape(nb, d, d).astype(BF16)
    sink = (b_sink.astype(F32) * LOG2E)[:, order].reshape(nb, 2, 1, 8, 1)
    sink = jnp.broadcast_to(sink, (nb, 2, SUBLANES, 8, SW_UNIT)).reshape(nb, 2, SUBLANES, 8 * SW_UNIT)
    wr = w_router.astype(F32).transpose(0, 2, 1)
    wr_hi = wr.astype(BF16)
    wr_lo = (wr - wr_hi.astype(F32)).astype(BF16)
    wr_hl = jnp.concatenate([wr_hi, wr_lo], axis=1)
    lane_expert = np.arange(N_EXPERTS * READ_SLAB) // READ_SLAB
    expand = jnp.asarray(np.arange(LANES)[:, None] == lane_expert[None, :], dtype=BF16)
    return {
        "norm_mix": norm_mix.astype(F32), "norm_ffn": norm_ffn.astype(F32),
        "norm_final": norm_final.astype(F32).reshape(1, d),
        "a_wqkv": a_w, "a_wo": a_wo.astype(BF16),
        "a_bias": jnp.stack([_na_bias_table(a_rpb[j] * LOG2E) for j in range(a_rpb.shape[0])]),
        "a_edge": _na_edge_mask(),
        "b_wqkv": b_w, "b_wo": b_wo_p, "b_bias": _sw_bias_table(), "b_sink": sink,
        "wr_hi": wr_hi, "wr_hl": wr_hl,
        "w_gate": w_gate.astype(F32), "w_up": w_up.astype(F32), "w_down": w_down.astype(F32),
        "expand": expand,
    }


def kernel(x_prompt, x_sample, norm_mix, norm_ffn, norm_final, a_wqkv, a_wo, a_rpb, b_wqkv, b_wo, b_sink,
           w_router, w_gate, w_up, w_down):
    p = _prepare(norm_mix, norm_ffn, norm_final, a_wqkv, a_wo, a_rpb, b_wqkv, b_wo, b_sink,
                 w_router, w_gate, w_up, w_down)
    return (_trunk(x_prompt, p), _trunk(x_sample, p))
```

```python
import functools

import numpy as np
import jax
import jax.numpy as jnp
from jax import lax
from jax.experimental import pallas as pl
from jax.experimental.pallas import tpu as pltpu

F32 = jnp.float32
BF16 = jnp.bfloat16
I32 = jnp.int32

D_MODEL = 1024
HEAD_DIM = 64
N_HEADS = 16
LANES = 128
SUBLANES = 8
N_PLANES = D_MODEL // LANES
COL_BLOCKS = D_MODEL // LANES
GRID_W = 64
NA_KH = 8
NA_KW = 16
NA_UNIT = 4 * GRID_W
NA_UNITS_PER_STEP = 8
SW_KV_HEADS = 4
SW_WINDOW = 128
SW_UNIT = 128
SW_UNITS_PER_STEP = 8
N_EXPERTS = 16
EXPERT_FF = 2048
FF_CHUNK = 512
ROUTE_TILE = 256
POSITION_TILES_PER_STEP = 8
ROUTE_TILES_PER_STEP = 4
SLAB = 56
RMS_EPS = 1e-6
NEG = -1e30
LOG2E = 1.4426950408889634
ROW_TILE = 512
DENSE_ROW_TILE = 1024
VMEM_LIMIT = 56 * 1024 * 1024


def _cparams(*sem):
    return pltpu.CompilerParams(dimension_semantics=sem, vmem_limit_bytes=VMEM_LIMIT)


def _rmsnorm(x, g):
    return x * lax.rsqrt(jnp.mean(x * x, axis=-1, keepdims=True) + RMS_EPS) * g


def _qkv_kernel(x_ref, g_ref, w_ref, qk_ref, vt_ref):
    y = _rmsnorm(x_ref[...], g_ref[...]).astype(BF16)
    r = jnp.dot(y, w_ref[...], preferred_element_type=F32)
    n_qk = qk_ref.shape[0]
    for p in range(n_qk):
        qk_ref[p] = r[:, p * LANES:(p + 1) * LANES].astype(BF16)
    for p in range(vt_ref.shape[0]):
        vt_ref[p] = r[:, (n_qk + p) * LANES:(n_qk + p + 1) * LANES].T.astype(BF16)


def _qkv_call(x, g, w, n_v_planes):
    t, d = x.shape
    n_qk = w.shape[1] // LANES - n_v_planes
    tm = min(DENSE_ROW_TILE, t)
    return pl.pallas_call(
        _qkv_kernel,
        name="qkv",
        grid=(t // tm,),
        in_specs=[pl.BlockSpec((tm, d), lambda i: (i, 0)),
                  pl.BlockSpec((1, d), lambda i: (0, 0)),
                  pl.BlockSpec((d, w.shape[1]), lambda i: (0, 0))],
        out_specs=[pl.BlockSpec((n_qk, tm, LANES), lambda i: (0, i, 0)),
                   pl.BlockSpec((n_v_planes, LANES, tm), lambda i: (0, 0, i))],
        out_shape=[jax.ShapeDtypeStruct((n_qk, t, LANES), BF16),
                   jax.ShapeDtypeStruct((n_v_planes, LANES, t), BF16)],
        compiler_params=_cparams("parallel"),
    )(x, g, w)


def _edge_case(u, n_units):
    i = pl.program_id(2)
    last = pl.num_programs(2) - 1
    if n_units == 1:
        return jnp.where(i == 0, 1, jnp.where(i == last, 2, 0))
    if u == 0:
        return jnp.where(i == 0, 1, 0)
    if u == n_units - 1:
        return jnp.where(i == last, 2, 0)
    return 0


def _attn_kernel(*refs, unit, n_units, has_edge, has_sink):
    q_ref, kp_ref, kc_ref, kn_ref, vp_ref, vc_ref, vn_ref, bias_ref = refs[:8]
    edge_ref = refs[8] if has_edge else None
    sink_ref = refs[8 + has_edge] if has_sink else None
    o_ref = refs[-1]
    n_planes = q_ref.shape[0]
    kcat = jnp.concatenate([kp_ref[0], kc_ref[0], kn_ref[0]], axis=0)
    vcat = jnp.concatenate([vp_ref[0], vc_ref[0], vn_ref[0]], axis=1)
    vcat = jnp.concatenate([vcat, jnp.ones((SUBLANES, vcat.shape[1]), BF16)], axis=0)
    lo_half = lax.broadcasted_iota(I32, (unit, LANES), 1) < HEAD_DIM
    zero = jnp.zeros((unit, LANES), BF16)
    def scores(u):
        kw = kcat[u * unit:(u + 3) * unit]
        qs = [q_ref[pi, u * unit:(u + 1) * unit, :] for pi in range(n_planes)]
        q_stack = jnp.concatenate(
            [jnp.where(lo_half if hh == 0 else jnp.logical_not(lo_half), q, zero) for q in qs for hh in range(2)],
            axis=0)
        return lax.dot_general(kw, q_stack, (((1,), (1,)), ((), ())), preferred_element_type=F32)

    st_next = scores(0)
    for u in range(n_units):
        case = _edge_case(u, n_units)
        vw = vcat[:, u * unit:(u + 3) * unit]
        if has_edge:
            st = st_next + bias_ref[0] + edge_ref[case]
        else:
            st = st_next + bias_ref[0, case]
        if u + 1 < n_units:
            st_next = scores(u + 1)
        m = jnp.max(st, axis=0, keepdims=True)
        if has_sink:
            sink = sink_ref[0, 0:1, :]
            m = jnp.maximum(m, sink)
        e = jnp.exp2(st - m)
        ot = jnp.dot(vw, e.astype(BF16), preferred_element_type=F32)
        l = ot[LANES:LANES + 1, :]
        if has_sink:
            l = l + jnp.exp2(sink - m)
        ot = ot[0:LANES, :] / l
        for pi in range(n_planes):
            both = jnp.concatenate(
                [ot[0:HEAD_DIM, (2 * pi) * unit:(2 * pi + 1) * unit],
                 ot[HEAD_DIM:LANES, (2 * pi + 1) * unit:(2 * pi + 2) * unit]], axis=0)
            o_ref[pi, u * unit:(u + 1) * unit, :] = both.T.astype(BF16)


def _attn_call(name, qk, vt, bias, edge, sink, batch, seq, unit, n_units, n_groups):
    t = batch * seq
    units = seq // unit
    n_units = min(n_units, units)
    steps = units // n_units
    assert seq % (unit * n_units) == 0 and units >= 4
    ppg = N_PLANES // n_groups
    qb = unit * n_units
    width = 2 * ppg * unit

    def cur(g, b, i):
        return b * steps + i

    def prev(g, b, i):
        return b * units + jnp.maximum(n_units * i - 1, 0)

    def nxt(g, b, i):
        return b * units + jnp.minimum(n_units * i + n_units, units - 1)

    in_specs = [pl.BlockSpec((ppg, qb, LANES), lambda g, b, i: (g, cur(g, b, i), 0)),
                pl.BlockSpec((1, unit, LANES), lambda g, b, i: (N_PLANES + g, prev(g, b, i), 0)),
                pl.BlockSpec((1, qb, LANES), lambda g, b, i: (N_PLANES + g, cur(g, b, i), 0)),
                pl.BlockSpec((1, unit, LANES), lambda g, b, i: (N_PLANES + g, nxt(g, b, i), 0)),
                pl.BlockSpec((1, LANES, unit), lambda g, b, i: (g, 0, prev(g, b, i))),
                pl.BlockSpec((1, LANES, qb), lambda g, b, i: (g, 0, cur(g, b, i))),
                pl.BlockSpec((1, LANES, unit), lambda g, b, i: (g, 0, nxt(g, b, i)))]
    args = [qk, qk, qk, qk, vt, vt, vt, bias]
    if edge is None:
        in_specs.append(pl.BlockSpec((1, 3, 3 * unit, width), lambda g, b, i: (g, 0, 0, 0)))
    else:
        in_specs.append(pl.BlockSpec((1, 3 * unit, width), lambda g, b, i: (g, 0, 0)))
        in_specs.append(pl.BlockSpec((3, 3 * unit, width), lambda g, b, i: (0, 0, 0)))
        args.append(edge)
    if sink is not None:
        in_specs.append(pl.BlockSpec((1, SUBLANES, width), lambda g, b, i: (g, 0, 0)))
        args.append(sink)
    return pl.pallas_call(
        functools.partial(_attn_kernel, unit=unit, n_units=n_units, has_edge=edge is not None,
                          has_sink=sink is not None),
        name=name,
        grid=(n_groups, batch, steps),
        in_specs=in_specs,
        out_specs=pl.BlockSpec((ppg, qb, LANES), lambda g, b, i: (g, cur(g, b, i), 0)),
        out_shape=jax.ShapeDtypeStruct((N_PLANES, t, LANES), BF16),
        compiler_params=_cparams("arbitrary", "arbitrary", "arbitrary"),
    )(*args)


def _na_row_ok():
    a = np.arange(NA_UNIT) // GRID_W
    j = np.arange(3 * NA_UNIT) // GRID_W
    return np.stack([
        (j[None, :] >= a[:, None]) & (j[None, :] < a[:, None] + NA_KH),
        np.broadcast_to((j[None, :] >= 4) & (j[None, :] < 4 + NA_KH), (NA_UNIT, 3 * NA_UNIT)),
        np.broadcast_to(j[None, :] < NA_KH, (NA_UNIT, 3 * NA_UNIT)),
    ])


def _na_edge_mask():
    ok = _na_row_ok().transpose(0, 2, 1)
    mask = np.where(ok, np.float32(0.0), np.float32(NEG)).astype(np.float32)
    return jnp.asarray(np.concatenate([mask, mask], axis=2))


def _na_bias_table(rpb):
    w = np.arange(GRID_W)
    dc = np.clip(w[None, :] - w[:, None], -(NA_KW - 1), NA_KW - 1) + NA_KW - 1
    c0 = np.clip(w - NA_KW // 2, 0, GRID_W - NA_KW)
    col_ok = (w[None, :] >= c0[:, None]) & (w[None, :] < c0[:, None] + NA_KW)
    cols = jnp.take(rpb.astype(F32), jnp.asarray(dc.reshape(-1)), axis=2)
    cols = cols.reshape(N_HEADS, 2 * NA_KH - 1, GRID_W, GRID_W)
    cols = jnp.where(jnp.asarray(col_ok), cols, NEG)
    rows = jnp.stack([cols[:, 3 - ai:15 - ai] for ai in range(4)], axis=1)
    tbl = rows.transpose(0, 2, 4, 1, 3).reshape(N_HEADS, 3 * NA_UNIT, NA_UNIT)
    tbl = tbl.reshape(N_PLANES, 2, 3 * NA_UNIT, NA_UNIT).transpose(0, 2, 1, 3)
    return tbl.reshape(N_PLANES, 3 * NA_UNIT, 2 * NA_UNIT)


SW_HEAD_ORDER = np.array([8 * m + 4 * hh + i for m in range(2) for i in range(4) for hh in range(2)])


def _sw_bias_table():
    slopes = np.asarray(2.0 ** (-8.0 * np.arange(1, N_HEADS + 1) / N_HEADS), dtype=np.float32)
    ks = np.arange(3 * SW_UNIT) - SW_WINDOW
    dist = ks[None, :] - np.arange(SW_UNIT)[:, None]
    win_ok = np.abs(dist) <= SW_WINDOW
    pos_ok = np.stack([np.ones(3 * SW_UNIT, bool), ks >= 0, ks < SW_UNIT])
    ok = win_ok[None] & pos_ok[:, None, :]
    alibi = -slopes[SW_HEAD_ORDER][:, None, None] * np.abs(dist).astype(np.float32)[None]
    alibi = alibi * np.float32(LOG2E)
    full = np.where(ok[:, None], alibi[None], np.float32(NEG)).astype(np.float32)
    full = full.reshape(3, 2, 8, SW_UNIT, 3 * SW_UNIT).transpose(1, 0, 4, 2, 3)
    return jnp.asarray(full.reshape(2, 3, 3 * SW_UNIT, 8 * SW_UNIT))


def _post_kernel(o_ref, x_ref, wo_ref, g_ref, wrh_ref, wrhl_ref, x1_ref, h_ref, aff_ref, afft_ref):
    oc = jnp.concatenate([o_ref[p] for p in range(N_PLANES)], axis=1)
    x1 = x_ref[...] + jnp.dot(oc, wo_ref[...], preferred_element_type=F32)
    x1_ref[...] = x1
    h = _rmsnorm(x1, g_ref[...])
    hb = h.astype(BF16)
    h_ref[...] = hb
    hl = (h - hb.astype(F32)).astype(BF16)
    nt = (((1,), (1,)), ((), ()))
    by_hb = lax.dot_general(wrhl_ref[...], hb, nt, preferred_element_type=F32)
    by_hl = lax.dot_general(wrh_ref[...], hl, nt, preferred_element_type=F32)
    logits = by_hb[0:N_EXPERTS] + (by_hb[N_EXPERTS:2 * N_EXPERTS] + by_hl)
    ex = jnp.exp(logits - jnp.max(logits, axis=0, keepdims=True))
    afft = ex / jnp.sum(ex, axis=0, keepdims=True)
    afft_ref[...] = afft
    pad = jnp.zeros((LANES - N_EXPERTS, afft.shape[1]), F32)
    aff_ref[...] = jnp.concatenate([afft, pad], axis=0).T


def _post_call(o, x, wo, g, wrh, wrhl):
    t, d = x.shape
    tm = min(DENSE_ROW_TILE, t)
    return pl.pallas_call(
        _post_kernel,
        name="post",
        grid=(t // tm,),
        in_specs=[pl.BlockSpec((N_PLANES, tm, LANES), lambda i: (0, i, 0)),
                  pl.BlockSpec((tm, d), lambda i: (i, 0)),
                  pl.BlockSpec((d, d), lambda i: (0, 0)),
                  pl.BlockSpec((1, d), lambda i: (0, 0)),
                  pl.BlockSpec((N_EXPERTS, d), lambda i: (0, 0)),
                  pl.BlockSpec((2 * N_EXPERTS, d), lambda i: (0, 0))],
        out_specs=[pl.BlockSpec((tm, d), lambda i: (i, 0)),
                   pl.BlockSpec((tm, d), lambda i: (i, 0)),
                   pl.BlockSpec((tm, LANES), lambda i: (i, 0)),
                   pl.BlockSpec((N_EXPERTS, tm), lambda i: (0, i))],
        out_shape=[jax.ShapeDtypeStruct((t, d), F32),
                   jax.ShapeDtypeStruct((t, d), BF16),
                   jax.ShapeDtypeStruct((t, LANES), F32),
                   jax.ShapeDtypeStruct((N_EXPERTS, t), F32)],
        compiler_params=_cparams("parallel"),
    )(o, x, wo, g, wrh, wrhl)


def _threshold_kernel(aff_ref, thr_ref, need_ref, *, cap):
    bits = lax.bitcast_convert_type(aff_ref[...], I32)

    def body(k, cur):
        cand = cur | jnp.left_shift(jnp.int32(1), 30 - k)
        cnt = jnp.sum(jnp.where(bits >= cand, 1, 0), axis=1, keepdims=True)
        return jnp.where(cnt >= cap, cand, cur)

    cur = lax.fori_loop(0, 31, body, jnp.zeros((N_EXPERTS, 1), I32))
    above = jnp.sum(jnp.where(bits > cur, 1, 0), axis=1, keepdims=True)
    thr_ref[...] = jnp.broadcast_to(cur, thr_ref.shape)
    need_ref[...] = jnp.broadcast_to(cap - above, need_ref.shape)


def _threshold_call(afft, cap):
    e, t = afft.shape
    return pl.pallas_call(
        functools.partial(_threshold_kernel, cap=cap),
        name="threshold",
        grid=(1,),
        in_specs=[pl.BlockSpec((e, t), lambda i: (0, 0))],
        out_specs=[pl.BlockSpec((e, LANES), lambda i: (0, 0)),
                   pl.BlockSpec((e, LANES), lambda i: (0, 0))],
        out_shape=[jax.ShapeDtypeStruct((e, LANES), I32),
                   jax.ShapeDtypeStruct((e, LANES), I32)],
        compiler_params=_cparams("arbitrary"),
    )(afft)


def _positions_kernel(aff_ref, thr_ref, need_ref, lpos_ref, lpost_ref, starts_ref, npass_ref, base_sc, eqb_sc):
    @pl.when(pl.program_id(0) == 0)
    def _():
        base_sc[...] = jnp.zeros_like(base_sc)
        eqb_sc[...] = jnp.zeros_like(eqb_sc)

    thr = thr_ref[:, 0:1]
    need = need_ref[:, 0:1].astype(F32)
    r = lax.broadcasted_iota(I32, (ROUTE_TILE, ROUTE_TILE), 0)
    c = lax.broadcasted_iota(I32, (ROUTE_TILE, ROUTE_TILE), 1)
    before = jnp.where(r < c, 1.0, 0.0).astype(BF16)
    pad = jnp.full((LANES - N_EXPERTS, ROUTE_TILE), -1.0, F32)
    base = base_sc[...]
    eqb = eqb_sc[...]
    for k in range(starts_ref.shape[0]):
        cols = slice(k * ROUTE_TILE, (k + 1) * ROUTE_TILE)
        bits = lax.bitcast_convert_type(aff_ref[:, cols], I32)
        eq = bits == thr
        eqf = jnp.where(eq, 1.0, 0.0)
        eq_rank = eqb[:, 0:1] + jnp.dot(eqf.astype(BF16), before, preferred_element_type=F32)
        sel = (bits > thr) | (eq & (eq_rank < need))
        self_ = jnp.where(sel, 1.0, 0.0)
        rank = jnp.dot(self_.astype(BF16), before, preferred_element_type=F32)
        cnt = jnp.sum(self_, axis=1, keepdims=True)
        lpos = jnp.where(sel, rank, -1.0)
        lpos_ref[:, cols] = lpos.astype(I32)
        lpost_ref[cols, :] = jnp.concatenate([lpos, pad], axis=0).T
        starts_ref[k] = base.astype(I32)
        most = jnp.max(cnt, axis=0, keepdims=True)
        passes = sum(jnp.where(most > float(first), 1.0, 0.0) for first in range(0, ROUTE_TILE, SLAB))
        npass_ref[k] = jnp.broadcast_to(passes, npass_ref.shape[1:]).astype(I32)
        base = base + cnt
        eqb = eqb + jnp.sum(eqf, axis=1, keepdims=True)
    base_sc[...] = base
    eqb_sc[...] = eqb


def _positions_call(afft, thr, need):
    e, t = afft.shape
    n_tiles = t // ROUTE_TILE
    per_step = min(POSITION_TILES_PER_STEP, n_tiles)
    assert n_tiles % per_step == 0
    span = per_step * ROUTE_TILE
    return pl.pallas_call(
        _positions_kernel,
        name="positions",
        grid=(n_tiles // per_step,),
        in_specs=[pl.BlockSpec((e, span), lambda i: (0, i)),
                  pl.BlockSpec((e, LANES), lambda i: (0, 0)),
                  pl.BlockSpec((e, LANES), lambda i: (0, 0))],
        out_specs=[pl.BlockSpec((e, span), lambda i: (0, i)),
                   pl.BlockSpec((span, LANES), lambda i: (i, 0)),
                   pl.BlockSpec((per_step, e, LANES), lambda i: (i, 0, 0)),
                   pl.BlockSpec((per_step, SUBLANES, LANES), lambda i: (i, 0, 0))],
        out_shape=[jax.ShapeDtypeStruct((e, t), I32),
                   jax.ShapeDtypeStruct((t, LANES), F32),
                   jax.ShapeDtypeStruct((n_tiles, e, LANES), I32),
                   jax.ShapeDtypeStruct((n_tiles, SUBLANES, LANES), I32)],
        scratch_shapes=[pltpu.VMEM((e, LANES), F32), pltpu.VMEM((e, LANES), F32)],
        compiler_params=_cparams("arbitrary"),
    )(afft, thr, need)


SLAB_ROWS = SLAB * COL_BLOCKS
READ_SLAB = SLAB + SUBLANES


def _dispatch_kernel(starts_sm, npass_sm, h_ref, lpos_ref, xe_ref, slab_sc, sem, state_sm, *, cap):
    i = pl.program_id(0)
    sub = lax.broadcasted_iota(I32, (SLAB, ROUTE_TILE), 0)

    def slab_copy(buf, e, dst):
        return pltpu.make_async_copy(
            slab_sc.at[buf, pl.ds(e * SLAB_ROWS, SLAB_ROWS)],
            xe_ref.at[e, pl.ds(pl.multiple_of(dst * COL_BLOCKS, COL_BLOCKS), SLAB_ROWS)],
            sem.at[buf, e])

    def wait_all(buf):
        for e in range(N_EXPERTS):
            slab_copy(buf, e, 0).wait()

    @pl.when(i == 0)
    def _():
        slab_sc[0, pl.ds(0, SLAB_ROWS), :] = jnp.zeros((SLAB_ROWS, LANES), F32)
        pads = [pltpu.make_async_copy(slab_sc.at[0, pl.ds(0, SLAB_ROWS)],
                                      xe_ref.at[e, pl.ds(cap * COL_BLOCKS, SLAB_ROWS)], sem.at[0, e])
                for e in range(N_EXPERTS)]
        for cp in pads:
            cp.start()
        for cp in pads:
            cp.wait()
        state_sm[0] = 0
        state_sm[1] = 0

    def one_pass(tile, cols, s):
        buf = state_sm[1]
        rows = []
        for e in range(N_EXPERTS):
            lp = lpos_ref[e:e + 1, cols] - SLAB * s
            rows.append(jnp.where(lp == sub, 1.0, 0.0).astype(BF16))
        onehot = jnp.concatenate(rows, axis=0)
        res = jnp.dot(onehot, h_ref[cols, :], preferred_element_type=F32)
        for cb in range(COL_BLOCKS):
            slab_sc[buf, pl.ds(cb, N_EXPERTS * SLAB, stride=COL_BLOCKS), :] = res[:, cb * LANES:(cb + 1) * LANES]

        @pl.when(state_sm[0] == 1)
        def _():
            wait_all(1 - buf)

        for e in range(N_EXPERTS):
            dst = jnp.minimum(starts_sm[tile * N_EXPERTS + e] + SLAB * s, cap)
            slab_copy(buf, e, dst).start()
        state_sm[0] = 1
        state_sm[1] = 1 - buf

    for k in range(h_ref.shape[0] // ROUTE_TILE):
        tile = i * (h_ref.shape[0] // ROUTE_TILE) + k
        cols = slice(k * ROUTE_TILE, (k + 1) * ROUTE_TILE)

        def tile_pass(s, carry, tile=tile, cols=cols):
            one_pass(tile, cols, s)
            return carry

        lax.fori_loop(0, npass_sm[tile], tile_pass, 0)

    @pl.when((i == pl.num_programs(0) - 1) & (state_sm[0] == 1))
    def _():
        wait_all(1 - state_sm[1])


def _dispatch_call(starts, npass, h, lpos, cap):
    t, d = h.shape
    n_tiles = t // ROUTE_TILE
    per_step = min(ROUTE_TILES_PER_STEP, n_tiles)
    assert n_tiles % per_step == 0
    span = per_step * ROUTE_TILE
    grid_spec = pltpu.PrefetchScalarGridSpec(
        num_scalar_prefetch=2,
        grid=(n_tiles // per_step,),
        in_specs=[pl.BlockSpec((span, d), lambda i, *_: (i, 0)),
                  pl.BlockSpec((N_EXPERTS, span), lambda i, *_: (0, i))],
        out_specs=pl.BlockSpec(memory_space=pl.ANY),
        scratch_shapes=[pltpu.VMEM((2, N_EXPERTS * SLAB_ROWS, LANES), F32),
                        pltpu.SemaphoreType.DMA((2, N_EXPERTS)),
                        pltpu.SMEM((2,), I32)],
    )
    return pl.pallas_call(
        functools.partial(_dispatch_kernel, cap=cap),
        name="dispatch",
        grid_spec=grid_spec,
        out_shape=jax.ShapeDtypeStruct((N_EXPERTS, (cap + SLAB) * COL_BLOCKS, LANES), F32),
        compiler_params=_cparams("arbitrary"),
    )(starts, npass, h, lpos)


def _expert_kernel(x_ref, wg_hbm, wu_hbm, wd_hbm, o_ref, wg0, wu0, wd0, wg1, wu1, wd1, sg, su, sd, sem,
                   *, layer, n_chunks):
    e = pl.program_id(0)
    j = pl.program_id(1)
    n_experts = pl.num_programs(0)
    d_rows = D_MODEL // n_chunks
    f_rows = EXPERT_FF // n_chunks
    sets = ((wg0, wu0, wd0), (wg1, wu1, wd1))

    def chunk_copies(ex, c):
        slot = (ex * n_chunks + c) % 2
        return [pltpu.make_async_copy(wg_hbm.at[layer, ex, pl.ds(c * d_rows, d_rows)], sg.at[slot], sem.at[0, slot]),
                pltpu.make_async_copy(wu_hbm.at[layer, ex, pl.ds(c * d_rows, d_rows)], su.at[slot], sem.at[1, slot]),
                pltpu.make_async_copy(wd_hbm.at[layer, ex, pl.ds(c * f_rows, f_rows)], sd.at[slot], sem.at[2, slot])]

    def start(ex, c):
        for cp in chunk_copies(ex, c):
            cp.start()

    def finish(ex, c, into):
        for cp in chunk_copies(ex, c):
            cp.wait()
        slot = (ex * n_chunks + c) % 2
        wg, wu, wd = into
        wg[pl.ds(c * d_rows, d_rows), :] = sg[slot].astype(BF16)
        wu[pl.ds(c * d_rows, d_rows), :] = su[slot].astype(BF16)
        wd[pl.ds(c * f_rows, f_rows), :] = sd[slot].astype(BF16)

    @pl.when((e == 0) & (j == 0))
    def _():
        for c in range(n_chunks):
            start(0, c)
            finish(0, c, sets[0])
        start(1, 0)

    nxt = (e + 1) % n_experts

    @pl.when(j + 1 < n_chunks)
    def _():
        start(nxt, j + 1)

    @pl.when((j + 1 == n_chunks) & (e + 1 < n_experts))
    def _():
        start((e + 2) % n_experts, 0)

    def step(cur, other):
        finish(nxt, j, other)
        wg, wu, wd = cur
        tm = x_ref.shape[1] // COL_BLOCKS
        x = jnp.concatenate([x_ref[0, pl.ds(cb, tm, stride=COL_BLOCKS), :] for cb in range(COL_BLOCKS)],
                            axis=1).astype(BF16)
        acc = jnp.zeros((tm, D_MODEL), F32)
        for fc in range(EXPERT_FF // FF_CHUNK):
            fs = slice(fc * FF_CHUNK, (fc + 1) * FF_CHUNK)
            g = jnp.dot(x, wg[:, fs], preferred_element_type=F32)
            u = jnp.dot(x, wu[:, fs], preferred_element_type=F32)
            hmid = (g * (1.0 / (1.0 + jnp.exp(-g))) * u).astype(BF16)
            acc = acc + jnp.dot(hmid, wd[fs, :], preferred_element_type=F32)
        for cb in range(COL_BLOCKS):
            o_ref[0, cb] = acc[:, cb * LANES:(cb + 1) * LANES]

    @pl.when(e % 2 == 0)
    def _():
        step(sets[0], sets[1])

    @pl.when(e % 2 == 1)
    def _():
        step(sets[1], sets[0])


def _expert_call(xe, wg, wu, wd, layer, cap):
    tm = min(ROW_TILE, cap)
    n_chunks = cap // tm
    d, f = wg.shape[2], wg.shape[3]
    assert d % n_chunks == 0 and f % n_chunks == 0
    return pl.pallas_call(
        functools.partial(_expert_kernel, layer=layer, n_chunks=n_chunks),
        name="expert",
        grid=(N_EXPERTS, n_chunks),
        in_specs=[pl.BlockSpec((1, tm * COL_BLOCKS, LANES), lambda e, j: (e, j, 0)),
                  pl.BlockSpec(memory_space=pl.ANY),
                  pl.BlockSpec(memory_space=pl.ANY),
                  pl.BlockSpec(memory_space=pl.ANY)],
        out_specs=pl.BlockSpec((1, COL_BLOCKS, tm, LANES), lambda e, j: (e, 0, j, 0)),
        out_shape=jax.ShapeDtypeStruct((N_EXPERTS, COL_BLOCKS, cap, LANES), F32),
        scratch_shapes=[pltpu.VMEM((d, f), BF16), pltpu.VMEM((d, f), BF16), pltpu.VMEM((f, d), BF16),
                        pltpu.VMEM((d, f), BF16), pltpu.VMEM((d, f), BF16), pltpu.VMEM((f, d), BF16),
                        pltpu.VMEM((2, d // n_chunks, f), F32), pltpu.VMEM((2, d // n_chunks, f), F32),
                        pltpu.VMEM((2, f // n_chunks, d), F32),
                        pltpu.SemaphoreType.DMA((3, 2))],
        compiler_params=_cparams("arbitrary", "arbitrary"),
    )(xe, wg, wu, wd)


def _combine_kernel(starts_sm, npass_sm, x_ref, aff_ref, lpost_ref, expand_ref, g_ref, ye_ref, o_ref,
                    slab_sc, sem, *, cap, final):
    i = pl.program_id(0)
    per_step = x_ref.shape[0] // ROUTE_TILE
    lane = lax.broadcasted_iota(I32, (1, LANES), 1)
    slot_in_slab = (lax.broadcasted_iota(I32, (1, N_EXPERTS * READ_SLAB), 1) % READ_SLAB).astype(F32)
    expand = expand_ref[...]

    def slab_src(tile, s, e):
        want = starts_sm[tile * N_EXPERTS + e] + SLAB * s
        src = jnp.minimum((want // SUBLANES) * SUBLANES, cap - READ_SLAB)
        return want, pl.multiple_of(src, SUBLANES)

    def slab_copy(tile, s, to, e):
        src = slab_src(tile, s, e)[1]
        return pltpu.make_async_copy(
            ye_ref.at[e, :, pl.ds(src, READ_SLAB), :],
            slab_sc.at[to, :, pl.ds(e * READ_SLAB, READ_SLAB), :],
            sem.at[to, e])

    def fetch(tile, s, to):
        for e in range(N_EXPERTS):
            slab_copy(tile, s, to, e).start()

    def one_tile(k):
        tile = i * per_step + k
        rows_k = slice(k * ROUTE_TILE, (k + 1) * ROUTE_TILE)
        buf = (k % 2) if per_step % 2 == 0 else tile % 2
        aff = aff_ref[rows_k, :]
        ghi = aff.astype(BF16)
        glo = (aff - ghi.astype(F32)).astype(BF16)
        lpost = lpost_ref[rows_k, :]

        def slab_row(s):
            shift = jnp.zeros((1, LANES), F32)
            for e in range(N_EXPERTS):
                want, src = slab_src(tile, s, e)
                shift = jnp.where(lane == e, (want - src).astype(F32), shift)
            lo = jnp.asarray(SLAB * s, F32)
            owned = (lpost >= lo) & (lpost < lo + SLAB)
            return jnp.where(owned, lpost - lo + shift, -1.0).astype(BF16)

        spread = jnp.dot(jnp.concatenate([ghi, glo, slab_row(0)], axis=0), expand, preferred_element_type=F32)
        ghi_x = spread[0:ROUTE_TILE]
        glo_x = spread[ROUTE_TILE:2 * ROUTE_TILE]

        def add_pass(s, row_x, acc):
            match = row_x == slot_in_slab
            w = jnp.concatenate([jnp.where(match, ghi_x, 0.0).astype(BF16),
                                 jnp.where(match, glo_x, 0.0).astype(BF16)], axis=0)
            for e in range(N_EXPERTS):
                slab_copy(tile, s, buf, e).wait()
            rows = jnp.concatenate([slab_sc[buf, cb] for cb in range(COL_BLOCKS)],
                                   axis=1).astype(BF16)
            both = jnp.dot(w, rows, preferred_element_type=F32)
            return acc + (both[0:ROUTE_TILE] + both[ROUTE_TILE:2 * ROUTE_TILE])

        if k + 1 < per_step:
            fetch(tile + 1, 0, 1 - buf)
        else:
            @pl.when(i + 1 < pl.num_programs(0))
            def _():
                fetch(tile + 1, 0, 1 - buf)

        y = add_pass(0, spread[2 * ROUTE_TILE:3 * ROUTE_TILE], jnp.zeros((ROUTE_TILE, x_ref.shape[1]), F32))

        def extra_pass(s, acc):
            fetch(tile, s, buf)
            return add_pass(s, jnp.dot(slab_row(s), expand, preferred_element_type=F32), acc)

        y = lax.fori_loop(1, npass_sm[tile], extra_pass, y)
        out = x_ref[rows_k, :] + y
        if final:
            out = _rmsnorm(out, g_ref[...])
        o_ref[rows_k, :] = out

    @pl.when(i == 0)
    def _():
        fetch(0, 0, 0)

    for k in range(per_step):
        one_tile(k)


def _combine_call(starts, npass, x1, aff, lpost, expand, g, ye, cap, final):
    t, d = x1.shape
    n_tiles = t // ROUTE_TILE
    per_step = min(ROUTE_TILES_PER_STEP, n_tiles)
    assert n_tiles % per_step == 0
    span = per_step * ROUTE_TILE
    grid_spec = pltpu.PrefetchScalarGridSpec(
        num_scalar_prefetch=2,
        grid=(n_tiles // per_step,),
        in_specs=[pl.BlockSpec((span, d), lambda i, *_: (i, 0)),
                  pl.BlockSpec((span, LANES), lambda i, *_: (i, 0)),
                  pl.BlockSpec((span, LANES), lambda i, *_: (i, 0)),
                  pl.BlockSpec((LANES, N_EXPERTS * READ_SLAB), lambda i, *_: (0, 0)),
                  pl.BlockSpec((1, d), lambda i, *_: (0, 0)),
                  pl.BlockSpec(memory_space=pl.ANY)],
        out_specs=pl.BlockSpec((span, d), lambda i, *_: (i, 0)),
        scratch_shapes=[pltpu.VMEM((2, COL_BLOCKS, N_EXPERTS * READ_SLAB, LANES), F32),
                        pltpu.SemaphoreType.DMA((2, N_EXPERTS))],
    )
    return pl.pallas_call(
        functools.partial(_combine_kernel, cap=cap, final=final),
        name="combine",
        grid_spec=grid_spec,
        out_shape=jax.ShapeDtypeStruct((t, d), F32),
        compiler_params=_cparams("arbitrary"),
    )(starts, npass, x1, aff, lpost, expand, g, ye)


def _moe(x1, h, aff, afft, wg, wu, wd, layer, expand, g_final, final):
    t = x1.shape[0]
    cap = 2 * t // N_EXPERTS
    assert t % ROUTE_TILE == 0 and cap >= READ_SLAB and cap % SUBLANES == 0
    thr, need = _threshold_call(afft, cap)
    lpos, lpost, starts3, npass3 = _positions_call(afft, thr, need)
    starts = starts3[:, :, 0].reshape(-1)
    npass = npass3[:, 0, 0]
    xe = _dispatch_call(starts, npass, h, lpos, cap)
    ye = _expert_call(xe, wg, wu, wd, layer, cap)
    return _combine_call(starts, npass, x1, aff, lpost, expand, g_final, ye, cap, final)


def _trunk(x, p):
    batch, seq, d = x.shape
    x = x.reshape(batch * seq, d)
    depth = p["norm_mix"].shape[0]
    for i in range(depth):
        j = i // 2
        if i % 2 == 0:
            qk, vt = _qkv_call(x, p["norm_mix"][i:i + 1], p["a_wqkv"][j], N_PLANES)
            o = _attn_call("na_attn", qk, vt, p["a_bias"][j], p["a_edge"], None, batch, seq,
                           NA_UNIT, NA_UNITS_PER_STEP, N_PLANES)
            wo = p["a_wo"][j]
        else:
            qk, vt = _qkv_call(x, p["norm_mix"][i:i + 1], p["b_wqkv"][j], 2)
            o = _attn_call("sw_attn", qk, vt, p["b_bias"], None, p["b_sink"][j], batch, seq,
                           SW_UNIT, SW_UNITS_PER_STEP, 2)
            wo = p["b_wo"][j]
        x1, h, aff, afft = _post_call(o, x, wo, p["norm_ffn"][i:i + 1], p["wr_hi"][i], p["wr_hl"][i])
        x = _moe(x1, h, aff, afft, p["w_gate"], p["w_up"], p["w_down"], i,
                 p["expand"], p["norm_final"], i == depth - 1)
    return x.reshape(batch, seq, d)


def _prepare(norm_mix, norm_ffn, norm_final, a_wqkv, a_wo, a_rpb, b_wqkv, b_wo, b_sink,
             w_router, w_gate, w_up, w_down):
    d = D_MODEL
    scale = HEAD_DIM ** -0.5 * LOG2E
    a_w = jnp.concatenate([a_wqkv[:, :, :d] * scale, a_wqkv[:, :, d:]], axis=2).astype(BF16)
    order = SW_HEAD_ORDER
    nb = b_wqkv.shape[0]
    bq = (b_wqkv[:, :, :d] * scale).reshape(nb, d, N_HEADS, HEAD_DIM)[:, :, order].reshape(nb, d, d)
    b_w = jnp.concatenate([bq, b_wqkv[:, :, d:]], axis=2).astype(BF16)
    b_wo_p = b_wo.reshape(nb, N_HEADS, HEAD_DIM, d)[:, order].reshape(nb, d, d).astype(BF16)
    sink = (b_sink.astype(F32) * LOG2E)[:, order].reshape(nb, 2, 1, 8, 1)
    sink = jnp.broadcast_to(sink, (nb, 2, SUBLANES, 8, SW_UNIT)).reshape(nb, 2, SUBLANES, 8 * SW_UNIT)
    wr = w_router.astype(F32).transpose(0, 2, 1)
    wr_hi = wr.astype(BF16)
    wr_lo = (wr - wr_hi.astype(F32)).astype(BF16)
    wr_hl = jnp.concatenate([wr_hi, wr_lo], axis=1)
    lane_expert = np.arange(N_EXPERTS * READ_SLAB) // READ_SLAB
    expand = jnp.asarray(np.arange(LANES)[:, None] == lane_expert[None, :], dtype=BF16)
    return {
        "norm_mix": norm_mix.astype(F32), "norm_ffn": norm_ffn.astype(F32),
        "norm_final": norm_final.astype(F32).reshape(1, d),
        "a_wqkv": a_w, "a_wo": a_wo.astype(BF16),
        "a_bias": jnp.stack([_na_bias_table(a_rpb[j] * LOG2E) for j in range(a_rpb.shape[0])]),
        "a_edge": _na_edge_mask(),
        "b_wqkv": b_w, "b_wo": b_wo_p, "b_bias": _sw_bias_table(), "b_sink": sink,
        "wr_hi": wr_hi, "wr_hl": wr_hl,
        "w_gate": w_gate.astype(F32), "w_up": w_up.astype(F32), "w_down": w_down.astype(F32),
        "expand": expand,
    }


def kernel(x_prompt, x_sample, norm_mix, norm_ffn, norm_final, a_wqkv, a_wo, a_rpb, b_wqkv, b_wo, b_sink,
           w_router, w_gate, w_up, w_down):
    p = _prepare(norm_mix, norm_ffn, norm_final, a_wqkv, a_wo, a_rpb, b_wqkv, b_wo, b_sink,
                 w_router, w_gate, w_up, w_down)
    return (_trunk(x_prompt, p), _trunk(x_sample, p))
```

```python
import functools

import numpy as np
import jax
import jax.numpy as jnp
from jax import lax
from jax.experimental import pallas as pl
from jax.experimental.pallas import tpu as pltpu

F32 = jnp.float32
BF16 = jnp.bfloat16
I32 = jnp.int32

D_MODEL = 1024
HEAD_DIM = 64
N_HEADS = 16
LANES = 128
SUBLANES = 8
N_PLANES = D_MODEL // LANES
COL_BLOCKS = D_MODEL // LANES
GRID_W = 64
NA_KH = 8
NA_KW = 16
NA_UNIT = 4 * GRID_W
NA_UNITS_PER_STEP = 16
SW_KV_HEADS = 4
SW_WINDOW = 128
SW_UNIT = 128
SW_UNITS_PER_STEP = 16
N_EXPERTS = 16
EXPERT_FF = 2048
FF_CHUNK = 512
ROUTE_TILE = 256
POSITION_TILES_PER_STEP = 8
SLAB = 56
RMS_EPS = 1e-6
NEG = -1e30
LOG2E = 1.4426950408889634
ROW_TILE = 512
DENSE_ROW_TILE = 1024
VMEM_LIMIT = 56 * 1024 * 1024


def _cparams(*sem):
    return pltpu.CompilerParams(dimension_semantics=sem, vmem_limit_bytes=VMEM_LIMIT)


def _rmsnorm(x, g):
    return x * lax.rsqrt(jnp.mean(x * x, axis=-1, keepdims=True) + RMS_EPS) * g


def _qkv_kernel(x_ref, g_ref, w_ref, qk_ref, vt_ref):
    y = _rmsnorm(x_ref[...], g_ref[...]).astype(BF16)
    r = jnp.dot(y, w_ref[...], preferred_element_type=F32)
    n_qk = qk_ref.shape[0]
    for p in range(n_qk):
        qk_ref[p] = r[:, p * LANES:(p + 1) * LANES].astype(BF16)
    for p in range(vt_ref.shape[0]):
        vt_ref[p] = r[:, (n_qk + p) * LANES:(n_qk + p + 1) * LANES].T.astype(BF16)


def _qkv_call(x, g, w, n_v_planes):
    t, d = x.shape
    n_qk = w.shape[1] // LANES - n_v_planes
    tm = min(DENSE_ROW_TILE, t)
    return pl.pallas_call(
        _qkv_kernel,
        name="qkv",
        grid=(t // tm,),
        in_specs=[pl.BlockSpec((tm, d), lambda i: (i, 0)),
                  pl.BlockSpec((1, d), lambda i: (0, 0)),
                  pl.BlockSpec((d, w.shape[1]), lambda i: (0, 0))],
        out_specs=[pl.BlockSpec((n_qk, tm, LANES), lambda i: (0, i, 0)),
                   pl.BlockSpec((n_v_planes, LANES, tm), lambda i: (0, 0, i))],
        out_shape=[jax.ShapeDtypeStruct((n_qk, t, LANES), BF16),
                   jax.ShapeDtypeStruct((n_v_planes, LANES, t), BF16)],
        compiler_params=_cparams("parallel"),
    )(x, g, w)


def _edge_case(u, n_units):
    i = pl.program_id(2)
    last = pl.num_programs(2) - 1
    if n_units == 1:
        return jnp.where(i == 0, 1, jnp.where(i == last, 2, 0))
    if u == 0:
        return jnp.where(i == 0, 1, 0)
    if u == n_units - 1:
        return jnp.where(i == last, 2, 0)
    return 0


def _attn_kernel(*refs, unit, n_units, has_sink):
    q_ref, kp_ref, kc_ref, kn_ref, vp_ref, vc_ref, vn_ref, bias_ref = refs[:8]
    sink_ref = refs[8] if has_sink else None
    o_ref = refs[-1]
    n_planes = q_ref.shape[0]
    kcat = jnp.concatenate([kp_ref[0], kc_ref[0], kn_ref[0]], axis=0)
    vcat = jnp.concatenate([vp_ref[0], vc_ref[0], vn_ref[0]], axis=1)
    vcat = jnp.concatenate([vcat, jnp.ones((SUBLANES, vcat.shape[1]), BF16)], axis=0)
    lo_half = lax.broadcasted_iota(I32, (unit, LANES), 1) < HEAD_DIM
    zero = jnp.zeros((unit, LANES), BF16)

    def scores(u):
        kw = kcat[u * unit:(u + 3) * unit]
        qs = [q_ref[pi, u * unit:(u + 1) * unit, :] for pi in range(n_planes)]
        q_stack = jnp.concatenate(
            [jnp.where(lo_half if hh == 0 else jnp.logical_not(lo_half), q, zero) for q in qs for hh in range(2)],
            axis=0)
        return lax.dot_general(kw, q_stack, (((1,), (1,)), ((), ())), preferred_element_type=F32)

    st_next = scores(0)
    for u in range(n_units):
        case = _edge_case(u, n_units)
        vw = vcat[:, u * unit:(u + 3) * unit]
        st = st_next + bias_ref[0, case]
        if u + 1 < n_units:
            st_next = scores(u + 1)
        m = jnp.max(st, axis=0, keepdims=True)
        if has_sink:
            sink = sink_ref[0, 0:1, :]
            m = jnp.maximum(m, sink)
        e = jnp.exp2(st - m)
        ot = jnp.dot(vw, e.astype(BF16), preferred_element_type=F32)
        l = ot[LANES:LANES + 1, :]
        if has_sink:
            l = l + jnp.exp2(sink - m)
        ot = ot[0:LANES, :] / l
        for pi in range(n_planes):
            both = jnp.concatenate(
                [ot[0:HEAD_DIM, (2 * pi) * unit:(2 * pi + 1) * unit],
                 ot[HEAD_DIM:LANES, (2 * pi + 1) * unit:(2 * pi + 2) * unit]], axis=0)
            o_ref[pi, u * unit:(u + 1) * unit, :] = both.T.astype(BF16)


def _attn_call(name, qk, vt, bias, sink, batch, seq, unit, n_units, n_groups):
    t = batch * seq
    units = seq // unit
    n_units = min(n_units, units)
    steps = units // n_units
    assert seq % (unit * n_units) == 0 and units >= 4
    ppg = N_PLANES // n_groups
    qb = unit * n_units
    width = 2 * ppg * unit

    def cur(g, b, i):
        return b * steps + i

    def prev(g, b, i):
        return b * units + jnp.maximum(n_units * i - 1, 0)

    def nxt(g, b, i):
        return b * units + jnp.minimum(n_units * i + n_units, units - 1)

    in_specs = [pl.BlockSpec((ppg, qb, LANES), lambda g, b, i: (g, cur(g, b, i), 0)),
                pl.BlockSpec((1, unit, LANES), lambda g, b, i: (N_PLANES + g, prev(g, b, i), 0)),
                pl.BlockSpec((1, qb, LANES), lambda g, b, i: (N_PLANES + g, cur(g, b, i), 0)),
                pl.BlockSpec((1, unit, LANES), lambda g, b, i: (N_PLANES + g, nxt(g, b, i), 0)),
                pl.BlockSpec((1, LANES, unit), lambda g, b, i: (g, 0, prev(g, b, i))),
                pl.BlockSpec((1, LANES, qb), lambda g, b, i: (g, 0, cur(g, b, i))),
                pl.BlockSpec((1, LANES, unit), lambda g, b, i: (g, 0, nxt(g, b, i))),
                pl.BlockSpec((1, 3, 3 * unit, width), lambda g, b, i: (g, 0, 0, 0))]
    args = [qk, qk, qk, qk, vt, vt, vt, bias]
    if sink is not None:
        in_specs.append(pl.BlockSpec((1, SUBLANES, width), lambda g, b, i: (g, 0, 0)))
        args.append(sink)
    return pl.pallas_call(
        functools.partial(_attn_kernel, unit=unit, n_units=n_units, has_sink=sink is not None),
        name=name,
        grid=(n_groups, batch, steps),
        in_specs=in_specs,
        out_specs=pl.BlockSpec((ppg, qb, LANES), lambda g, b, i: (g, cur(g, b, i), 0)),
        out_shape=jax.ShapeDtypeStruct((N_PLANES, t, LANES), BF16),
        compiler_params=_cparams("arbitrary", "arbitrary", "arbitrary"),
    )(*args)


def _na_bias_table(rpb):
    a = np.arange(NA_UNIT) // GRID_W
    j = np.arange(3 * NA_UNIT) // GRID_W
    row_ok = np.stack([
        (j[None, :] >= a[:, None]) & (j[None, :] < a[:, None] + NA_KH),
        np.broadcast_to((j[None, :] >= 4) & (j[None, :] < 4 + NA_KH), (NA_UNIT, 3 * NA_UNIT)),
        np.broadcast_to(j[None, :] < NA_KH, (NA_UNIT, 3 * NA_UNIT)),
    ])
    w = np.arange(GRID_W)
    dc = np.clip(w[None, :] - w[:, None], -(NA_KW - 1), NA_KW - 1) + NA_KW - 1
    c0 = np.clip(w - NA_KW // 2, 0, GRID_W - NA_KW)
    col_ok = (w[None, :] >= c0[:, None]) & (w[None, :] < c0[:, None] + NA_KW)
    cols = jnp.take(rpb.astype(F32), jnp.asarray(dc.reshape(-1)), axis=2)
    cols = cols.reshape(N_HEADS, 2 * NA_KH - 1, GRID_W, GRID_W)
    cols = jnp.where(jnp.asarray(col_ok), cols, NEG)
    rows = jnp.stack([cols[:, 3 - ai:15 - ai] for ai in range(4)], axis=1)
    tbl = rows.transpose(0, 2, 4, 1, 3).reshape(N_HEADS, 3 * NA_UNIT, NA_UNIT)
    ok = jnp.asarray(row_ok.transpose(0, 2, 1))
    full = jnp.where(ok[:, None], tbl[None], NEG)
    full = full.reshape(3, N_PLANES, 2, 3 * NA_UNIT, NA_UNIT).transpose(1, 0, 3, 2, 4)
    return full.reshape(N_PLANES, 3, 3 * NA_UNIT, 2 * NA_UNIT)


SW_HEAD_ORDER = np.array([8 * m + 4 * hh + i for m in range(2) for i in range(4) for hh in range(2)])


def _sw_bias_table():
    slopes = np.asarray(2.0 ** (-8.0 * np.arange(1, N_HEADS + 1) / N_HEADS), dtype=np.float32)
    ks = np.arange(3 * SW_UNIT) - SW_WINDOW
    dist = ks[None, :] - np.arange(SW_UNIT)[:, None]
    win_ok = np.abs(dist) <= SW_WINDOW
    pos_ok = np.stack([np.ones(3 * SW_UNIT, bool), ks >= 0, ks < SW_UNIT])
    ok = win_ok[None] & pos_ok[:, None, :]
    alibi = -slopes[SW_HEAD_ORDER][:, None, None] * np.abs(dist).astype(np.float32)[None]
    alibi = alibi * np.float32(LOG2E)
    full = np.where(ok[:, None], alibi[None], np.float32(NEG)).astype(np.float32)
    full = full.reshape(3, 2, 8, SW_UNIT, 3 * SW_UNIT).transpose(1, 0, 4, 2, 3)
    return jnp.asarray(full.reshape(2, 3, 3 * SW_UNIT, 8 * SW_UNIT))


def _post_kernel(o_ref, x_ref, wo_ref, g_ref, wrh_ref, wrhl_ref, x1_ref, h_ref, aff_ref, afft_ref):
    oc = jnp.concatenate([o_ref[p] for p in range(N_PLANES)], axis=1)
    x1 = x_ref[...] + jnp.dot(oc, wo_ref[...], preferred_element_type=F32)
    x1_ref[...] = x1
    h = _rmsnorm(x1, g_ref[...])
    hb = h.astype(BF16)
    h_ref[...] = hb
    hl = (h - hb.astype(F32)).astype(BF16)
    nt = (((1,), (1,)), ((), ()))
    by_hb = lax.dot_general(wrhl_ref[...], hb, nt, preferred_element_type=F32)
    by_hl = lax.dot_general(wrh_ref[...], hl, nt, preferred_element_type=F32)
    logits = by_hb[0:N_EXPERTS] + (by_hb[N_EXPERTS:2 * N_EXPERTS] + by_hl)
    ex = jnp.exp(logits - jnp.max(logits, axis=0, keepdims=True))
    afft = ex / jnp.sum(ex, axis=0, keepdims=True)
    afft_ref[...] = afft
    pad = jnp.zeros((LANES - N_EXPERTS, afft.shape[1]), F32)
    aff_ref[...] = jnp.concatenate([afft, pad], axis=0).T


def _post_call(o, x, wo, g, wrh, wrhl):
    t, d = x.shape
    tm = min(DENSE_ROW_TILE, t)
    return pl.pallas_call(
        _post_kernel,
        name="post",
        grid=(t // tm,),
        in_specs=[pl.BlockSpec((N_PLANES, tm, LANES), lambda i: (0, i, 0)),
                  pl.BlockSpec((tm, d), lambda i: (i, 0)),
                  pl.BlockSpec((d, d), lambda i: (0, 0)),
                  pl.BlockSpec((1, d), lambda i: (0, 0)),
                  pl.BlockSpec((N_EXPERTS, d), lambda i: (0, 0)),
                  pl.BlockSpec((2 * N_EXPERTS, d), lambda i: (0, 0))],
        out_specs=[pl.BlockSpec((tm, d), lambda i: (i, 0)),
                   pl.BlockSpec((tm, d), lambda i: (i, 0)),
                   pl.BlockSpec((tm, LANES), lambda i: (i, 0)),
                   pl.BlockSpec((N_EXPERTS, tm), lambda i: (0, i))],
        out_shape=[jax.ShapeDtypeStruct((t, d), F32),
                   jax.ShapeDtypeStruct((t, d), BF16),
                   jax.ShapeDtypeStruct((t, LANES), F32),
                   jax.ShapeDtypeStruct((N_EXPERTS, t), F32)],
        compiler_params=_cparams("parallel"),
    )(o, x, wo, g, wrh, wrhl)


def _threshold_kernel(aff_ref, thr_ref, need_ref, *, cap):
    bits = lax.bitcast_convert_type(aff_ref[...], I32)

    def body(k, cur):
        cand = cur | jnp.left_shift(jnp.int32(1), 30 - k)
        cnt = jnp.sum(jnp.where(bits >= cand, 1, 0), axis=1, keepdims=True)
        return jnp.where(cnt >= cap, cand, cur)

    cur = lax.fori_loop(0, 31, body, jnp.zeros((N_EXPERTS, 1), I32))
    above = jnp.sum(jnp.where(bits > cur, 1, 0), axis=1, keepdims=True)
    thr_ref[...] = jnp.broadcast_to(cur, thr_ref.shape)
    need_ref[...] = jnp.broadcast_to(cap - above, need_ref.shape)


def _threshold_call(afft, cap):
    e, t = afft.shape
    return pl.pallas_call(
        functools.partial(_threshold_kernel, cap=cap),
        name="threshold",
        grid=(1,),
        in_specs=[pl.BlockSpec((e, t), lambda i: (0, 0))],
        out_specs=[pl.BlockSpec((e, LANES), lambda i: (0, 0)),
                   pl.BlockSpec((e, LANES), lambda i: (0, 0))],
        out_shape=[jax.ShapeDtypeStruct((e, LANES), I32),
                   jax.ShapeDtypeStruct((e, LANES), I32)],
        compiler_params=_cparams("arbitrary"),
    )(afft)


def _positions_kernel(aff_ref, thr_ref, need_ref, lpos_ref, lpost_ref, starts_ref, npass_ref, base_sc, eqb_sc):
    @pl.when(pl.program_id(0) == 0)
    def _():
        base_sc[...] = jnp.zeros_like(base_sc)
        eqb_sc[...] = jnp.zeros_like(eqb_sc)

    thr = thr_ref[:, 0:1]
    need = need_ref[:, 0:1].astype(F32)
    r = lax.broadcasted_iota(I32, (ROUTE_TILE, ROUTE_TILE), 0)
    c = lax.broadcasted_iota(I32, (ROUTE_TILE, ROUTE_TILE), 1)
    before = jnp.where(r < c, 1.0, 0.0).astype(BF16)
    pad = jnp.full((LANES - N_EXPERTS, ROUTE_TILE), -1.0, F32)
    base = base_sc[...]
    eqb = eqb_sc[...]
    for k in range(starts_ref.shape[0]):
        cols = slice(k * ROUTE_TILE, (k + 1) * ROUTE_TILE)
        bits = lax.bitcast_convert_type(aff_ref[:, cols], I32)
        eq = bits == thr
        eqf = jnp.where(eq, 1.0, 0.0)
        eq_rank = eqb[:, 0:1] + jnp.dot(eqf.astype(BF16), before, preferred_element_type=F32)
        sel = (bits > thr) | (eq & (eq_rank < need))
        self_ = jnp.where(sel, 1.0, 0.0)
        rank = jnp.dot(self_.astype(BF16), before, preferred_element_type=F32)
        cnt = jnp.sum(self_, axis=1, keepdims=True)
        lpos = jnp.where(sel, rank, -1.0)
        lpos_ref[:, cols] = lpos.astype(I32)
        lpost_ref[cols, :] = jnp.concatenate([lpos, pad], axis=0).T
        starts_ref[k] = base.astype(I32)
        most = jnp.max(cnt, axis=0, keepdims=True)
        passes = sum(jnp.where(most > float(first), 1.0, 0.0) for first in range(0, ROUTE_TILE, SLAB))
        npass_ref[k] = jnp.broadcast_to(passes, npass_ref.shape[1:]).astype(I32)
        base = base + cnt
        eqb = eqb + jnp.sum(eqf, axis=1, keepdims=True)
    base_sc[...] = base
    eqb_sc[...] = eqb


def _positions_call(afft, thr, need):
    e, t = afft.shape
    n_tiles = t // ROUTE_TILE
    per_step = min(POSITION_TILES_PER_STEP, n_tiles)
    assert n_tiles % per_step == 0
    span = per_step * ROUTE_TILE
    return pl.pallas_call(
        _positions_kernel,
        name="positions",
        grid=(n_tiles // per_step,),
        in_specs=[pl.BlockSpec((e, span), lambda i: (0, i)),
                  pl.BlockSpec((e, LANES), lambda i: (0, 0)),
                  pl.BlockSpec((e, LANES), lambda i: (0, 0))],
        out_specs=[pl.BlockSpec((e, span), lambda i: (0, i)),
                   pl.BlockSpec((span, LANES), lambda i: (i, 0)),
                   pl.BlockSpec((per_step, e, LANES), lambda i: (i, 0, 0)),
                   pl.BlockSpec((per_step, SUBLANES, LANES), lambda i: (i, 0, 0))],
        out_shape=[jax.ShapeDtypeStruct((e, t), I32),
                   jax.ShapeDtypeStruct((t, LANES), F32),
                   jax.ShapeDtypeStruct((n_tiles, e, LANES), I32),
                   jax.ShapeDtypeStruct((n_tiles, SUBLANES, LANES), I32)],
        scratch_shapes=[pltpu.VMEM((e, LANES), F32), pltpu.VMEM((e, LANES), F32)],
        compiler_params=_cparams("arbitrary"),
    )(afft, thr, need)


SLAB_ROWS = SLAB * COL_BLOCKS
READ_SLAB = SLAB + SUBLANES


def _dispatch_kernel(starts_sm, npass_sm, h_ref, lpos_ref, xe_ref, slab_sc, sem, state_sm, *, cap):
    i = pl.program_id(0)
    sub = lax.broadcasted_iota(I32, (SLAB, ROUTE_TILE), 0)

    def slab_copy(buf, e, dst):
        return pltpu.make_async_copy(
            slab_sc.at[buf, pl.ds(e * SLAB_ROWS, SLAB_ROWS)],
            xe_ref.at[e, pl.ds(pl.multiple_of(dst * COL_BLOCKS, COL_BLOCKS), SLAB_ROWS)],
            sem.at[buf, e])

    def wait_all(buf):
        for e in range(N_EXPERTS):
            slab_copy(buf, e, 0).wait()

    @pl.when(i == 0)
    def _():
        slab_sc[0, pl.ds(0, SLAB_ROWS), :] = jnp.zeros((SLAB_ROWS, LANES), F32)
        pads = [pltpu.make_async_copy(slab_sc.at[0, pl.ds(0, SLAB_ROWS)],
                                      xe_ref.at[e, pl.ds(cap * COL_BLOCKS, SLAB_ROWS)], sem.at[0, e])
                for e in range(N_EXPERTS)]
        for cp in pads:
            cp.start()
        for cp in pads:
            cp.wait()
        state_sm[0] = 0
        state_sm[1] = 0

    def one_pass(s, carry):
        buf = state_sm[1]
        rows = []
        for e in range(N_EXPERTS):
            lp = lpos_ref[e:e + 1, :] - SLAB * s
            rows.append(jnp.where(lp == sub, 1.0, 0.0).astype(BF16))
        onehot = jnp.concatenate(rows, axis=0)
        res = jnp.dot(onehot, h_ref[...], preferred_element_type=F32)
        for cb in range(COL_BLOCKS):
            slab_sc[buf, pl.ds(cb, N_EXPERTS * SLAB, stride=COL_BLOCKS), :] = res[:, cb * LANES:(cb + 1) * LANES]

        @pl.when(state_sm[0] == 1)
        def _():
            wait_all(1 - buf)

        for e in range(N_EXPERTS):
            dst = jnp.minimum(starts_sm[i * N_EXPERTS + e] + SLAB * s, cap)
            slab_copy(buf, e, dst).start()
        state_sm[0] = 1
        state_sm[1] = 1 - buf
        return carry

    lax.fori_loop(0, npass_sm[i], one_pass, 0)

    @pl.when((i == pl.num_programs(0) - 1) & (state_sm[0] == 1))
    def _():
        wait_all(1 - state_sm[1])


def _dispatch_call(starts, npass, h, lpos, cap):
    t, d = h.shape
    n_tiles = t // ROUTE_TILE
    grid_spec = pltpu.PrefetchScalarGridSpec(
        num_scalar_prefetch=2,
        grid=(n_tiles,),
        in_specs=[pl.BlockSpec((ROUTE_TILE, d), lambda i, *_: (i, 0)),
                  pl.BlockSpec((N_EXPERTS, ROUTE_TILE), lambda i, *_: (0, i))],
        out_specs=pl.BlockSpec(memory_space=pl.ANY),
        scratch_shapes=[pltpu.VMEM((2, N_EXPERTS * SLAB_ROWS, LANES), F32),
                        pltpu.SemaphoreType.DMA((2, N_EXPERTS)),
                        pltpu.SMEM((2,), I32)],
    )
    return pl.pallas_call(
        functools.partial(_dispatch_kernel, cap=cap),
        name="dispatch",
        grid_spec=grid_spec,
        out_shape=jax.ShapeDtypeStruct((N_EXPERTS, (cap + SLAB) * COL_BLOCKS, LANES), F32),
        compiler_params=_cparams("arbitrary"),
    )(starts, npass, h, lpos)


def _expert_kernel(x_ref, wg_hbm, wu_hbm, wd_hbm, o_ref, wgb, wub, wdb, sg, su, sd, sem, *, layer, n_chunks):
    e = pl.program_id(0)
    j = pl.program_id(1)
    n_experts = pl.num_programs(0)
    d_rows = D_MODEL // n_chunks
    f_rows = EXPERT_FF // n_chunks

    def chunk_copies(ex, c):
        slot = (ex * n_chunks + c) % 2
        return [pltpu.make_async_copy(wg_hbm.at[layer, ex, pl.ds(c * d_rows, d_rows)], sg.at[slot], sem.at[0, slot]),
                pltpu.make_async_copy(wu_hbm.at[layer, ex, pl.ds(c * d_rows, d_rows)], su.at[slot], sem.at[1, slot]),
                pltpu.make_async_copy(wd_hbm.at[layer, ex, pl.ds(c * f_rows, f_rows)], sd.at[slot], sem.at[2, slot])]

    def start(ex, c):
        for cp in chunk_copies(ex, c):
            cp.start()

    def finish(ex, c):
        for cp in chunk_copies(ex, c):
            cp.wait()
        slot = (ex * n_chunks + c) % 2
        to = ex % 2
        wgb[to, pl.ds(c * d_rows, d_rows), :] = sg[slot].astype(BF16)
        wub[to, pl.ds(c * d_rows, d_rows), :] = su[slot].astype(BF16)
        wdb[to, pl.ds(c * f_rows, f_rows), :] = sd[slot].astype(BF16)

    @pl.when((e == 0) & (j == 0))
    def _():
        for c in range(n_chunks):
            start(0, c)
            finish(0, c)
        start(1, 0)

    nxt = (e + 1) % n_experts

    @pl.when(j + 1 < n_chunks)
    def _():
        start(nxt, j + 1)

    @pl.when((j + 1 == n_chunks) & (e + 1 < n_experts))
    def _():
        start((e + 2) % n_experts, 0)

    finish(nxt, j)

    cur = e % 2
    tm = x_ref.shape[1] // COL_BLOCKS
    x = jnp.concatenate([x_ref[0, pl.ds(cb, tm, stride=COL_BLOCKS), :] for cb in range(COL_BLOCKS)],
                        axis=1).astype(BF16)
    acc = jnp.zeros((tm, D_MODEL), F32)
    for fc in range(EXPERT_FF // FF_CHUNK):
        fs = slice(fc * FF_CHUNK, (fc + 1) * FF_CHUNK)
        g = jnp.dot(x, wgb[cur, :, fs], preferred_element_type=F32)
        u = jnp.dot(x, wub[cur, :, fs], preferred_element_type=F32)
        hmid = (g * (1.0 / (1.0 + jnp.exp(-g))) * u).astype(BF16)
        acc = acc + jnp.dot(hmid, wdb[cur, fs, :], preferred_element_type=F32)
    for cb in range(COL_BLOCKS):
        o_ref[0, cb] = acc[:, cb * LANES:(cb + 1) * LANES]


def _expert_call(xe, wg, wu, wd, layer, cap):
    tm = min(ROW_TILE, cap)
    n_chunks = cap // tm
    d, f = wg.shape[2], wg.shape[3]
    assert d % n_chunks == 0 and f % n_chunks == 0
    return pl.pallas_call(
        functools.partial(_expert_kernel, layer=layer, n_chunks=n_chunks),
        name="expert",
        grid=(N_EXPERTS, n_chunks),
        in_specs=[pl.BlockSpec((1, tm * COL_BLOCKS, LANES), lambda e, j: (e, j, 0)),
                  pl.BlockSpec(memory_space=pl.ANY),
                  pl.BlockSpec(memory_space=pl.ANY),
                  pl.BlockSpec(memory_space=pl.ANY)],
        out_specs=pl.BlockSpec((1, COL_BLOCKS, tm, LANES), lambda e, j: (e, 0, j, 0)),
        out_shape=jax.ShapeDtypeStruct((N_EXPERTS, COL_BLOCKS, cap, LANES), F32),
        scratch_shapes=[pltpu.VMEM((2, d, f), BF16), pltpu.VMEM((2, d, f), BF16), pltpu.VMEM((2, f, d), BF16),
                        pltpu.VMEM((2, d // n_chunks, f), F32), pltpu.VMEM((2, d // n_chunks, f), F32),
                        pltpu.VMEM((2, f // n_chunks, d), F32),
                        pltpu.SemaphoreType.DMA((3, 2))],
        compiler_params=_cparams("arbitrary", "arbitrary"),
    )(xe, wg, wu, wd)


def _combine_kernel(starts_sm, npass_sm, x_ref, aff_ref, lpost_ref, expand_ref, g_ref, ye_ref, o_ref,
                    slab_sc, sem, *, cap, final):
    i = pl.program_id(0)
    lane = lax.broadcasted_iota(I32, (1, LANES), 1)
    slot_in_slab = (lax.broadcasted_iota(I32, (1, N_EXPERTS * READ_SLAB), 1) % READ_SLAB).astype(F32)
    aff = aff_ref[...]
    ghi = aff.astype(BF16)
    glo = (aff - ghi.astype(F32)).astype(BF16)
    expand = expand_ref[...]
    lpost = lpost_ref[...]
    buf = i % 2

    def slab_src(tile, s, e):
        want = starts_sm[tile * N_EXPERTS + e] + SLAB * s
        src = jnp.minimum((want // SUBLANES) * SUBLANES, cap - READ_SLAB)
        return want, pl.multiple_of(src, SUBLANES)

    def slab_copy(tile, s, to, e):
        src = slab_src(tile, s, e)[1]
        return pltpu.make_async_copy(
            ye_ref.at[e, :, pl.ds(src, READ_SLAB), :],
            slab_sc.at[to, :, pl.ds(e * READ_SLAB, READ_SLAB), :],
            sem.at[to, e])

    def fetch(tile, s, to):
        for e in range(N_EXPERTS):
            slab_copy(tile, s, to, e).start()

    def slab_row(s):
        shift = jnp.zeros((1, LANES), F32)
        for e in range(N_EXPERTS):
            want, src = slab_src(i, s, e)
            shift = jnp.where(lane == e, (want - src).astype(F32), shift)
        lo = jnp.asarray(SLAB * s, F32)
        owned = (lpost >= lo) & (lpost < lo + SLAB)
        return jnp.where(owned, lpost - lo + shift, -1.0).astype(BF16)

    spread = jnp.dot(jnp.concatenate([ghi, glo, slab_row(0)], axis=0), expand, preferred_element_type=F32)
    ghi_x = spread[0:ROUTE_TILE]
    glo_x = spread[ROUTE_TILE:2 * ROUTE_TILE]

    def add_pass(s, row_x, acc):
        match = row_x == slot_in_slab
        w = jnp.concatenate([jnp.where(match, ghi_x, 0.0).astype(BF16),
                             jnp.where(match, glo_x, 0.0).astype(BF16)], axis=0)
        for e in range(N_EXPERTS):
            slab_copy(i, s, buf, e).wait()
        rows = jnp.concatenate([slab_sc[buf, cb] for cb in range(COL_BLOCKS)],
                               axis=1).astype(BF16)
        both = jnp.dot(w, rows, preferred_element_type=F32)
        return acc + (both[0:ROUTE_TILE] + both[ROUTE_TILE:2 * ROUTE_TILE])

    @pl.when(i == 0)
    def _():
        fetch(0, 0, 0)

    @pl.when(i + 1 < pl.num_programs(0))
    def _():
        fetch(i + 1, 0, 1 - buf)

    y = add_pass(0, spread[2 * ROUTE_TILE:3 * ROUTE_TILE], jnp.zeros(x_ref.shape, F32))

    def extra_pass(s, acc):
        fetch(i, s, buf)
        return add_pass(s, jnp.dot(slab_row(s), expand, preferred_element_type=F32), acc)

    y = lax.fori_loop(1, npass_sm[i], extra_pass, y)
    out = x_ref[...] + y
    if final:
        out = _rmsnorm(out, g_ref[...])
    o_ref[...] = out


def _combine_call(starts, npass, x1, aff, lpost, expand, g, ye, cap, final):
    t, d = x1.shape
    n_tiles = t // ROUTE_TILE
    grid_spec = pltpu.PrefetchScalarGridSpec(
        num_scalar_prefetch=2,
        grid=(n_tiles,),
        in_specs=[pl.BlockSpec((ROUTE_TILE, d), lambda i, *_: (i, 0)),
                  pl.BlockSpec((ROUTE_TILE, LANES), lambda i, *_: (i, 0)),
                  pl.BlockSpec((ROUTE_TILE, LANES), lambda i, *_: (i, 0)),
                  pl.BlockSpec((LANES, N_EXPERTS * READ_SLAB), lambda i, *_: (0, 0)),
                  pl.BlockSpec((1, d), lambda i, *_: (0, 0)),
                  pl.BlockSpec(memory_space=pl.ANY)],
        out_specs=pl.BlockSpec((ROUTE_TILE, d), lambda i, *_: (i, 0)),
        scratch_shapes=[pltpu.VMEM((2, COL_BLOCKS, N_EXPERTS * READ_SLAB, LANES), F32),
                        pltpu.SemaphoreType.DMA((2, N_EXPERTS))],
    )
    return pl.pallas_call(
        functools.partial(_combine_kernel, cap=cap, final=final),
        name="combine",
        grid_spec=grid_spec,
        out_shape=jax.ShapeDtypeStruct((t, d), F32),
        compiler_params=_cparams("arbitrary"),
    )(starts, npass, x1, aff, lpost, expand, g, ye)


def _moe(x1, h, aff, afft, wg, wu, wd, layer, expand, g_final, final):
    t = x1.shape[0]
    cap = 2 * t // N_EXPERTS
    assert t % ROUTE_TILE == 0 and cap >= READ_SLAB and cap % SUBLANES == 0
    thr, need = _threshold_call(afft, cap)
    lpos, lpost, starts3, npass3 = _positions_call(afft, thr, need)
    starts = starts3[:, :, 0].reshape(-1)
    npass = npass3[:, 0, 0]
    xe = _dispatch_call(starts, npass, h, lpos, cap)
    ye = _expert_call(xe, wg, wu, wd, layer, cap)
    return _combine_call(starts, npass, x1, aff, lpost, expand, g_final, ye, cap, final)


def _trunk(x, p):
    batch, seq, d = x.shape
    x = x.reshape(batch * seq, d)
    depth = p["norm_mix"].shape[0]
    for i in range(depth):
        j = i // 2
        if i % 2 == 0:
            qk, vt = _qkv_call(x, p["norm_mix"][i:i + 1], p["a_wqkv"][j], N_PLANES)
            o = _attn_call("na_attn", qk, vt, p["a_bias"][j], None, batch, seq,
                           NA_UNIT, NA_UNITS_PER_STEP, N_PLANES)
            wo = p["a_wo"][j]
        else:
            qk, vt = _qkv_call(x, p["norm_mix"][i:i + 1], p["b_wqkv"][j], 2)
            o = _attn_call("sw_attn", qk, vt, p["b_bias"], p["b_sink"][j], batch, seq,
                           SW_UNIT, SW_UNITS_PER_STEP, 2)
            wo = p["b_wo"][j]
        x1, h, aff, afft = _post_call(o, x, wo, p["norm_ffn"][i:i + 1], p["wr_hi"][i], p["wr_hl"][i])
        x = _moe(x1, h, aff, afft, p["w_gate"], p["w_up"], p["w_down"], i,
                 p["expand"], p["norm_final"], i == depth - 1)
    return x.reshape(batch, seq, d)


def _prepare(norm_mix, norm_ffn, norm_final, a_wqkv, a_wo, a_rpb, b_wqkv, b_wo, b_sink,
             w_router, w_gate, w_up, w_down):
    d = D_MODEL
    scale = HEAD_DIM ** -0.5 * LOG2E
    a_w = jnp.concatenate([a_wqkv[:, :, :d] * scale, a_wqkv[:, :, d:]], axis=2).astype(BF16)
    order = SW_HEAD_ORDER
    nb = b_wqkv.shape[0]
    bq = (b_wqkv[:, :, :d] * scale).reshape(nb, d, N_HEADS, HEAD_DIM)[:, :, order].reshape(nb, d, d)
    b_w = jnp.concatenate([bq, b_wqkv[:, :, d:]], axis=2).astype(BF16)
    b_wo_p = b_wo.reshape(nb, N_HEADS, HEAD_DIM, d)[:, order].reshape(nb, d, d).astype(BF16)
    sink = (b_sink.astype(F32) * LOG2E)[:, order].reshape(nb, 2, 1, 8, 1)
    sink = jnp.broadcast_to(sink, (nb, 2, SUBLANES, 8, SW_UNIT)).reshape(nb, 2, SUBLANES, 8 * SW_UNIT)
    wr = w_router.astype(F32).transpose(0, 2, 1)
    wr_hi = wr.astype(BF16)
    wr_lo = (wr - wr_hi.astype(F32)).astype(BF16)
    wr_hl = jnp.concatenate([wr_hi, wr_lo], axis=1)
    lane_expert = np.arange(N_EXPERTS * READ_SLAB) // READ_SLAB
    expand = jnp.asarray(np.arange(LANES)[:, None] == lane_expert[None, :], dtype=BF16)
    return {
        "norm_mix": norm_mix.astype(F32), "norm_ffn": norm_ffn.astype(F32),
        "norm_final": norm_final.astype(F32).reshape(1, d),
        "a_wqkv": a_w, "a_wo": a_wo.astype(BF16),
        "a_bias": jnp.stack([_na_bias_table(a_rpb[j] * LOG2E) for j in range(a_rpb.shape[0])]),
        "b_wqkv": b_w, "b_wo": b_wo_p, "b_bias": _sw_bias_table(), "b_sink": sink,
        "wr_hi": wr_hi, "wr_hl": wr_hl,
        "w_gate": w_gate.astype(F32), "w_up": w_up.astype(F32), "w_down": w_down.astype(F32),
        "expand": expand,
    }


def kernel(x_prompt, x_sample, norm_mix, norm_ffn, norm_final, a_wqkv, a_wo, a_rpb, b_wqkv, b_wo, b_sink,
           w_router, w_gate, w_up, w_down):
    p = _prepare(norm_mix, norm_ffn, norm_final, a_wqkv, a_wo, a_rpb, b_wqkv, b_wo, b_sink,
                 w_router, w_gate, w_up, w_down)
    return (_trunk(x_prompt, p), _trunk(x_sample, p))
```

```python
import functools

import numpy as np
import jax
import jax.numpy as jnp
from jax import lax
from jax.experimental import pallas as pl
from jax.experimental.pallas import tpu as pltpu

F32 = jnp.float32
BF16 = jnp.bfloat16
I32 = jnp.int32

D_MODEL = 1024
HEAD_DIM = 64
N_HEADS = 16
LANES = 128
SUBLANES = 8
N_PLANES = D_MODEL // LANES
COL_BLOCKS = D_MODEL // LANES
GRID_W = 64
NA_KH = 8
NA_KW = 16
NA_UNIT = 4 * GRID_W
NA_UNITS_PER_STEP = 32
SW_KV_HEADS = 4
SW_WINDOW = 128
SW_UNIT = 128
SW_UNITS_PER_STEP = 32
N_EXPERTS = 16
EXPERT_FF = 2048
FF_CHUNK = 512
ROUTE_TILE = 256
POSITION_TILES_PER_STEP = 8
SLAB = 56
RMS_EPS = 1e-6
NEG = -1e30
LOG2E = 1.4426950408889634
ROW_TILE = 512
DENSE_ROW_TILE = 1024
VMEM_LIMIT = 56 * 1024 * 1024


def _cparams(*sem):
    return pltpu.CompilerParams(dimension_semantics=sem, vmem_limit_bytes=VMEM_LIMIT)


def _rmsnorm(x, g):
    return x * lax.rsqrt(jnp.mean(x * x, axis=-1, keepdims=True) + RMS_EPS) * g


def _qkv_kernel(x_ref, g_ref, w_ref, qk_ref, vt_ref):
    y = _rmsnorm(x_ref[...], g_ref[...]).astype(BF16)
    r = jnp.dot(y, w_ref[...], preferred_element_type=F32)
    n_qk = qk_ref.shape[0]
    for p in range(n_qk):
        qk_ref[p] = r[:, p * LANES:(p + 1) * LANES].astype(BF16)
    for p in range(vt_ref.shape[0]):
        vt_ref[p] = r[:, (n_qk + p) * LANES:(n_qk + p + 1) * LANES].T.astype(BF16)


def _qkv_call(x, g, w, n_v_planes):
    t, d = x.shape
    n_qk = w.shape[1] // LANES - n_v_planes
    tm = min(DENSE_ROW_TILE, t)
    return pl.pallas_call(
        _qkv_kernel,
        name="qkv",
        grid=(t // tm,),
        in_specs=[pl.BlockSpec((tm, d), lambda i: (i, 0)),
                  pl.BlockSpec((1, d), lambda i: (0, 0)),
                  pl.BlockSpec((d, w.shape[1]), lambda i: (0, 0))],
        out_specs=[pl.BlockSpec((n_qk, tm, LANES), lambda i: (0, i, 0)),
                   pl.BlockSpec((n_v_planes, LANES, tm), lambda i: (0, 0, i))],
        out_shape=[jax.ShapeDtypeStruct((n_qk, t, LANES), BF16),
                   jax.ShapeDtypeStruct((n_v_planes, LANES, t), BF16)],
        compiler_params=_cparams("parallel"),
    )(x, g, w)


def _edge_case(u, n_units):
    i = pl.program_id(2)
    last = pl.num_programs(2) - 1
    if n_units == 1:
        return jnp.where(i == 0, 1, jnp.where(i == last, 2, 0))
    if u == 0:
        return jnp.where(i == 0, 1, 0)
    if u == n_units - 1:
        return jnp.where(i == last, 2, 0)
    return 0


def _attn_kernel(*refs, unit, n_units, has_sink):
    q_ref, kp_ref, kc_ref, kn_ref, vp_ref, vc_ref, vn_ref, bias_ref = refs[:8]
    sink_ref = refs[8] if has_sink else None
    o_ref = refs[-1]
    n_planes = q_ref.shape[0]
    kcat = jnp.concatenate([kp_ref[0], kc_ref[0], kn_ref[0]], axis=0)
    vcat = jnp.concatenate([vp_ref[0], vc_ref[0], vn_ref[0]], axis=1)
    vcat = jnp.concatenate([vcat, jnp.ones((SUBLANES, vcat.shape[1]), BF16)], axis=0)
    lo_half = lax.broadcasted_iota(I32, (unit, LANES), 1) < HEAD_DIM
    zero = jnp.zeros((unit, LANES), BF16)

    def scores(u):
        kw = kcat[u * unit:(u + 3) * unit]
        qs = [q_ref[pi, u * unit:(u + 1) * unit, :] for pi in range(n_planes)]
        q_stack = jnp.concatenate(
            [jnp.where(lo_half if hh == 0 else jnp.logical_not(lo_half), q, zero) for q in qs for hh in range(2)],
            axis=0)
        return lax.dot_general(kw, q_stack, (((1,), (1,)), ((), ())), preferred_element_type=F32)

    st_next = scores(0)
    for u in range(n_units):
        case = _edge_case(u, n_units)
        vw = vcat[:, u * unit:(u + 3) * unit]
        st = st_next + bias_ref[0, case]
        if u + 1 < n_units:
            st_next = scores(u + 1)
        m = jnp.max(st, axis=0, keepdims=True)
        if has_sink:
            sink = sink_ref[0, 0:1, :]
            m = jnp.maximum(m, sink)
        e = jnp.exp2(st - m)
        ot = jnp.dot(vw, e.astype(BF16), preferred_element_type=F32)
        l = ot[LANES:LANES + 1, :]
        if has_sink:
            l = l + jnp.exp2(sink - m)
        ot = ot[0:LANES, :] / l
        for pi in range(n_planes):
            both = jnp.concatenate(
                [ot[0:HEAD_DIM, (2 * pi) * unit:(2 * pi + 1) * unit],
                 ot[HEAD_DIM:LANES, (2 * pi + 1) * unit:(2 * pi + 2) * unit]], axis=0)
            o_ref[pi, u * unit:(u + 1) * unit, :] = both.T.astype(BF16)


def _attn_call(name, qk, vt, bias, sink, batch, seq, unit, n_units, n_groups):
    t = batch * seq
    units = seq // unit
    n_units = min(n_units, units)
    steps = units // n_units
    assert seq % (unit * n_units) == 0 and units >= 4
    ppg = N_PLANES // n_groups
    qb = unit * n_units
    width = 2 * ppg * unit

    def cur(g, b, i):
        return b * steps + i

    def prev(g, b, i):
        return b * units + jnp.maximum(n_units * i - 1, 0)

    def nxt(g, b, i):
        return b * units + jnp.minimum(n_units * i + n_units, units - 1)

    in_specs = [pl.BlockSpec((ppg, qb, LANES), lambda g, b, i: (g, cur(g, b, i), 0)),
                pl.BlockSpec((1, unit, LANES), lambda g, b, i: (N_PLANES + g, prev(g, b, i), 0)),
                pl.BlockSpec((1, qb, LANES), lambda g, b, i: (N_PLANES + g, cur(g, b, i), 0)),
                pl.BlockSpec((1, unit, LANES), lambda g, b, i: (N_PLANES + g, nxt(g, b, i), 0)),
                pl.BlockSpec((1, LANES, unit), lambda g, b, i: (g, 0, prev(g, b, i))),
                pl.BlockSpec((1, LANES, qb), lambda g, b, i: (g, 0, cur(g, b, i))),
                pl.BlockSpec((1, LANES, unit), lambda g, b, i: (g, 0, nxt(g, b, i))),
                pl.BlockSpec((1, 3, 3 * unit, width), lambda g, b, i: (g, 0, 0, 0))]
    args = [qk, qk, qk, qk, vt, vt, vt, bias]
    if sink is not None:
        in_specs.append(pl.BlockSpec((1, SUBLANES, width), lambda g, b, i: (g, 0, 0)))
        args.append(sink)
    return pl.pallas_call(
        functools.partial(_attn_kernel, unit=unit, n_units=n_units, has_sink=sink is not None),
        name=name,
        grid=(n_groups, batch, steps),
        in_specs=in_specs,
        out_specs=pl.BlockSpec((ppg, qb, LANES), lambda g, b, i: (g, cur(g, b, i), 0)),
        out_shape=jax.ShapeDtypeStruct((N_PLANES, t, LANES), BF16),
        compiler_params=_cparams("arbitrary", "arbitrary", "arbitrary"),
    )(*args)


def _na_bias_table(rpb):
    a = np.arange(NA_UNIT) // GRID_W
    j = np.arange(3 * NA_UNIT) // GRID_W
    row_ok = np.stack([
        (j[None, :] >= a[:, None]) & (j[None, :] < a[:, None] + NA_KH),
        np.broadcast_to((j[None, :] >= 4) & (j[None, :] < 4 + NA_KH), (NA_UNIT, 3 * NA_UNIT)),
        np.broadcast_to(j[None, :] < NA_KH, (NA_UNIT, 3 * NA_UNIT)),
    ])
    w = np.arange(GRID_W)
    dc = np.clip(w[None, :] - w[:, None], -(NA_KW - 1), NA_KW - 1) + NA_KW - 1
    c0 = np.clip(w - NA_KW // 2, 0, GRID_W - NA_KW)
    col_ok = (w[None, :] >= c0[:, None]) & (w[None, :] < c0[:, None] + NA_KW)
    cols = jnp.take(rpb.astype(F32), jnp.asarray(dc.reshape(-1)), axis=2)
    cols = cols.reshape(N_HEADS, 2 * NA_KH - 1, GRID_W, GRID_W)
    cols = jnp.where(jnp.asarray(col_ok), cols, NEG)
    rows = jnp.stack([cols[:, 3 - ai:15 - ai] for ai in range(4)], axis=1)
    tbl = rows.transpose(0, 2, 4, 1, 3).reshape(N_HEADS, 3 * NA_UNIT, NA_UNIT)
    ok = jnp.asarray(row_ok.transpose(0, 2, 1))
    full = jnp.where(ok[:, None], tbl[None], NEG)
    full = full.reshape(3, N_PLANES, 2, 3 * NA_UNIT, NA_UNIT).transpose(1, 0, 3, 2, 4)
    return full.reshape(N_PLANES, 3, 3 * NA_UNIT, 2 * NA_UNIT)


SW_HEAD_ORDER = np.array([8 * m + 4 * hh + i for m in range(2) for i in range(4) for hh in range(2)])


def _sw_bias_table():
    slopes = np.asarray(2.0 ** (-8.0 * np.arange(1, N_HEADS + 1) / N_HEADS), dtype=np.float32)
    ks = np.arange(3 * SW_UNIT) - SW_WINDOW
    dist = ks[None, :] - np.arange(SW_UNIT)[:, None]
    win_ok = np.abs(dist) <= SW_WINDOW
    pos_ok = np.stack([np.ones(3 * SW_UNIT, bool), ks >= 0, ks < SW_UNIT])
    ok = win_ok[None] & pos_ok[:, None, :]
    alibi = -slopes[SW_HEAD_ORDER][:, None, None] * np.abs(dist).astype(np.float32)[None]
    alibi = alibi * np.float32(LOG2E)
    full = np.where(ok[:, None], alibi[None], np.float32(NEG)).astype(np.float32)
    full = full.reshape(3, 2, 8, SW_UNIT, 3 * SW_UNIT).transpose(1, 0, 4, 2, 3)
    return jnp.asarray(full.reshape(2, 3, 3 * SW_UNIT, 8 * SW_UNIT))


def _post_kernel(o_ref, x_ref, wo_ref, g_ref, wrh_ref, wrhl_ref, x1_ref, h_ref, aff_ref, afft_ref):
    oc = jnp.concatenate([o_ref[p] for p in range(N_PLANES)], axis=1)
    x1 = x_ref[...] + jnp.dot(oc, wo_ref[...], preferred_element_type=F32)
    x1_ref[...] = x1
    h = _rmsnorm(x1, g_ref[...])
    hb = h.astype(BF16)
    h_ref[...] = hb
    hl = (h - hb.astype(F32)).astype(BF16)
    nt = (((1,), (1,)), ((), ()))
    by_hb = lax.dot_general(wrhl_ref[...], hb, nt, preferred_element_type=F32)
    by_hl = lax.dot_general(wrh_ref[...], hl, nt, preferred_element_type=F32)
    logits = by_hb[0:N_EXPERTS] + (by_hb[N_EXPERTS:2 * N_EXPERTS] + by_hl)
    ex = jnp.exp(logits - jnp.max(logits, axis=0, keepdims=True))
    afft = ex / jnp.sum(ex, axis=0, keepdims=True)
    afft_ref[...] = afft
    pad = jnp.zeros((LANES - N_EXPERTS, afft.shape[1]), F32)
    aff_ref[...] = jnp.concatenate([afft, pad], axis=0).T


def _post_call(o, x, wo, g, wrh, wrhl):
    t, d = x.shape
    tm = min(DENSE_ROW_TILE, t)
    return pl.pallas_call(
        _post_kernel,
        name="post",
        grid=(t // tm,),
        in_specs=[pl.BlockSpec((N_PLANES, tm, LANES), lambda i: (0, i, 0)),
                  pl.BlockSpec((tm, d), lambda i: (i, 0)),
                  pl.BlockSpec((d, d), lambda i: (0, 0)),
                  pl.BlockSpec((1, d), lambda i: (0, 0)),
                  pl.BlockSpec((N_EXPERTS, d), lambda i: (0, 0)),
                  pl.BlockSpec((2 * N_EXPERTS, d), lambda i: (0, 0))],
        out_specs=[pl.BlockSpec((tm, d), lambda i: (i, 0)),
                   pl.BlockSpec((tm, d), lambda i: (i, 0)),
                   pl.BlockSpec((tm, LANES), lambda i: (i, 0)),
                   pl.BlockSpec((N_EXPERTS, tm), lambda i: (0, i))],
        out_shape=[jax.ShapeDtypeStruct((t, d), F32),
                   jax.ShapeDtypeStruct((t, d), BF16),
                   jax.ShapeDtypeStruct((t, LANES), F32),
                   jax.ShapeDtypeStruct((N_EXPERTS, t), F32)],
        compiler_params=_cparams("parallel"),
    )(o, x, wo, g, wrh, wrhl)


def _threshold_kernel(aff_ref, thr_ref, need_ref, *, cap):
    bits = lax.bitcast_convert_type(aff_ref[...], I32)

    def body(k, cur):
        cand = cur | jnp.left_shift(jnp.int32(1), 30 - k)
        cnt = jnp.sum(jnp.where(bits >= cand, 1, 0), axis=1, keepdims=True)
        return jnp.where(cnt >= cap, cand, cur)

    cur = lax.fori_loop(0, 31, body, jnp.zeros((N_EXPERTS, 1), I32))
    above = jnp.sum(jnp.where(bits > cur, 1, 0), axis=1, keepdims=True)
    thr_ref[...] = jnp.broadcast_to(cur, thr_ref.shape)
    need_ref[...] = jnp.broadcast_to(cap - above, need_ref.shape)


def _threshold_call(afft, cap):
    e, t = afft.shape
    return pl.pallas_call(
        functools.partial(_threshold_kernel, cap=cap),
        name="threshold",
        grid=(1,),
        in_specs=[pl.BlockSpec((e, t), lambda i: (0, 0))],
        out_specs=[pl.BlockSpec((e, LANES), lambda i: (0, 0)),
                   pl.BlockSpec((e, LANES), lambda i: (0, 0))],
        out_shape=[jax.ShapeDtypeStruct((e, LANES), I32),
                   jax.ShapeDtypeStruct((e, LANES), I32)],
        compiler_params=_cparams("arbitrary"),
    )(afft)


def _positions_kernel(aff_ref, thr_ref, need_ref, lpos_ref, lpost_ref, starts_ref, npass_ref, base_sc, eqb_sc):
    @pl.when(pl.program_id(0) == 0)
    def _():
        base_sc[...] = jnp.zeros_like(base_sc)
        eqb_sc[...] = jnp.zeros_like(eqb_sc)

    thr = thr_ref[:, 0:1]
    need = need_ref[:, 0:1].astype(F32)
    r = lax.broadcasted_iota(I32, (ROUTE_TILE, ROUTE_TILE), 0)
    c = lax.broadcasted_iota(I32, (ROUTE_TILE, ROUTE_TILE), 1)
    before = jnp.where(r < c, 1.0, 0.0).astype(BF16)
    pad = jnp.full((LANES - N_EXPERTS, ROUTE_TILE), -1.0, F32)
    base = base_sc[...]
    eqb = eqb_sc[...]
    for k in range(starts_ref.shape[0]):
        cols = slice(k * ROUTE_TILE, (k + 1) * ROUTE_TILE)
        bits = lax.bitcast_convert_type(aff_ref[:, cols], I32)
        eq = bits == thr
        eqf = jnp.where(eq, 1.0, 0.0)
        eq_rank = eqb[:, 0:1] + jnp.dot(eqf.astype(BF16), before, preferred_element_type=F32)
        sel = (bits > thr) | (eq & (eq_rank < need))
        self_ = jnp.where(sel, 1.0, 0.0)
        rank = jnp.dot(self_.astype(BF16), before, preferred_element_type=F32)
        cnt = jnp.sum(self_, axis=1, keepdims=True)
        lpos = jnp.where(sel, rank, -1.0)
        lpos_ref[:, cols] = lpos.astype(I32)
        lpost_ref[cols, :] = jnp.concatenate([lpos, pad], axis=0).T
        starts_ref[k] = base.astype(I32)
        most = jnp.max(cnt, axis=0, keepdims=True)
        passes = sum(jnp.where(most > float(first), 1.0, 0.0) for first in range(0, ROUTE_TILE, SLAB))
        npass_ref[k] = jnp.broadcast_to(passes, npass_ref.shape[1:]).astype(I32)
        base = base + cnt
        eqb = eqb + jnp.sum(eqf, axis=1, keepdims=True)
    base_sc[...] = base
    eqb_sc[...] = eqb


def _positions_call(afft, thr, need):
    e, t = afft.shape
    n_tiles = t // ROUTE_TILE
    per_step = min(POSITION_TILES_PER_STEP, n_tiles)
    assert n_tiles % per_step == 0
    span = per_step * ROUTE_TILE
    return pl.pallas_call(
        _positions_kernel,
        name="positions",
        grid=(n_tiles // per_step,),
        in_specs=[pl.BlockSpec((e, span), lambda i: (0, i)),
                  pl.BlockSpec((e, LANES), lambda i: (0, 0)),
                  pl.BlockSpec((e, LANES), lambda i: (0, 0))],
        out_specs=[pl.BlockSpec((e, span), lambda i: (0, i)),
                   pl.BlockSpec((span, LANES), lambda i: (i, 0)),
                   pl.BlockSpec((per_step, e, LANES), lambda i: (i, 0, 0)),
                   pl.BlockSpec((per_step, SUBLANES, LANES), lambda i: (i, 0, 0))],
        out_shape=[jax.ShapeDtypeStruct((e, t), I32),
                   jax.ShapeDtypeStruct((t, LANES), F32),
                   jax.ShapeDtypeStruct((n_tiles, e, LANES), I32),
                   jax.ShapeDtypeStruct((n_tiles, SUBLANES, LANES), I32)],
        scratch_shapes=[pltpu.VMEM((e, LANES), F32), pltpu.VMEM((e, LANES), F32)],
        compiler_params=_cparams("arbitrary"),
    )(afft, thr, need)


SLAB_ROWS = SLAB * COL_BLOCKS
READ_SLAB = SLAB + SUBLANES


def _dispatch_kernel(starts_sm, npass_sm, h_ref, lpos_ref, xe_ref, slab_sc, sem, state_sm, *, cap):
    i = pl.program_id(0)
    sub = lax.broadcasted_iota(I32, (SLAB, ROUTE_TILE), 0)

    def slab_copy(buf, e, dst):
        return pltpu.make_async_copy(
            slab_sc.at[buf, pl.ds(e * SLAB_ROWS, SLAB_ROWS)],
            xe_ref.at[e, pl.ds(pl.multiple_of(dst * COL_BLOCKS, COL_BLOCKS), SLAB_ROWS)],
            sem.at[buf, e])

    def wait_all(buf):
        for e in range(N_EXPERTS):
            slab_copy(buf, e, 0).wait()

    @pl.when(i == 0)
    def _():
        slab_sc[0, pl.ds(0, SLAB_ROWS), :] = jnp.zeros((SLAB_ROWS, LANES), F32)
        pads = [pltpu.make_async_copy(slab_sc.at[0, pl.ds(0, SLAB_ROWS)],
                                      xe_ref.at[e, pl.ds(cap * COL_BLOCKS, SLAB_ROWS)], sem.at[0, e])
                for e in range(N_EXPERTS)]
        for cp in pads:
            cp.start()
        for cp in pads:
            cp.wait()
        state_sm[0] = 0
        state_sm[1] = 0

    def one_pass(s, carry):
        buf = state_sm[1]
        rows = []
        for e in range(N_EXPERTS):
            lp = lpos_ref[e:e + 1, :] - SLAB * s
            rows.append(jnp.where(lp == sub, 1.0, 0.0).astype(BF16))
        onehot = jnp.concatenate(rows, axis=0)
        res = jnp.dot(onehot, h_ref[...], preferred_element_type=F32)
        for cb in range(COL_BLOCKS):
            slab_sc[buf, pl.ds(cb, N_EXPERTS * SLAB, stride=COL_BLOCKS), :] = res[:, cb * LANES:(cb + 1) * LANES]

        @pl.when(state_sm[0] == 1)
        def _():
            wait_all(1 - buf)

        for e in range(N_EXPERTS):
            dst = jnp.minimum(starts_sm[i * N_EXPERTS + e] + SLAB * s, cap)
            slab_copy(buf, e, dst).start()
        state_sm[0] = 1
        state_sm[1] = 1 - buf
        return carry

    lax.fori_loop(0, npass_sm[i], one_pass, 0)

    @pl.when((i == pl.num_programs(0) - 1) & (state_sm[0] == 1))
    def _():
        wait_all(1 - state_sm[1])


def _dispatch_call(starts, npass, h, lpos, cap):
    t, d = h.shape
    n_tiles = t // ROUTE_TILE
    grid_spec = pltpu.PrefetchScalarGridSpec(
        num_scalar_prefetch=2,
        grid=(n_tiles,),
        in_specs=[pl.BlockSpec((ROUTE_TILE, d), lambda i, *_: (i, 0)),
                  pl.BlockSpec((N_EXPERTS, ROUTE_TILE), lambda i, *_: (0, i))],
        out_specs=pl.BlockSpec(memory_space=pl.ANY),
        scratch_shapes=[pltpu.VMEM((2, N_EXPERTS * SLAB_ROWS, LANES), F32),
                        pltpu.SemaphoreType.DMA((2, N_EXPERTS)),
                        pltpu.SMEM((2,), I32)],
    )
    return pl.pallas_call(
        functools.partial(_dispatch_kernel, cap=cap),
        name="dispatch",
        grid_spec=grid_spec,
        out_shape=jax.ShapeDtypeStruct((N_EXPERTS, (cap + SLAB) * COL_BLOCKS, LANES), F32),
        compiler_params=_cparams("arbitrary"),
    )(starts, npass, h, lpos)


def _expert_kernel(x_ref, wg_hbm, wu_hbm, wd_hbm, o_ref, wgb, wub, wdb, sg, su, sd, sem, *, layer, n_chunks):
    e = pl.program_id(0)
    j = pl.program_id(1)
    n_experts = pl.num_programs(0)
    d_rows = D_MODEL // n_chunks
    f_rows = EXPERT_FF // n_chunks

    def chunk_copies(ex, c):
        slot = (ex * n_chunks + c) % 2
        return [pltpu.make_async_copy(wg_hbm.at[layer, ex, pl.ds(c * d_rows, d_rows)], sg.at[slot], sem.at[0, slot]),
                pltpu.make_async_copy(wu_hbm.at[layer, ex, pl.ds(c * d_rows, d_rows)], su.at[slot], sem.at[1, slot]),
                pltpu.make_async_copy(wd_hbm.at[layer, ex, pl.ds(c * f_rows, f_rows)], sd.at[slot], sem.at[2, slot])]

    def start(ex, c):
        for cp in chunk_copies(ex, c):
            cp.start()

    def finish(ex, c):
        for cp in chunk_copies(ex, c):
            cp.wait()
        slot = (ex * n_chunks + c) % 2
        to = ex % 2
        wgb[to, pl.ds(c * d_rows, d_rows), :] = sg[slot].astype(BF16)
        wub[to, pl.ds(c * d_rows, d_rows), :] = su[slot].astype(BF16)
        wdb[to, pl.ds(c * f_rows, f_rows), :] = sd[slot].astype(BF16)

    @pl.when((e == 0) & (j == 0))
    def _():
        for c in range(n_chunks):
            start(0, c)
            finish(0, c)
        start(1, 0)

    nxt = (e + 1) % n_experts

    @pl.when(j + 1 < n_chunks)
    def _():
        start(nxt, j + 1)

    @pl.when((j + 1 == n_chunks) & (e + 1 < n_experts))
    def _():
        start((e + 2) % n_experts, 0)

    finish(nxt, j)

    cur = e % 2
    tm = x_ref.shape[1] // COL_BLOCKS
    x = jnp.concatenate([x_ref[0, pl.ds(cb, tm, stride=COL_BLOCKS), :] for cb in range(COL_BLOCKS)],
                        axis=1).astype(BF16)
    acc = jnp.zeros((tm, D_MODEL), F32)
    for fc in range(EXPERT_FF // FF_CHUNK):
        fs = slice(fc * FF_CHUNK, (fc + 1) * FF_CHUNK)
        g = jnp.dot(x, wgb[cur, :, fs], preferred_element_type=F32)
        u = jnp.dot(x, wub[cur, :, fs], preferred_element_type=F32)
        hmid = (g * (1.0 / (1.0 + jnp.exp(-g))) * u).astype(BF16)
        acc = acc + jnp.dot(hmid, wdb[cur, fs, :], preferred_element_type=F32)
    for cb in range(COL_BLOCKS):
        o_ref[0, cb] = acc[:, cb * LANES:(cb + 1) * LANES]


def _expert_call(xe, wg, wu, wd, layer, cap):
    tm = min(ROW_TILE, cap)
    n_chunks = cap // tm
    d, f = wg.shape[2], wg.shape[3]
    assert d % n_chunks == 0 and f % n_chunks == 0
    return pl.pallas_call(
        functools.partial(_expert_kernel, layer=layer, n_chunks=n_chunks),
        name="expert",
        grid=(N_EXPERTS, n_chunks),
        in_specs=[pl.BlockSpec((1, tm * COL_BLOCKS, LANES), lambda e, j: (e, j, 0)),
                  pl.BlockSpec(memory_space=pl.ANY),
                  pl.BlockSpec(memory_space=pl.ANY),
                  pl.BlockSpec(memory_space=pl.ANY)],
        out_specs=pl.BlockSpec((1, COL_BLOCKS, tm, LANES), lambda e, j: (e, 0, j, 0)),
        out_shape=jax.ShapeDtypeStruct((N_EXPERTS, COL_BLOCKS, cap, LANES), F32),
        scratch_shapes=[pltpu.VMEM((2, d, f), BF16), pltpu.VMEM((2, d, f), BF16), pltpu.VMEM((2, f, d), BF16),
                        pltpu.VMEM((2, d // n_chunks, f), F32), pltpu.VMEM((2, d // n_chunks, f), F32),
                        pltpu.VMEM((2, f // n_chunks, d), F32),
                        pltpu.SemaphoreType.DMA((3, 2))],
        compiler_params=_cparams("arbitrary", "arbitrary"),
    )(xe, wg, wu, wd)


def _combine_kernel(starts_sm, npass_sm, x_ref, aff_ref, lpost_ref, expand_ref, g_ref, ye_ref, o_ref,
                    slab_sc, sem, *, cap, final):
    i = pl.program_id(0)
    lane = lax.broadcasted_iota(I32, (1, LANES), 1)
    slot_in_slab = (lax.broadcasted_iota(I32, (1, N_EXPERTS * READ_SLAB), 1) % READ_SLAB).astype(F32)
    aff = aff_ref[...]
    ghi = aff.astype(BF16)
    glo = (aff - ghi.astype(F32)).astype(BF16)
    expand = expand_ref[...]
    lpost = lpost_ref[...]
    buf = i % 2

    def slab_src(tile, s, e):
        want = starts_sm[tile * N_EXPERTS + e] + SLAB * s
        src = jnp.minimum((want // SUBLANES) * SUBLANES, cap - READ_SLAB)
        return want, pl.multiple_of(src, SUBLANES)

    def slab_copy(tile, s, to, e):
        src = slab_src(tile, s, e)[1]
        return pltpu.make_async_copy(
            ye_ref.at[e, :, pl.ds(src, READ_SLAB), :],
            slab_sc.at[to, :, pl.ds(e * READ_SLAB, READ_SLAB), :],
            sem.at[to, e])

    def fetch(tile, s, to):
        for e in range(N_EXPERTS):
            slab_copy(tile, s, to, e).start()

    def slab_row(s):
        shift = jnp.zeros((1, LANES), F32)
        for e in range(N_EXPERTS):
            want, src = slab_src(i, s, e)
            shift = jnp.where(lane == e, (want - src).astype(F32), shift)
        lo = jnp.asarray(SLAB * s, F32)
        owned = (lpost >= lo) & (lpost < lo + SLAB)
        return jnp.where(owned, lpost - lo + shift, -1.0).astype(BF16)

    spread = jnp.dot(jnp.concatenate([ghi, glo, slab_row(0)], axis=0), expand, preferred_element_type=F32)
    ghi_x = spread[0:ROUTE_TILE]
    glo_x = spread[ROUTE_TILE:2 * ROUTE_TILE]

    def add_pass(s, row_x, acc):
        match = row_x == slot_in_slab
        w = jnp.concatenate([jnp.where(match, ghi_x, 0.0).astype(BF16),
                             jnp.where(match, glo_x, 0.0).astype(BF16)], axis=0)
        for e in range(N_EXPERTS):
            slab_copy(i, s, buf, e).wait()
        rows = jnp.concatenate([slab_sc[buf, cb] for cb in range(COL_BLOCKS)],
                               axis=1).astype(BF16)
        both = jnp.dot(w, rows, preferred_element_type=F32)
        return acc + (both[0:ROUTE_TILE] + both[ROUTE_TILE:2 * ROUTE_TILE])

    @pl.when(i == 0)
    def _():
        fetch(0, 0, 0)

    @pl.when(i + 1 < pl.num_programs(0))
    def _():
        fetch(i + 1, 0, 1 - buf)

    y = add_pass(0, spread[2 * ROUTE_TILE:3 * ROUTE_TILE], jnp.zeros(x_ref.shape, F32))

    def extra_pass(s, acc):
        fetch(i, s, buf)
        return add_pass(s, jnp.dot(slab_row(s), expand, preferred_element_type=F32), acc)

    y = lax.fori_loop(1, npass_sm[i], extra_pass, y)
    out = x_ref[...] + y
    if final:
        out = _rmsnorm(out, g_ref[...])
    o_ref[...] = out


def _combine_call(starts, npass, x1, aff, lpost, expand, g, ye, cap, final):
    t, d = x1.shape
    n_tiles = t // ROUTE_TILE
    grid_spec = pltpu.PrefetchScalarGridSpec(
        num_scalar_prefetch=2,
        grid=(n_tiles,),
        in_specs=[pl.BlockSpec((ROUTE_TILE, d), lambda i, *_: (i, 0)),
                  pl.BlockSpec((ROUTE_TILE, LANES), lambda i, *_: (i, 0)),
                  pl.BlockSpec((ROUTE_TILE, LANES), lambda i, *_: (i, 0)),
                  pl.BlockSpec((LANES, N_EXPERTS * READ_SLAB), lambda i, *_: (0, 0)),
                  pl.BlockSpec((1, d), lambda i, *_: (0, 0)),
                  pl.BlockSpec(memory_space=pl.ANY)],
        out_specs=pl.BlockSpec((ROUTE_TILE, d), lambda i, *_: (i, 0)),
        scratch_shapes=[pltpu.VMEM((2, COL_BLOCKS, N_EXPERTS * READ_SLAB, LANES), F32),
                        pltpu.SemaphoreType.DMA((2, N_EXPERTS))],
    )
    return pl.pallas_call(
        functools.partial(_combine_kernel, cap=cap, final=final),
        name="combine",
        grid_spec=grid_spec,
        out_shape=jax.ShapeDtypeStruct((t, d), F32),
        compiler_params=_cparams("arbitrary"),
    )(starts, npass, x1, aff, lpost, expand, g, ye)


def _moe(x1, h, aff, afft, wg, wu, wd, layer, expand, g_final, final):
    t = x1.shape[0]
    cap = 2 * t // N_EXPERTS
    assert t % ROUTE_TILE == 0 and cap >= READ_SLAB and cap % SUBLANES == 0
    thr, need = _threshold_call(afft, cap)
    lpos, lpost, starts3, npass3 = _positions_call(afft, thr, need)
    starts = starts3[:, :, 0].reshape(-1)
    npass = npass3[:, 0, 0]
    xe = _dispatch_call(starts, npass, h, lpos, cap)
    ye = _expert_call(xe, wg, wu, wd, layer, cap)
    return _combine_call(starts, npass, x1, aff, lpost, expand, g_final, ye, cap, final)


def _trunk(x, p):
    batch, seq, d = x.shape
    x = x.reshape(batch * seq, d)
    depth = p["norm_mix"].shape[0]
    for i in range(depth):
        j = i // 2
        if i % 2 == 0:
            qk, vt = _qkv_call(x, p["norm_mix"][i:i + 1], p["a_wqkv"][j], N_PLANES)
            o = _attn_call("na_attn", qk, vt, p["a_bias"][j], None, batch, seq,
                           NA_UNIT, NA_UNITS_PER_STEP, N_PLANES)
            wo = p["a_wo"][j]
        else:
            qk, vt = _qkv_call(x, p["norm_mix"][i:i + 1], p["b_wqkv"][j], 2)
            o = _attn_call("sw_attn", qk, vt, p["b_bias"], p["b_sink"][j], batch, seq,
                           SW_UNIT, SW_UNITS_PER_STEP, 2)
            wo = p["b_wo"][j]
        x1, h, aff, afft = _post_call(o, x, wo, p["norm_ffn"][i:i + 1], p["wr_hi"][i], p["wr_hl"][i])
        x = _moe(x1, h, aff, afft, p["w_gate"], p["w_up"], p["w_down"], i,
                 p["expand"], p["norm_final"], i == depth - 1)
    return x.reshape(batch, seq, d)


def _prepare(norm_mix, norm_ffn, norm_final, a_wqkv, a_wo, a_rpb, b_wqkv, b_wo, b_sink,
             w_router, w_gate, w_up, w_down):
    d = D_MODEL
    scale = HEAD_DIM ** -0.5 * LOG2E
    a_w = jnp.concatenate([a_wqkv[:, :, :d] * scale, a_wqkv[:, :, d:]], axis=2).astype(BF16)
    order = SW_HEAD_ORDER
    nb = b_wqkv.shape[0]
    bq = (b_wqkv[:, :, :d] * scale).reshape(nb, d, N_HEADS, HEAD_DIM)[:, :, order].reshape(nb, d, d)
    b_w = jnp.concatenate([bq, b_wqkv[:, :, d:]], axis=2).astype(BF16)
    b_wo_p = b_wo.reshape(nb, N_HEADS, HEAD_DIM, d)[:, order].reshape(nb, d, d).astype(BF16)
    sink = (b_sink.astype(F32) * LOG2E)[:, order].reshape(nb, 2, 1, 8, 1)
    sink = jnp.broadcast_to(sink, (nb, 2, SUBLANES, 8, SW_UNIT)).reshape(nb, 2, SUBLANES, 8 * SW_UNIT)
    wr = w_router.astype(F32).transpose(0, 2, 1)
    wr_hi = wr.astype(BF16)
    wr_lo = (wr - wr_hi.astype(F32)).astype(BF16)
    wr_hl = jnp.concatenate([wr_hi, wr_lo], axis=1)
    lane_expert = np.arange(N_EXPERTS * READ_SLAB) // READ_SLAB
    expand = jnp.asarray(np.arange(LANES)[:, None] == lane_expert[None, :], dtype=BF16)
    return {
        "norm_mix": norm_mix.astype(F32), "norm_ffn": norm_ffn.astype(F32),
        "norm_final": norm_final.astype(F32).reshape(1, d),
        "a_wqkv": a_w, "a_wo": a_wo.astype(BF16),
        "a_bias": jnp.stack([_na_bias_table(a_rpb[j] * LOG2E) for j in range(a_rpb.shape[0])]),
        "b_wqkv": b_w, "b_wo": b_wo_p, "b_bias": _sw_bias_table(), "b_sink": sink,
        "wr_hi": wr_hi, "wr_hl": wr_hl,
        "w_gate": w_gate.astype(F32), "w_up": w_up.astype(F32), "w_down": w_down.astype(F32),
        "expand": expand,
    }


def kernel(x_prompt, x_sample, norm_mix, norm_ffn, norm_final, a_wqkv, a_wo, a_rpb, b_wqkv, b_wo, b_sink,
           w_router, w_gate, w_up, w_down):
    p = _prepare(norm_mix, norm_ffn, norm_final, a_wqkv, a_wo, a_rpb, b_wqkv, b_wo, b_sink,
                 w_router, w_gate, w_up, w_down)
    return (_trunk(x_prompt, p), _trunk(x_sample, p))
```

```python
import functools

import numpy as np
import jax
import jax.numpy as jnp
from jax import lax
from jax.experimental import pallas as pl
from jax.experimental.pallas import tpu as pltpu

F32 = jnp.float32
BF16 = jnp.bfloat16
I32 = jnp.int32

D_MODEL = 1024
HEAD_DIM = 64
N_HEADS = 16
LANES = 128
SUBLANES = 8
N_PLANES = D_MODEL // LANES
COL_BLOCKS = D_MODEL // LANES
GRID_W = 64
NA_KH = 8
NA_KW = 16
NA_UNIT = 4 * GRID_W
NA_UNITS_PER_STEP = 16
SW_KV_HEADS = 4
SW_WINDOW = 128
SW_UNIT = 128
SW_UNITS_PER_STEP = 16
N_EXPERTS = 16
EXPERT_FF = 2048
FF_CHUNK = 512
ROUTE_TILE = 256
POSITION_TILES_PER_STEP = 8
SLAB = 56
RMS_EPS = 1e-6
NEG = -1e30
LOG2E = 1.4426950408889634
ROW_TILE = 512
DENSE_ROW_TILE = 1024
VMEM_LIMIT = 56 * 1024 * 1024


def _cparams(*sem):
    return pltpu.CompilerParams(dimension_semantics=sem, vmem_limit_bytes=VMEM_LIMIT)


def _rmsnorm(x, g):
    return x * lax.rsqrt(jnp.mean(x * x, axis=-1, keepdims=True) + RMS_EPS) * g


def _qkv_kernel(x_ref, g_ref, w_ref, qk_ref, vt_ref):
    y = _rmsnorm(x_ref[...], g_ref[...]).astype(BF16)
    r = jnp.dot(y, w_ref[...], preferred_element_type=F32)
    n_qk = qk_ref.shape[0]
    for p in range(n_qk):
        qk_ref[p] = r[:, p * LANES:(p + 1) * LANES].astype(BF16)
    for p in range(vt_ref.shape[0]):
        vt_ref[p] = r[:, (n_qk + p) * LANES:(n_qk + p + 1) * LANES].T.astype(BF16)


def _qkv_call(x, g, w, n_v_planes):
    t, d = x.shape
    n_qk = w.shape[1] // LANES - n_v_planes
    tm = min(DENSE_ROW_TILE, t)
    return pl.pallas_call(
        _qkv_kernel,
        name="qkv",
        grid=(t // tm,),
        in_specs=[pl.BlockSpec((tm, d), lambda i: (i, 0)),
                  pl.BlockSpec((1, d), lambda i: (0, 0)),
                  pl.BlockSpec((d, w.shape[1]), lambda i: (0, 0))],
        out_specs=[pl.BlockSpec((n_qk, tm, LANES), lambda i: (0, i, 0)),
                   pl.BlockSpec((n_v_planes, LANES, tm), lambda i: (0, 0, i))],
        out_shape=[jax.ShapeDtypeStruct((n_qk, t, LANES), BF16),
                   jax.ShapeDtypeStruct((n_v_planes, LANES, t), BF16)],
        compiler_params=_cparams("parallel"),
    )(x, g, w)


def _edge_case(u, n_units):
    i = pl.program_id(2)
    last = pl.num_programs(2) - 1
    if n_units == 1:
        return jnp.where(i == 0, 1, jnp.where(i == last, 2, 0))
    if u == 0:
        return jnp.where(i == 0, 1, 0)
    if u == n_units - 1:
        return jnp.where(i == last, 2, 0)
    return 0


def _attn_kernel(*refs, unit, n_units, has_sink):
    q_ref, kp_ref, kc_ref, kn_ref, vp_ref, vc_ref, vn_ref, bias_ref = refs[:8]
    sink_ref = refs[8] if has_sink else None
    o_ref = refs[-1]
    n_planes = q_ref.shape[0]
    kcat = jnp.concatenate([kp_ref[0], kc_ref[0], kn_ref[0]], axis=0)
    vcat = jnp.concatenate([vp_ref[0], vc_ref[0], vn_ref[0]], axis=1)
    vcat = jnp.concatenate([vcat, jnp.ones((SUBLANES, vcat.shape[1]), BF16)], axis=0)
    lo_half = lax.broadcasted_iota(I32, (unit, LANES), 1) < HEAD_DIM
    zero = jnp.zeros((unit, LANES), BF16)

    def scores(u):
        kw = kcat[u * unit:(u + 3) * unit]
        qs = [q_ref[pi, u * unit:(u + 1) * unit, :] for pi in range(n_planes)]
        q_stack = jnp.concatenate(
            [jnp.where(lo_half if hh == 0 else jnp.logical_not(lo_half), q, zero) for q in qs for hh in range(2)],
            axis=0)
        return lax.dot_general(kw, q_stack, (((1,), (1,)), ((), ())), preferred_element_type=F32)

    st_next = scores(0)
    for u in range(n_units):
        case = _edge_case(u, n_units)
        vw = vcat[:, u * unit:(u + 3) * unit]
        st = st_next + bias_ref[0, case]
        if u + 1 < n_units:
            st_next = scores(u + 1)
        m = jnp.max(st, axis=0, keepdims=True)
        if has_sink:
            sink = sink_ref[0, 0:1, :]
            m = jnp.maximum(m, sink)
        e = jnp.exp2(st - m)
        ot = jnp.dot(vw, e.astype(BF16), preferred_element_type=F32)
        l = ot[LANES:LANES + 1, :]
        if has_sink:
            l = l + jnp.exp2(sink - m)
        ot = ot[0:LANES, :] / l
        for pi in range(n_planes):
            both = jnp.concatenate(
                [ot[0:HEAD_DIM, (2 * pi) * unit:(2 * pi + 1) * unit],
                 ot[HEAD_DIM:LANES, (2 * pi + 1) * unit:(2 * pi + 2) * unit]], axis=0)
            o_ref[pi, u * unit:(u + 1) * unit, :] = both.T.astype(BF16)


def _attn_call(name, qk, vt, bias, sink, batch, seq, unit, n_units, n_groups):
    t = batch * seq
    units = seq // unit
    n_units = min(n_units, units)
    steps = units // n_units
    assert seq % (unit * n_units) == 0 and units >= 4
    ppg = N_PLANES // n_groups
    qb = unit * n_units
    width = 2 * ppg * unit

    def cur(g, b, i):
        return b * steps + i

    def prev(g, b, i):
        return b * units + jnp.maximum(n_units * i - 1, 0)

    def nxt(g, b, i):
        return b * units + jnp.minimum(n_units * i + n_units, units - 1)

    in_specs = [pl.BlockSpec((ppg, qb, LANES), lambda g, b, i: (g, cur(g, b, i), 0)),
                pl.BlockSpec((1, unit, LANES), lambda g, b, i: (N_PLANES + g, prev(g, b, i), 0)),
                pl.BlockSpec((1, qb, LANES), lambda g, b, i: (N_PLANES + g, cur(g, b, i), 0)),
                pl.BlockSpec((1, unit, LANES), lambda g, b, i: (N_PLANES + g, nxt(g, b, i), 0)),
                pl.BlockSpec((1, LANES, unit), lambda g, b, i: (g, 0, prev(g, b, i))),
                pl.BlockSpec((1, LANES, qb), lambda g, b, i: (g, 0, cur(g, b, i))),
                pl.BlockSpec((1, LANES, unit), lambda g, b, i: (g, 0, nxt(g, b, i))),
                pl.BlockSpec((1, 3, 3 * unit, width), lambda g, b, i: (g, 0, 0, 0))]
    args = [qk, qk, qk, qk, vt, vt, vt, bias]
    if sink is not None:
        in_specs.append(pl.BlockSpec((1, SUBLANES, width), lambda g, b, i: (g, 0, 0)))
        args.append(sink)
    return pl.pallas_call(
        functools.partial(_attn_kernel, unit=unit, n_units=n_units, has_sink=sink is not None),
        name=name,
        grid=(n_groups, batch, steps),
        in_specs=in_specs,
        out_specs=pl.BlockSpec((ppg, qb, LANES), lambda g, b, i: (g, cur(g, b, i), 0)),
        out_shape=jax.ShapeDtypeStruct((N_PLANES, t, LANES), BF16),
        compiler_params=_cparams("arbitrary", "arbitrary", "arbitrary"),
    )(*args)


def _na_bias_table(rpb):
    a = np.arange(NA_UNIT) // GRID_W
    j = np.arange(3 * NA_UNIT) // GRID_W
    row_ok = np.stack([
        (j[None, :] >= a[:, None]) & (j[None, :] < a[:, None] + NA_KH),
        np.broadcast_to((j[None, :] >= 4) & (j[None, :] < 4 + NA_KH), (NA_UNIT, 3 * NA_UNIT)),
        np.broadcast_to(j[None, :] < NA_KH, (NA_UNIT, 3 * NA_UNIT)),
    ])
    w = np.arange(GRID_W)
    dc = np.clip(w[None, :] - w[:, None], -(NA_KW - 1), NA_KW - 1) + NA_KW - 1
    c0 = np.clip(w - NA_KW // 2, 0, GRID_W - NA_KW)
    col_ok = (w[None, :] >= c0[:, None]) & (w[None, :] < c0[:, None] + NA_KW)
    cols = jnp.take(rpb.astype(F32), jnp.asarray(dc.reshape(-1)), axis=2)
    cols = cols.reshape(N_HEADS, 2 * NA_KH - 1, GRID_W, GRID_W)
    cols = jnp.where(jnp.asarray(col_ok), cols, NEG)
    rows = jnp.stack([cols[:, 3 - ai:15 - ai] for ai in range(4)], axis=1)
    tbl = rows.transpose(0, 2, 4, 1, 3).reshape(N_HEADS, 3 * NA_UNIT, NA_UNIT)
    ok = jnp.asarray(row_ok.transpose(0, 2, 1))
    full = jnp.where(ok[:, None], tbl[None], NEG)
    full = full.reshape(3, N_PLANES, 2, 3 * NA_UNIT, NA_UNIT).transpose(1, 0, 3, 2, 4)
    return full.reshape(N_PLANES, 3, 3 * NA_UNIT, 2 * NA_UNIT)


SW_HEAD_ORDER = np.array([8 * m + 4 * hh + i for m in range(2) for i in range(4) for hh in range(2)])


def _sw_bias_table():
    slopes = np.asarray(2.0 ** (-8.0 * np.arange(1, N_HEADS + 1) / N_HEADS), dtype=np.float32)
    ks = np.arange(3 * SW_UNIT) - SW_WINDOW
    dist = ks[None, :] - np.arange(SW_UNIT)[:, None]
    win_ok = np.abs(dist) <= SW_WINDOW
    pos_ok = np.stack([np.ones(3 * SW_UNIT, bool), ks >= 0, ks < SW_UNIT])
    ok = win_ok[None] & pos_ok[:, None, :]
    alibi = -slopes[SW_HEAD_ORDER][:, None, None] * np.abs(dist).astype(np.float32)[None]
    alibi = alibi * np.float32(LOG2E)
    full = np.where(ok[:, None], alibi[None], np.float32(NEG)).astype(np.float32)
    full = full.reshape(3, 2, 8, SW_UNIT, 3 * SW_UNIT).transpose(1, 0, 4, 2, 3)
    return jnp.asarray(full.reshape(2, 3, 3 * SW_UNIT, 8 * SW_UNIT))


def _post_kernel(o_ref, x_ref, wo_ref, g_ref, wrh_ref, wrhl_ref, x1_ref, h_ref, aff_ref, afft_ref):
    oc = jnp.concatenate([o_ref[p] for p in range(N_PLANES)], axis=1)
    x1 = x_ref[...] + jnp.dot(oc, wo_ref[...], preferred_element_type=F32)
    x1_ref[...] = x1
    h = _rmsnorm(x1, g_ref[...])
    hb = h.astype(BF16)
    h_ref[...] = hb
    hl = (h - hb.astype(F32)).astype(BF16)
    nt = (((1,), (1,)), ((), ()))
    by_hb = lax.dot_general(wrhl_ref[...], hb, nt, preferred_element_type=F32)
    by_hl = lax.dot_general(wrh_ref[...], hl, nt, preferred_element_type=F32)
    logits = by_hb[0:N_EXPERTS] + (by_hb[N_EXPERTS:2 * N_EXPERTS] + by_hl)
    ex = jnp.exp(logits - jnp.max(logits, axis=0, keepdims=True))
    afft = ex / jnp.sum(ex, axis=0, keepdims=True)
    afft_ref[...] = afft
    pad = jnp.zeros((LANES - N_EXPERTS, afft.shape[1]), F32)
    aff_ref[...] = jnp.concatenate([afft, pad], axis=0).T


def _post_call(o, x, wo, g, wrh, wrhl):
    t, d = x.shape
    tm = min(DENSE_ROW_TILE, t)
    return pl.pallas_call(
        _post_kernel,
        name="post",
        grid=(t // tm,),
        in_specs=[pl.BlockSpec((N_PLANES, tm, LANES), lambda i: (0, i, 0)),
                  pl.BlockSpec((tm, d), lambda i: (i, 0)),
                  pl.BlockSpec((d, d), lambda i: (0, 0)),
                  pl.BlockSpec((1, d), lambda i: (0, 0)),
                  pl.BlockSpec((N_EXPERTS, d), lambda i: (0, 0)),
                  pl.BlockSpec((2 * N_EXPERTS, d), lambda i: (0, 0))],
        out_specs=[pl.BlockSpec((tm, d), lambda i: (i, 0)),
                   pl.BlockSpec((tm, d), lambda i: (i, 0)),
                   pl.BlockSpec((tm, LANES), lambda i: (i, 0)),
                   pl.BlockSpec((N_EXPERTS, tm), lambda i: (0, i))],
        out_shape=[jax.ShapeDtypeStruct((t, d), F32),
                   jax.ShapeDtypeStruct((t, d), BF16),
                   jax.ShapeDtypeStruct((t, LANES), F32),
                   jax.ShapeDtypeStruct((N_EXPERTS, t), F32)],
        compiler_params=_cparams("parallel"),
    )(o, x, wo, g, wrh, wrhl)


def _threshold_kernel(aff_ref, thr_ref, need_ref, *, cap):
    bits = lax.bitcast_convert_type(aff_ref[...], I32)

    def body(k, cur):
        cand = cur | jnp.left_shift(jnp.int32(1), 30 - k)
        cnt = jnp.sum(jnp.where(bits >= cand, 1, 0), axis=1, keepdims=True)
        return jnp.where(cnt >= cap, cand, cur)

    cur = lax.fori_loop(0, 31, body, jnp.zeros((N_EXPERTS, 1), I32))
    above = jnp.sum(jnp.where(bits > cur, 1, 0), axis=1, keepdims=True)
    thr_ref[...] = jnp.broadcast_to(cur, thr_ref.shape)
    need_ref[...] = jnp.broadcast_to(cap - above, need_ref.shape)


def _threshold_call(afft, cap):
    e, t = afft.shape
    return pl.pallas_call(
        functools.partial(_threshold_kernel, cap=cap),
        name="threshold",
        grid=(1,),
        in_specs=[pl.BlockSpec((e, t), lambda i: (0, 0))],
        out_specs=[pl.BlockSpec((e, LANES), lambda i: (0, 0)),
                   pl.BlockSpec((e, LANES), lambda i: (0, 0))],
        out_shape=[jax.ShapeDtypeStruct((e, LANES), I32),
                   jax.ShapeDtypeStruct((e, LANES), I32)],
        compiler_params=_cparams("arbitrary"),
    )(afft)


def _positions_kernel(aff_ref, thr_ref, need_ref, lpos_ref, lpost_ref, starts_ref, npass_ref, base_sc, eqb_sc):
    @pl.when(pl.program_id(0) == 0)
    def _():
        base_sc[...] = jnp.zeros_like(base_sc)
        eqb_sc[...] = jnp.zeros_like(eqb_sc)

    thr = thr_ref[:, 0:1]
    need = need_ref[:, 0:1].astype(F32)
    r = lax.broadcasted_iota(I32, (ROUTE_TILE, ROUTE_TILE), 0)
    c = lax.broadcasted_iota(I32, (ROUTE_TILE, ROUTE_TILE), 1)
    before = jnp.where(r < c, 1.0, 0.0).astype(BF16)
    pad = jnp.full((LANES - N_EXPERTS, ROUTE_TILE), -1.0, F32)
    base = base_sc[...]
    eqb = eqb_sc[...]
    for k in range(starts_ref.shape[0]):
        cols = slice(k * ROUTE_TILE, (k + 1) * ROUTE_TILE)
        bits = lax.bitcast_convert_type(aff_ref[:, cols], I32)
        eq = bits == thr
        eqf = jnp.where(eq, 1.0, 0.0)
        eq_rank = eqb[:, 0:1] + jnp.dot(eqf.astype(BF16), before, preferred_element_type=F32)
        sel = (bits > thr) | (eq & (eq_rank < need))
        self_ = jnp.where(sel, 1.0, 0.0)
        rank = jnp.dot(self_.astype(BF16), before, preferred_element_type=F32)
        cnt = jnp.sum(self_, axis=1, keepdims=True)
        lpos = jnp.where(sel, rank, -1.0)
        lpos_ref[:, cols] = lpos.astype(I32)
        lpost_ref[cols, :] = jnp.concatenate([lpos, pad], axis=0).T
        starts_ref[k] = base.astype(I32)
        most = jnp.max(cnt, axis=0, keepdims=True)
        passes = sum(jnp.where(most > float(first), 1.0, 0.0) for first in range(0, ROUTE_TILE, SLAB))
        npass_ref[k] = jnp.broadcast_to(passes, npass_ref.shape[1:]).astype(I32)
        base = base + cnt
        eqb = eqb + jnp.sum(eqf, axis=1, keepdims=True)
    base_sc[...] = base
    eqb_sc[...] = eqb


def _positions_call(afft, thr, need):
    e, t = afft.shape
    n_tiles = t // ROUTE_TILE
    per_step = min(POSITION_TILES_PER_STEP, n_tiles)
    assert n_tiles % per_step == 0
    span = per_step * ROUTE_TILE
    return pl.pallas_call(
        _positions_kernel,
        name="positions",
        grid=(n_tiles // per_step,),
        in_specs=[pl.BlockSpec((e, span), lambda i: (0, i)),
                  pl.BlockSpec((e, LANES), lambda i: (0, 0)),
                  pl.BlockSpec((e, LANES), lambda i: (0, 0))],
        out_specs=[pl.BlockSpec((e, span), lambda i: (0, i)),
                   pl.BlockSpec((span, LANES), lambda i: (i, 0)),
                   pl.BlockSpec((per_step, e, LANES), lambda i: (i, 0, 0)),
                   pl.BlockSpec((per_step, SUBLANES, LANES), lambda i: (i, 0, 0))],
        out_shape=[jax.ShapeDtypeStruct((e, t), I32),
                   jax.ShapeDtypeStruct((t, LANES), F32),
                   jax.ShapeDtypeStruct((n_tiles, e, LANES), I32),
                   jax.ShapeDtypeStruct((n_tiles, SUBLANES, LANES), I32)],
        scratch_shapes=[pltpu.VMEM((e, LANES), F32), pltpu.VMEM((e, LANES), F32)],
        compiler_params=_cparams("arbitrary"),
    )(afft, thr, need)


SLAB_ROWS = SLAB * COL_BLOCKS
READ_SLAB = SLAB + SUBLANES


def _dispatch_kernel(starts_sm, npass_sm, h_ref, lpos_ref, xe_ref, slab_sc, sem, state_sm, *, cap):
    i = pl.program_id(0)
    sub = lax.broadcasted_iota(I32, (SLAB, ROUTE_TILE), 0)

    def slab_copy(buf, e, dst):
        return pltpu.make_async_copy(
            slab_sc.at[buf, pl.ds(e * SLAB_ROWS, SLAB_ROWS)],
            xe_ref.at[e, pl.ds(pl.multiple_of(dst * COL_BLOCKS, COL_BLOCKS), SLAB_ROWS)],
            sem.at[buf, e])

    def wait_all(buf):
        for e in range(N_EXPERTS):
            slab_copy(buf, e, 0).wait()

    @pl.when(i == 0)
    def _():
        slab_sc[0, pl.ds(0, SLAB_ROWS), :] = jnp.zeros((SLAB_ROWS, LANES), F32)
        pads = [pltpu.make_async_copy(slab_sc.at[0, pl.ds(0, SLAB_ROWS)],
                                      xe_ref.at[e, pl.ds(cap * COL_BLOCKS, SLAB_ROWS)], sem.at[0, e])
                for e in range(N_EXPERTS)]
        for cp in pads:
            cp.start()
        for cp in pads:
            cp.wait()
        state_sm[0] = 0
        state_sm[1] = 0

    def one_pass(s, carry):
        buf = state_sm[1]
        rows = []
        for e in range(N_EXPERTS):
            lp = lpos_ref[e:e + 1, :] - SLAB * s
            rows.append(jnp.where(lp == sub, 1.0, 0.0).astype(BF16))
        onehot = jnp.concatenate(rows, axis=0)
        res = jnp.dot(onehot, h_ref[...], preferred_element_type=F32)
        for cb in range(COL_BLOCKS):
            slab_sc[buf, pl.ds(cb, N_EXPERTS * SLAB, stride=COL_BLOCKS), :] = res[:, cb * LANES:(cb + 1) * LANES]

        @pl.when(state_sm[0] == 1)
        def _():
            wait_all(1 - buf)

        for e in range(N_EXPERTS):
            dst = jnp.minimum(starts_sm[i * N_EXPERTS + e] + SLAB * s, cap)
            slab_copy(buf, e, dst).start(priority=e % 2)
        state_sm[0] = 1
        state_sm[1] = 1 - buf
        return carry

    lax.fori_loop(0, npass_sm[i], one_pass, 0)

    @pl.when((i == pl.num_programs(0) - 1) & (state_sm[0] == 1))
    def _():
        wait_all(1 - state_sm[1])


def _dispatch_call(starts, npass, h, lpos, cap):
    t, d = h.shape
    n_tiles = t // ROUTE_TILE
    grid_spec = pltpu.PrefetchScalarGridSpec(
        num_scalar_prefetch=2,
        grid=(n_tiles,),
        in_specs=[pl.BlockSpec((ROUTE_TILE, d), lambda i, *_: (i, 0)),
                  pl.BlockSpec((N_EXPERTS, ROUTE_TILE), lambda i, *_: (0, i))],
        out_specs=pl.BlockSpec(memory_space=pl.ANY),
        scratch_shapes=[pltpu.VMEM((2, N_EXPERTS * SLAB_ROWS, LANES), F32),
                        pltpu.SemaphoreType.DMA((2, N_EXPERTS)),
                        pltpu.SMEM((2,), I32)],
    )
    return pl.pallas_call(
        functools.partial(_dispatch_kernel, cap=cap),
        name="dispatch",
        grid_spec=grid_spec,
        out_shape=jax.ShapeDtypeStruct((N_EXPERTS, (cap + SLAB) * COL_BLOCKS, LANES), F32),
        compiler_params=_cparams("arbitrary"),
    )(starts, npass, h, lpos)


def _expert_kernel(x_ref, wg_hbm, wu_hbm, wd_hbm, o_ref, wgb, wub, wdb, sg, su, sd, sem, *, layer, n_chunks):
    e = pl.program_id(0)
    j = pl.program_id(1)
    n_experts = pl.num_programs(0)
    d_rows = D_MODEL // n_chunks
    f_rows = EXPERT_FF // n_chunks

    def chunk_copies(ex, c):
        slot = (ex * n_chunks + c) % 2
        return [pltpu.make_async_copy(wg_hbm.at[layer, ex, pl.ds(c * d_rows, d_rows)], sg.at[slot], sem.at[0, slot]),
                pltpu.make_async_copy(wu_hbm.at[layer, ex, pl.ds(c * d_rows, d_rows)], su.at[slot], sem.at[1, slot]),
                pltpu.make_async_copy(wd_hbm.at[layer, ex, pl.ds(c * f_rows, f_rows)], sd.at[slot], sem.at[2, slot])]

    def start(ex, c):
        for cp in chunk_copies(ex, c):
            cp.start()

    def finish(ex, c):
        for cp in chunk_copies(ex, c):
            cp.wait()
        slot = (ex * n_chunks + c) % 2
        to = ex % 2
        wgb[to, pl.ds(c * d_rows, d_rows), :] = sg[slot].astype(BF16)
        wub[to, pl.ds(c * d_rows, d_rows), :] = su[slot].astype(BF16)
        wdb[to, pl.ds(c * f_rows, f_rows), :] = sd[slot].astype(BF16)

    @pl.when((e == 0) & (j == 0))
    def _():
        for c in range(n_chunks):
            start(0, c)
            finish(0, c)
        start(1, 0)

    nxt = (e + 1) % n_experts

    @pl.when(j + 1 < n_chunks)
    def _():
        start(nxt, j + 1)

    @pl.when((j + 1 == n_chunks) & (e + 1 < n_experts))
    def _():
        start((e + 2) % n_experts, 0)

    finish(nxt, j)

    cur = e % 2
    tm = x_ref.shape[1] // COL_BLOCKS
    x = jnp.concatenate([x_ref[0, pl.ds(cb, tm, stride=COL_BLOCKS), :] for cb in range(COL_BLOCKS)],
                        axis=1).astype(BF16)
    acc = jnp.zeros((tm, D_MODEL), F32)
    for fc in range(EXPERT_FF // FF_CHUNK):
        fs = slice(fc * FF_CHUNK, (fc + 1) * FF_CHUNK)
        g = jnp.dot(x, wgb[cur, :, fs], preferred_element_type=F32)
        u = jnp.dot(x, wub[cur, :, fs], preferred_element_type=F32)
        hmid = (g * (1.0 / (1.0 + jnp.exp(-g))) * u).astype(BF16)
        acc = acc + jnp.dot(hmid, wdb[cur, fs, :], preferred_element_type=F32)
    for cb in range(COL_BLOCKS):
        o_ref[0, cb] = acc[:, cb * LANES:(cb + 1) * LANES]


def _expert_call(xe, wg, wu, wd, layer, cap):
    tm = min(ROW_TILE, cap)
    n_chunks = cap // tm
    d, f = wg.shape[2], wg.shape[3]
    assert d % n_chunks == 0 and f % n_chunks == 0
    return pl.pallas_call(
        functools.partial(_expert_kernel, layer=layer, n_chunks=n_chunks),
        name="expert",
        grid=(N_EXPERTS, n_chunks),
        in_specs=[pl.BlockSpec((1, tm * COL_BLOCKS, LANES), lambda e, j: (e, j, 0)),
                  pl.BlockSpec(memory_space=pl.ANY),
                  pl.BlockSpec(memory_space=pl.ANY),
                  pl.BlockSpec(memory_space=pl.ANY)],
        out_specs=pl.BlockSpec((1, COL_BLOCKS, tm, LANES), lambda e, j: (e, 0, j, 0)),
        out_shape=jax.ShapeDtypeStruct((N_EXPERTS, COL_BLOCKS, cap, LANES), F32),
        scratch_shapes=[pltpu.VMEM((2, d, f), BF16), pltpu.VMEM((2, d, f), BF16), pltpu.VMEM((2, f, d), BF16),
                        pltpu.VMEM((2, d // n_chunks, f), F32), pltpu.VMEM((2, d // n_chunks, f), F32),
                        pltpu.VMEM((2, f // n_chunks, d), F32),
                        pltpu.SemaphoreType.DMA((3, 2))],
        compiler_params=_cparams("arbitrary", "arbitrary"),
    )(xe, wg, wu, wd)


def _combine_kernel(starts_sm, npass_sm, x_ref, aff_ref, lpost_ref, expand_ref, g_ref, ye_ref, o_ref,
                    slab_sc, sem, *, cap, final):
    i = pl.program_id(0)
    lane = lax.broadcasted_iota(I32, (1, LANES), 1)
    slot_in_slab = (lax.broadcasted_iota(I32, (1, N_EXPERTS * READ_SLAB), 1) % READ_SLAB).astype(F32)
    aff = aff_ref[...]
    ghi = aff.astype(BF16)
    glo = (aff - ghi.astype(F32)).astype(BF16)
    expand = expand_ref[...]
    lpost = lpost_ref[...]
    buf = i % 2

    def slab_src(tile, s, e):
        want = starts_sm[tile * N_EXPERTS + e] + SLAB * s
        src = jnp.minimum((want // SUBLANES) * SUBLANES, cap - READ_SLAB)
        return want, pl.multiple_of(src, SUBLANES)

    def slab_copy(tile, s, to, e):
        src = slab_src(tile, s, e)[1]
        return pltpu.make_async_copy(
            ye_ref.at[e, :, pl.ds(src, READ_SLAB), :],
            slab_sc.at[to, :, pl.ds(e * READ_SLAB, READ_SLAB), :],
            sem.at[to, e])

    def fetch(tile, s, to):
        for e in range(N_EXPERTS):
            slab_copy(tile, s, to, e).start(priority=e % 2)

    def slab_row(s):
        shift = jnp.zeros((1, LANES), F32)
        for e in range(N_EXPERTS):
            want, src = slab_src(i, s, e)
            shift = jnp.where(lane == e, (want - src).astype(F32), shift)
        lo = jnp.asarray(SLAB * s, F32)
        owned = (lpost >= lo) & (lpost < lo + SLAB)
        return jnp.where(owned, lpost - lo + shift, -1.0).astype(BF16)

    spread = jnp.dot(jnp.concatenate([ghi, glo, slab_row(0)], axis=0), expand, preferred_element_type=F32)
    ghi_x = spread[0:ROUTE_TILE]
    glo_x = spread[ROUTE_TILE:2 * ROUTE_TILE]

    def add_pass(s, row_x, acc):
        match = row_x == slot_in_slab
        w = jnp.concatenate([jnp.where(match, ghi_x, 0.0).astype(BF16),
                             jnp.where(match, glo_x, 0.0).astype(BF16)], axis=0)
        for e in range(N_EXPERTS):
            slab_copy(i, s, buf, e).wait()
        rows = jnp.concatenate([slab_sc[buf, cb] for cb in range(COL_BLOCKS)],
                               axis=1).astype(BF16)
        both = jnp.dot(w, rows, preferred_element_type=F32)
        return acc + (both[0:ROUTE_TILE] + both[ROUTE_TILE:2 * ROUTE_TILE])

    @pl.when(i == 0)
    def _():
        fetch(0, 0, 0)

    @pl.when(i + 1 < pl.num_programs(0))
    def _():
        fetch(i + 1, 0, 1 - buf)

    y = add_pass(0, spread[2 * ROUTE_TILE:3 * ROUTE_TILE], jnp.zeros(x_ref.shape, F32))

    def extra_pass(s, acc):
        fetch(i, s, buf)
        return add_pass(s, jnp.dot(slab_row(s), expand, preferred_element_type=F32), acc)

    y = lax.fori_loop(1, npass_sm[i], extra_pass, y)
    out = x_ref[...] + y
    if final:
        out = _rmsnorm(out, g_ref[...])
    o_ref[...] = out


def _combine_call(starts, npass, x1, aff, lpost, expand, g, ye, cap, final):
    t, d = x1.shape
    n_tiles = t // ROUTE_TILE
    grid_spec = pltpu.PrefetchScalarGridSpec(
        num_scalar_prefetch=2,
        grid=(n_tiles,),
        in_specs=[pl.BlockSpec((ROUTE_TILE, d), lambda i, *_: (i, 0)),
                  pl.BlockSpec((ROUTE_TILE, LANES), lambda i, *_: (i, 0)),
                  pl.BlockSpec((ROUTE_TILE, LANES), lambda i, *_: (i, 0)),
                  pl.BlockSpec((LANES, N_EXPERTS * READ_SLAB), lambda i, *_: (0, 0)),
                  pl.BlockSpec((1, d), lambda i, *_: (0, 0)),
                  pl.BlockSpec(memory_space=pl.ANY)],
        out_specs=pl.BlockSpec((ROUTE_TILE, d), lambda i, *_: (i, 0)),
        scratch_shapes=[pltpu.VMEM((2, COL_BLOCKS, N_EXPERTS * READ_SLAB, LANES), F32),
                        pltpu.SemaphoreType.DMA((2, N_EXPERTS))],
    )
    return pl.pallas_call(
        functools.partial(_combine_kernel, cap=cap, final=final),
        name="combine",
        grid_spec=grid_spec,
        out_shape=jax.ShapeDtypeStruct((t, d), F32),
        compiler_params=_cparams("arbitrary"),
    )(starts, npass, x1, aff, lpost, expand, g, ye)


def _moe(x1, h, aff, afft, wg, wu, wd, layer, expand, g_final, final):
    t = x1.shape[0]
    cap = 2 * t // N_EXPERTS
    assert t % ROUTE_TILE == 0 and cap >= READ_SLAB and cap % SUBLANES == 0
    thr, need = _threshold_call(afft, cap)
    lpos, lpost, starts3, npass3 = _positions_call(afft, thr, need)
    starts = starts3[:, :, 0].reshape(-1)
    npass = npass3[:, 0, 0]
    xe = _dispatch_call(starts, npass, h, lpos, cap)
    ye = _expert_call(xe, wg, wu, wd, layer, cap)
    return _combine_call(starts, npass, x1, aff, lpost, expand, g_final, ye, cap, final)


def _trunk(x, p):
    batch, seq, d = x.shape
    x = x.reshape(batch * seq, d)
    depth = p["norm_mix"].shape[0]
    for i in range(depth):
        j = i // 2
        if i % 2 == 0:
            qk, vt = _qkv_call(x, p["norm_mix"][i:i + 1], p["a_wqkv"][j], N_PLANES)
            o = _attn_call("na_attn", qk, vt, p["a_bias"][j], None, batch, seq,
                           NA_UNIT, NA_UNITS_PER_STEP, N_PLANES)
            wo = p["a_wo"][j]
        else:
            qk, vt = _qkv_call(x, p["norm_mix"][i:i + 1], p["b_wqkv"][j], 2)
            o = _attn_call("sw_attn", qk, vt, p["b_bias"], p["b_sink"][j], batch, seq,
                           SW_UNIT, SW_UNITS_PER_STEP, 2)
            wo = p["b_wo"][j]
        x1, h, aff, afft = _post_call(o, x, wo, p["norm_ffn"][i:i + 1], p["wr_hi"][i], p["wr_hl"][i])
        x = _moe(x1, h, aff, afft, p["w_gate"], p["w_up"], p["w_down"], i,
                 p["expand"], p["norm_final"], i == depth - 1)
    return x.reshape(batch, seq, d)


def _prepare(norm_mix, norm_ffn, norm_final, a_wqkv, a_wo, a_rpb, b_wqkv, b_wo, b_sink,
             w_router, w_gate, w_up, w_down):
    d = D_MODEL
    scale = HEAD_DIM ** -0.5 * LOG2E
    a_w = jnp.concatenate([a_wqkv[:, :, :d] * scale, a_wqkv[:, :, d:]], axis=2).astype(BF16)
    order = SW_HEAD_ORDER
    nb = b_wqkv.shape[0]
    bq = (b_wqkv[:, :, :d] * scale).reshape(nb, d, N_HEADS, HEAD_DIM)[:, :, order].reshape(nb, d, d)
    b_w = jnp.concatenate([bq, b_wqkv[:, :, d:]], axis=2).astype(BF16)
    b_wo_p = b_wo.reshape(nb, N_HEADS, HEAD_DIM, d)[:, order].reshape(nb, d, d).astype(BF16)
    sink = (b_sink.astype(F32) * LOG2E)[:, order].reshape(nb, 2, 1, 8, 1)
    sink = jnp.broadcast_to(sink, (nb, 2, SUBLANES, 8, SW_UNIT)).reshape(nb, 2, SUBLANES, 8 * SW_UNIT)
    wr = w_router.astype(F32).transpose(0, 2, 1)
    wr_hi = wr.astype(BF16)
    wr_lo = (wr - wr_hi.astype(F32)).astype(BF16)
    wr_hl = jnp.concatenate([wr_hi, wr_lo], axis=1)
    lane_expert = np.arange(N_EXPERTS * READ_SLAB) // READ_SLAB
    expand = jnp.asarray(np.arange(LANES)[:, None] == lane_expert[None, :], dtype=BF16)
    return {
        "norm_mix": norm_mix.astype(F32), "norm_ffn": norm_ffn.astype(F32),
        "norm_final": norm_final.astype(F32).reshape(1, d),
        "a_wqkv": a_w, "a_wo": a_wo.astype(BF16),
        "a_bias": jnp.stack([_na_bias_table(a_rpb[j] * LOG2E) for j in range(a_rpb.shape[0])]),
        "b_wqkv": b_w, "b_wo": b_wo_p, "b_bias": _sw_bias_table(), "b_sink": sink,
        "wr_hi": wr_hi, "wr_hl": wr_hl,
        "w_gate": w_gate.astype(F32), "w_up": w_up.astype(F32), "w_down": w_down.astype(F32),
        "expand": expand,
    }


def kernel(x_prompt, x_sample, norm_mix, norm_ffn, norm_final, a_wqkv, a_wo, a_rpb, b_wqkv, b_wo, b_sink,
           w_router, w_gate, w_up, w_down):
    p = _prepare(norm_mix, norm_ffn, norm_final, a_wqkv, a_wo, a_rpb, b_wqkv, b_wo, b_sink,
                 w_router, w_gate, w_up, w_down)
    return (_trunk(x_prompt, p), _trunk(x_sample, p))
```
